```python
import jax, jax.numpy as jnp
from jax import lax
import numpy as np

D_MODEL = 2048
BATCH = 4
SEQ = 2048
DEPTH = 2
DEC_BATCH = 16
DEC_SEQ = 32
PAST_LEN = 1024

CHUNK = 64
LEFT_CHUNKS = 8
BAND_PAST = CHUNK * LEFT_CHUNKS
N_HEADS_A = 8
HEAD_DIM = 128
D_A = N_HEADS_A * HEAD_DIM
REL_CLIP = 128
D_B = 1024
N_BLOCKS_B = 8
BLOCK_B = D_B // N_BLOCKS_B
CONV_B = 4
LRU_C = 8.0
D_C = D_MODEL
CONV_C = 31
IN_AB = 3 * D_A + D_A + D_B + D_B
IN_CV = 3 * D_C
N_EVEN = (DEPTH + 1) // 2
N_ODD = DEPTH // 2
EPS = 1e-6

kernel_name = "chunk_band_attn_rglru_conformer_conv_stream"


def _rmsnorm(x, g):
    xf = x.astype(jnp.float32)
    y = xf * lax.rsqrt(jnp.mean(xf * xf, axis=-1, keepdims=True) + EPS) * g.astype(jnp.float32)
    return y.astype(x.dtype)


def _layernorm(x, g, b):
    xf = x.astype(jnp.float32)
    mu = jnp.mean(xf, axis=-1, keepdims=True)
    var = jnp.mean(jnp.square(xf - mu), axis=-1, keepdims=True)
    y = (xf - mu) * lax.rsqrt(var + EPS) * g.astype(jnp.float32) + b.astype(jnp.float32)
    return y.astype(x.dtype)


def _causal_dwconv(x, buf, w, b):
    W = w.shape[0]
    C = x.shape[-1]
    xp = jnp.concatenate([buf.astype(x.dtype), x], axis=1)
    y = lax.conv_general_dilated(xp, w[:, None, :].astype(x.dtype), window_strides=(1,), padding='VALID',
                                 dimension_numbers=('NWC', 'WIO', 'NWC'), feature_group_count=C)
    return y + b.astype(x.dtype), xp[:, xp.shape[1] - (W - 1):]


def _band_softmax(qb, kb, vb, q_pos, k_pos, rel_table):
    s = jnp.einsum('bnqhd,bnkhd->bnhqk', qb, kb, preferred_element_type=jnp.float32) * (HEAD_DIM ** -0.5)
    rel = jnp.clip(q_pos[:, :, None] - k_pos[:, None, :], -REL_CLIP, REL_CLIP) + REL_CLIP
    bias = jnp.moveaxis(rel_table[:, rel], 0, 1).astype(jnp.float32)
    s = s + bias[None]
    s = jnp.where((k_pos >= 0)[None, :, None, None, :], s, -jnp.inf)
    p = jax.nn.softmax(s, axis=-1).astype(vb.dtype)
    return jnp.einsum('bnhqk,bnkhd->bnqhd', p, vb)


def _attn_prompt(q, k, v, rel_table):
    B, S, H, Dh = q.shape
    nc = S // CHUNK
    pad = jnp.zeros((B, BAND_PAST, H, Dh), k.dtype)
    kp = jnp.concatenate([pad, k], axis=1).reshape(B, nc + LEFT_CHUNKS, CHUNK, H, Dh)
    vp = jnp.concatenate([pad, v], axis=1).reshape(B, nc + LEFT_CHUNKS, CHUNK, H, Dh)
    idx = jnp.arange(nc)[:, None] + jnp.arange(LEFT_CHUNKS + 1)[None, :]
    band = (LEFT_CHUNKS + 1) * CHUNK
    kb = kp[:, idx].reshape(B, nc, band, H, Dh)
    vb = vp[:, idx].reshape(B, nc, band, H, Dh)
    qb = q.reshape(B, nc, CHUNK, H, Dh)
    q_pos = jnp.arange(S).reshape(nc, CHUNK)
    k_pos = (jnp.arange(nc) * CHUNK - BAND_PAST)[:, None] + jnp.arange(band)[None, :]
    o = _band_softmax(qb, kb, vb, q_pos, k_pos, rel_table)
    return o.reshape(B, S, H * Dh)


def _attn_sample(q, k, v, k_cache, v_cache, rel_table):
    B, T, H, Dh = q.shape
    L = k_cache.shape[1]
    kb = jnp.concatenate([k_cache.astype(k.dtype), k], axis=1)[:, None]
    vb = jnp.concatenate([v_cache.astype(v.dtype), v], axis=1)[:, None]
    q_pos = (PAST_LEN + jnp.arange(T))[None]
    k_pos = (PAST_LEN - L + jnp.arange(L + T))[None]
    o = _band_softmax(q[:, None], kb, vb, q_pos, k_pos, rel_table)
    return o.reshape(B, T, H * Dh)


def _rg_lru(x, h0, w_a, b_a, w_x, b_x, lam):
    B, T, _ = x.shape
    xb = x.reshape(B, T, N_BLOCKS_B, BLOCK_B)
    r = jax.nn.sigmoid(jnp.einsum('btnd,nde->btne', xb, w_a).reshape(B, T, D_B) + b_a).astype(jnp.float32)
    i = jax.nn.sigmoid(jnp.einsum('btnd,nde->btne', xb, w_x).reshape(B, T, D_B) + b_x).astype(jnp.float32)
    log_a = LRU_C * r * jax.nn.log_sigmoid(lam.astype(jnp.float32))
    a = jnp.exp(log_a)
    bt = jnp.sqrt(-jnp.expm1(2.0 * log_a)) * (i * x.astype(jnp.float32))
    bt = bt.at[:, 0].add(a[:, 0] * h0.astype(jnp.float32))

    def comb(left, right):
        a1, b1 = left
        a2, b2 = right
        return a1 * a2, a2 * b1 + b2

    _, h = lax.associative_scan(comb, (a, bt), axis=1)
    return h, h[:, -1]


def _layer_ab(x, k_cache, v_cache, h0, lb0, g_norm, w_in, w_out, rel_table, cw, cb, w_a, b_a, w_x, b_x, lam):
    B, T, _ = x.shape
    u = _rmsnorm(x, g_norm) @ w_in
    q, k, v, g_a, xb, g_b = jnp.split(u, [D_A, 2 * D_A, 3 * D_A, 4 * D_A, 4 * D_A + D_B], axis=-1)
    q = q.reshape(B, T, N_HEADS_A, HEAD_DIM)
    k = k.reshape(B, T, N_HEADS_A, HEAD_DIM)
    v = v.reshape(B, T, N_HEADS_A, HEAD_DIM)
    if k_cache is None:
        o_a = _attn_prompt(q, k, v, rel_table)
        keep = min(BAND_PAST, T)
        new_k, new_v = k[:, T - keep:], v[:, T - keep:]
    else:
        o_a = _attn_sample(q, k, v, k_cache, v_cache, rel_table)
        new_k, new_v = k, v
    xc, new_lb = _causal_dwconv(xb, lb0, cw, cb)
    h, h_last = _rg_lru(xc, h0, w_a, b_a, w_x, b_x, lam)
    mix = jnp.concatenate([jax.nn.silu(g_a) * o_a, jax.nn.silu(g_b) * h.astype(x.dtype)], axis=-1)
    return x + mix @ w_out, new_k, new_v, h_last.astype(x.dtype), new_lb


def _layer_conv(x, buf0, g_norm, w_in, w_out, dw_w, dw_b, ln_g, ln_b):
    u = _rmsnorm(x, g_norm) @ w_in
    val, glu_gate, gate = jnp.split(u, [D_C, 2 * D_C], axis=-1)
    z = val * jax.nn.sigmoid(glu_gate)
    zc, new_buf = _causal_dwconv(z, buf0, dw_w, dw_b)
    y = jax.nn.silu(_layernorm(zc, ln_g, ln_b)) * jax.nn.silu(gate)
    return x + y @ w_out, new_buf


def _trunk(x, kc, vc, hc, lbc, cbc, norm_ab, w_in_ab, w_out_ab, rel_bias, lru_conv_w, lru_conv_b,
           lru_w_a, lru_b_a, lru_w_x, lru_b_x, lru_lambda, norm_cv, w_in_cv, w_out_cv, dw_w, dw_b,
           ln_g, ln_b, final_norm):
    prompt = kc is None
    B = x.shape[0]
    ks, vs, hs, lbs, cbs = [], [], [], [], []
    for l in range(DEPTH):
        if l % 2 == 0:
            e = l // 2
            h0 = jnp.zeros((B, D_B), jnp.float32) if prompt else hc[e]
            lb0 = jnp.zeros((B, CONV_B - 1, D_B), x.dtype) if prompt else lbc[e]
            x, nk, nv, nh, nlb = _layer_ab(x, None if prompt else kc[e], None if prompt else vc[e], h0, lb0,
                                           norm_ab[e], w_in_ab[e], w_out_ab[e], rel_bias[e], lru_conv_w[e],
                                           lru_conv_b[e], lru_w_a[e], lru_b_a[e], lru_w_x[e], lru_b_x[e],
                                           lru_lambda[e])
            ks.append(nk); vs.append(nv); hs.append(nh); lbs.append(nlb)
        else:
            o = l // 2
            cb0 = jnp.zeros((B, CONV_C - 1, D_C), x.dtype) if prompt else cbc[o]
            x, ncb = _layer_conv(x, cb0, norm_cv[o], w_in_cv[o], w_out_cv[o], dw_w[o], dw_b[o], ln_g[o], ln_b[o])
            cbs.append(ncb)
    y = _rmsnorm(x, final_norm)
    return y, jnp.stack(ks), jnp.stack(vs), jnp.stack(hs), jnp.stack(lbs), jnp.stack(cbs)


def setup_inputs(seed: int = 0) -> dict:
    key = jax.random.key(seed)
    ks = jax.random.split(key, 32)
    f32 = jnp.float32
    nrm = lambda k, s, sc: jax.random.normal(k, s, f32) * sc
    cache_len = min(BAND_PAST, PAST_LEN)
    u = jax.random.uniform(ks[16], (N_EVEN, D_B), f32, 0.9, 0.999)
    base = u ** (1.0 / LRU_C)
    return {
        "x_prompt": nrm(ks[0], (BATCH, SEQ, D_MODEL), 1.0),
        "x_sample": nrm(ks[1], (DEC_BATCH, DEC_SEQ, D_MODEL), 1.0),
        "cache_attn_k": nrm(ks[2], (N_EVEN, DEC_BATCH, cache_len, N_HEADS_A, HEAD_DIM), 1.0),
        "cache_attn_v": nrm(ks[3], (N_EVEN, DEC_BATCH, cache_len, N_HEADS_A, HEAD_DIM), 1.0),
        "state_lru_h": nrm(ks[4], (N_EVEN, DEC_BATCH, D_B), 0.5),
        "state_lru_conv": nrm(ks[5], (N_EVEN, DEC_BATCH, CONV_B - 1, D_B), 1.0),
        "state_conv": nrm(ks[6], (N_ODD, DEC_BATCH, CONV_C - 1, D_C), 0.5),
        "norm_ab": 1.0 + nrm(ks[7], (N_EVEN, D_MODEL), 0.01),
        "w_in_ab": nrm(ks[8], (N_EVEN, D_MODEL, IN_AB), D_MODEL ** -0.5),
        "w_out_ab": nrm(ks[9], (N_EVEN, D_A + D_B, D_MODEL), (D_A + D_B) ** -0.5),
        "rel_bias": nrm(ks[10], (N_EVEN, N_HEADS_A, 2 * REL_CLIP + 1), 0.1),
        "lru_conv_w": nrm(ks[11], (N_EVEN, CONV_B, D_B), CONV_B ** -0.5),
        "lru_conv_b": nrm(ks[12], (N_EVEN, D_B), 0.01),
        "lru_w_a": nrm(ks[13], (N_EVEN, N_BLOCKS_B, BLOCK_B, BLOCK_B), BLOCK_B ** -0.5),
        "lru_b_a": nrm(ks[14], (N_EVEN, D_B), 0.01),
        "lru_w_x": nrm(ks[15], (N_EVEN, N_BLOCKS_B, BLOCK_B, BLOCK_B), BLOCK_B ** -0.5),
        "lru_b_x": nrm(ks[17], (N_EVEN, D_B), 0.01),
        "lru_lambda": jnp.log(base) - jnp.log1p(-base),
        "norm_cv": 1.0 + nrm(ks[18], (N_ODD, D_MODEL), 0.01),
        "w_in_cv": nrm(ks[19], (N_ODD, D_MODEL, IN_CV), D_MODEL ** -0.5),
        "w_out_cv": nrm(ks[20], (N_ODD, D_C, D_MODEL), D_C ** -0.5),
        "dw_w": nrm(ks[21], (N_ODD, CONV_C, D_C), CONV_C ** -0.5),
        "dw_b": nrm(ks[22], (N_ODD, D_C), 0.01),
        "ln_g": 1.0 + nrm(ks[23], (N_ODD, D_C), 0.01),
        "ln_b": nrm(ks[24], (N_ODD, D_C), 0.01),
        "final_norm": 1.0 + nrm(ks[25], (D_MODEL,), 0.01),
    }


def reference(x_prompt, x_sample, cache_attn_k, cache_attn_v, state_lru_h, state_lru_conv, state_conv,
              norm_ab, w_in_ab, w_out_ab, rel_bias, lru_conv_w, lru_conv_b, lru_w_a, lru_b_a, lru_w_x,
              lru_b_x, lru_lambda, norm_cv, w_in_cv, w_out_cv, dw_w, dw_b, ln_g, ln_b, final_norm):
    weights = (norm_ab, w_in_ab, w_out_ab, rel_bias, lru_conv_w, lru_conv_b, lru_w_a, lru_b_a, lru_w_x,
               lru_b_x, lru_lambda, norm_cv, w_in_cv, w_out_cv, dw_w, dw_b, ln_g, ln_b, final_norm)
    y_prompt, k_p, v_p, h_p, lb_p, cb_p = _trunk(x_prompt, None, None, None, None, None, *weights)
    y_sample, k_s, v_s, h_s, lb_s, cb_s = _trunk(x_sample, cache_attn_k, cache_attn_v, state_lru_h,
                                                 state_lru_conv, state_conv, *weights)
    return (y_prompt, y_sample, k_p, v_p, h_p, lb_p, cb_p, k_s, v_s, h_s, lb_s, cb_s)
```

```python
import functools

import jax
import jax.numpy as jnp
from jax import lax
from jax.experimental import pallas as pl
from jax.experimental.pallas import tpu as pltpu

D_MODEL = 2048
CHUNK = 64
LEFT_CHUNKS = 8
BAND_PAST = CHUNK * LEFT_CHUNKS
N_HEADS = 8
HEAD_DIM = 128
D_A = N_HEADS * HEAD_DIM
REL_CLIP = 128
D_B = 1024
N_BLOCKS_B = 8
BLOCK_B = D_B // N_BLOCKS_B
CONV_B = 4
LRU_C = 8.0
D_C = D_MODEL
CONV_C = 31
IN_AB = 4 * D_A + 2 * D_B
IN_CV = 3 * D_C
EPS = 1e-6

F32 = jnp.float32
BF16 = jnp.bfloat16

SUBLANES = 8
Q_TILE = 2 * CHUNK
K_WIN = BAND_PAST + Q_TILE
ATT_BLOCK = 512
CONV_HALO = 32
CONV_ROWS = 32
CONV_COLS = 512
LRU_HALO = 8
VMEM_LIMIT = 56 * 1024 * 1024


def _cparams(sem):
    return pltpu.CompilerParams(dimension_semantics=sem, vmem_limit_bytes=VMEM_LIMIT)


def _sigmoid(x):
    return jax.nn.sigmoid(x)


def _norm_to_bf16(x, g):
    ms = jnp.mean(x * x, axis=-1, keepdims=True)
    return (x * lax.rsqrt(ms + EPS) * g).astype(BF16)


def _norm_matmul_kernel(x_ref, g_ref, w_ref, o_ref, xn_ref):
    @pl.when(pl.program_id(1) == 0)
    def _():
        xn_ref[...] = _norm_to_bf16(x_ref[...], g_ref[...])

    o_ref[...] = jnp.dot(xn_ref[...], w_ref[...], preferred_element_type=F32)


def _norm_matmul(x, g, w, *, tm, tn):
    m, d = x.shape
    n = w.shape[1]
    return pl.pallas_call(
        _norm_matmul_kernel,
        grid=(m // tm, n // tn),
        in_specs=[
            pl.BlockSpec((tm, d), lambda i, j: (i, 0)),
            pl.BlockSpec((1, d), lambda i, j: (0, 0)),
            pl.BlockSpec((d, tn), lambda i, j: (0, j)),
        ],
        out_specs=pl.BlockSpec((tm, tn), lambda i, j: (i, j)),
        out_shape=jax.ShapeDtypeStruct((m, n), F32),
        scratch_shapes=[pltpu.VMEM((tm, d), BF16)],
        compiler_params=_cparams(("parallel", "arbitrary")),
        name="norm_in_proj",
    )(x, g, w)


def _norm_glu_kernel(x_ref, g_ref, wv_ref, wg_ref, wt_ref, z_ref, sg_ref, xn_ref):
    @pl.when(pl.program_id(1) == 0)
    def _():
        xn_ref[...] = _norm_to_bf16(x_ref[...], g_ref[...])

    xn = xn_ref[...]
    val = jnp.dot(xn, wv_ref[...], preferred_element_type=F32)
    glu = jnp.dot(xn, wg_ref[...], preferred_element_type=F32)
    gate = jnp.dot(xn, wt_ref[...], preferred_element_type=F32)
    z_ref[...] = val * _sigmoid(glu)
    sg_ref[...] = gate * _sigmoid(gate)


def _norm_glu(x, g, w, *, tm, tn):
    m, d = x.shape
    nb = D_C // tn
    return pl.pallas_call(
        _norm_glu_kernel,
        grid=(m // tm, nb),
        in_specs=[
            pl.BlockSpec((tm, d), lambda i, j: (i, 0)),
            pl.BlockSpec((1, d), lambda i, j: (0, 0)),
            pl.BlockSpec((d, tn), lambda i, j: (0, j)),
            pl.BlockSpec((d, tn), lambda i, j: (0, j + nb)),
            pl.BlockSpec((d, tn), lambda i, j: (0, j + 2 * nb)),
        ],
        out_specs=[
            pl.BlockSpec((tm, tn), lambda i, j: (i, j)),
            pl.BlockSpec((tm, tn), lambda i, j: (i, j)),
        ],
        out_shape=[jax.ShapeDtypeStruct((m, D_C), F32), jax.ShapeDtypeStruct((m, D_C), F32)],
        scratch_shapes=[pltpu.VMEM((tm, d), BF16)],
        compiler_params=_cparams(("parallel", "arbitrary")),
        name="norm_in_proj_glu",
    )(x, g, w, w, w)


def _attend(q, kw, vw, bias, kpos0):
    s = lax.dot_general(q, kw, (((1,), (1,)), ((), ())), preferred_element_type=F32)
    s = s * (HEAD_DIM ** -0.5) + bias
    if kpos0 is not None:
        col = lax.broadcasted_iota(jnp.int32, s.shape, 1)
        s = jnp.where(col + kpos0 >= 0, s, -jnp.inf)
    m = jnp.max(s, axis=-1, keepdims=True)
    p = jnp.exp(s - m)
    l = jnp.sum(p, axis=-1, keepdims=True)
    o = jnp.dot(p.astype(BF16), vw, preferred_element_type=F32)
    return o / l


def _attn_prompt_kernel(q_ref, kp_ref, kc_ref, vp_ref, vc_ref, bias_ref, o_ref, kcat_ref, vcat_ref):
    t = pl.program_id(1)
    kcat_ref[0:ATT_BLOCK, :] = kp_ref[...].astype(BF16)
    kcat_ref[ATT_BLOCK:, :] = kc_ref[...].astype(BF16)
    vcat_ref[0:ATT_BLOCK, :] = vp_ref[...].astype(BF16)
    vcat_ref[ATT_BLOCK:, :] = vc_ref[...].astype(BF16)
    for qi in range(ATT_BLOCK // Q_TILE):
        r0 = qi * Q_TILE
        kpos0 = t * ATT_BLOCK + r0 - BAND_PAST
        for h in range(N_HEADS):
            c0 = h * HEAD_DIM
            q = q_ref[r0:r0 + Q_TILE, c0:c0 + HEAD_DIM].astype(BF16)
            kw = kcat_ref[r0:r0 + K_WIN, c0:c0 + HEAD_DIM]
            vw = vcat_ref[r0:r0 + K_WIN, c0:c0 + HEAD_DIM]
            o_ref[r0:r0 + Q_TILE, c0:c0 + HEAD_DIM] = _attend(q, kw, vw, bias_ref[h], kpos0)


def _attn_prompt(u, bias):
    b, s, _ = u.shape
    blk = (None, ATT_BLOCK, D_A)
    prev = lambda col: (lambda bi, t: (bi, jnp.maximum(t - 1, 0), col))
    cur = lambda col: (lambda bi, t: (bi, t, col))
    return pl.pallas_call(
        _attn_prompt_kernel,
        grid=(b, s // ATT_BLOCK),
        in_specs=[
            pl.BlockSpec(blk, cur(0)),
            pl.BlockSpec(blk, prev(1)),
            pl.BlockSpec(blk, cur(1)),
            pl.BlockSpec(blk, prev(2)),
            pl.BlockSpec(blk, cur(2)),
            pl.BlockSpec(bias.shape, lambda bi, t: (0, 0, 0)),
        ],
        out_specs=pl.BlockSpec(blk, cur(0)),
        out_shape=jax.ShapeDtypeStruct((b, s, D_A), F32),
        scratch_shapes=[pltpu.VMEM((2 * ATT_BLOCK, D_A), BF16), pltpu.VMEM((2 * ATT_BLOCK, D_A), BF16)],
        compiler_params=_cparams(("parallel", "arbitrary")),
        name="attn_prompt",
    )(u, u, u, u, u, bias)


def _attn_sample_kernel(q_ref, kn_ref, vn_ref, kc_ref, vc_ref, bias_ref, o_ref, kcat_ref, vcat_ref):
    tq = q_ref.shape[0]
    lc = kc_ref.shape[0]
    kcat_ref[0:lc, :] = kc_ref[...].astype(BF16)
    vcat_ref[0:lc, :] = vc_ref[...].astype(BF16)
    kcat_ref[lc:lc + tq, :] = kn_ref[...].astype(BF16)
    vcat_ref[lc:lc + tq, :] = vn_ref[...].astype(BF16)
    kcat_ref[lc + tq:, :] = jnp.zeros((K_WIN - lc - tq, D_A), BF16)
    vcat_ref[lc + tq:, :] = jnp.zeros((K_WIN - lc - tq, D_A), BF16)
    for h in range(N_HEADS):
        c0 = h * HEAD_DIM
        q = q_ref[:, c0:c0 + HEAD_DIM].astype(BF16)
        kw = kcat_ref[:, c0:c0 + HEAD_DIM]
        vw = vcat_ref[:, c0:c0 + HEAD_DIM]
        o_ref[:, c0:c0 + HEAD_DIM] = _attend(q, kw, vw, bias_ref[h], None)


def _attn_sample(u, k_cache, v_cache, bias):
    b, t, _ = u.shape
    lc = k_cache.shape[1]
    new = lambda col: pl.BlockSpec((None, t, D_A), lambda bi: (bi, 0, col))
    cache = pl.BlockSpec((None, lc, D_A), lambda bi: (bi, 0, 0))
    return pl.pallas_call(
        _attn_sample_kernel,
        grid=(b,),
        in_specs=[new(0), new(1), new(2), cache, cache, pl.BlockSpec(bias.shape, lambda bi: (0, 0, 0))],
        out_specs=pl.BlockSpec((None, t, D_A), lambda bi: (bi, 0, 0)),
        out_shape=jax.ShapeDtypeStruct((b, t, D_A), F32),
        scratch_shapes=[pltpu.VMEM((K_WIN, D_A), BF16), pltpu.VMEM((K_WIN, D_A), BF16)],
        compiler_params=_cparams(("parallel",)),
        name="attn_sample",
    )(u, u, u, k_cache, v_cache, bias)


def _rel_bias_tiles(rel_table, tq, n_valid, chunk_masks):
    i = jnp.arange(tq)[:, None]
    j = jnp.arange(K_WIN)[None, :]
    idx = jnp.clip(i - j + BAND_PAST, -REL_CLIP, REL_CLIP) + REL_CLIP
    bias = rel_table[:, idx].astype(F32)
    hidden = j >= n_valid
    if chunk_masks:
        hidden = hidden | ((i < CHUNK) & (j >= BAND_PAST + CHUNK)) | ((i >= CHUNK) & (j < CHUNK))
    return jnp.where(hidden[None], -jnp.inf, bias)


def _log_sigmoid(x):
    return -(jnp.maximum(-x, 0.0) + jnp.log1p(jnp.exp(-jnp.abs(x))))


def _shift_rows(x, d, fill):
    n = x.shape[0]
    if d % SUBLANES == 0:
        return jnp.concatenate([jnp.full((d, x.shape[1]), fill, x.dtype), x[:n - d]], axis=0)
    rolled = pltpu.roll(x, d, 0)
    row = lax.broadcasted_iota(jnp.int32, x.shape, 0)
    return jnp.where(row < d, fill, rolled)


def _rglru_kernel(xb_ref, lb_ref, h0_ref, cw_ref, cb_ref, wa_ref, ba_ref, wx_ref, bx_ref, lam_ref,
                  h_ref, xext_ref, hc_ref):
    t = pl.program_id(1)
    tt = xb_ref.shape[0]

    @pl.when(t == 0)
    def _():
        xext_ref[0:LRU_HALO, :] = lb_ref[...]
        hc_ref[...] = h0_ref[...]

    xext_ref[LRU_HALO:, :] = xb_ref[...]
    xc = cb_ref[...]
    for k in range(CONV_B):
        off = LRU_HALO - (CONV_B - 1) + k
        xc = xc + cw_ref[k:k + 1, :] * xext_ref[off:off + tt, :]
    xext_ref[0:LRU_HALO, :] = xext_ref[tt:tt + LRU_HALO, :]

    xcb = xc.astype(BF16)
    ra, rx = [], []
    for n in range(N_BLOCKS_B):
        blk = xcb[:, n * BLOCK_B:(n + 1) * BLOCK_B]
        ra.append(jnp.dot(blk, wa_ref[n], preferred_element_type=F32))
        rx.append(jnp.dot(blk, wx_ref[n], preferred_element_type=F32))
    r = _sigmoid(jnp.concatenate(ra, axis=-1) + ba_ref[...])
    i = _sigmoid(jnp.concatenate(rx, axis=-1) + bx_ref[...])
    log_a = LRU_C * r * _log_sigmoid(lam_ref[...])
    a = jnp.exp(log_a)
    th = jnp.tanh(log_a)
    bt = jnp.sqrt(-2.0 * th / (1.0 - th)) * (i * xc)

    d = 1
    while d < tt:
        bt = bt + a * _shift_rows(bt, d, 0.0)
        a = a * _shift_rows(a, d, 1.0)
        d *= 2
    h = a * hc_ref[...] + bt
    h_ref[...] = h
    hc_ref[...] = h[tt - 1:tt, :]


def _rglru(u, lb0, h0, cw, cb, w_a, b_a, w_x, b_x, lam, *, tt):
    b, t, _ = u.shape
    row = lambda a: a.reshape(1, D_B)
    full = lambda a: pl.BlockSpec(a.shape, lambda bi, ti: (0,) * a.ndim)
    args = (cw, row(cb), w_a, row(b_a), w_x, row(b_x), row(lam))
    return pl.pallas_call(
        _rglru_kernel,
        grid=(b, t // tt),
        in_specs=[
            pl.BlockSpec((None, tt, D_B), lambda bi, ti: (bi, ti, 4 * D_A // D_B)),
            pl.BlockSpec((None, LRU_HALO, D_B), lambda bi, ti: (bi, 0, 0)),
            pl.BlockSpec((None, 1, D_B), lambda bi, ti: (bi, 0, 0)),
        ] + [full(a) for a in args],
        out_specs=pl.BlockSpec((None, tt, D_B), lambda bi, ti: (bi, ti, 0)),
        out_shape=jax.ShapeDtypeStruct((b, t, D_B), F32),
        scratch_shapes=[pltpu.VMEM((LRU_HALO + tt, D_B), F32), pltpu.VMEM((1, D_B), F32)],
        compiler_params=_cparams(("parallel", "arbitrary")),
        name="rglru",
    )(u, lb0, h0, *args)


def _gated_out_kernel(oa_ref, ga_ref, h_ref, gb_ref, x_ref, w_ref, o_ref):
    ga = ga_ref[...]
    gb = gb_ref[...]
    ma = (ga * _sigmoid(ga) * oa_ref[...]).astype(BF16)
    mb = (gb * _sigmoid(gb) * h_ref[...]).astype(BF16)
    acc = jnp.dot(ma, w_ref[0:D_A, :], preferred_element_type=F32)
    acc = acc + jnp.dot(mb, w_ref[D_A:, :], preferred_element_type=F32)
    o_ref[...] = x_ref[...] + acc


def _gated_out(o_a, u, h, x, w, *, tm):
    m = x.shape[0]
    return pl.pallas_call(
        _gated_out_kernel,
        grid=(m // tm,),
        in_specs=[
            pl.BlockSpec((tm, D_A), lambda i: (i, 0)),
            pl.BlockSpec((tm, D_A), lambda i: (i, 3)),
            pl.BlockSpec((tm, D_B), lambda i: (i, 0)),
            pl.BlockSpec((tm, D_B), lambda i: (i, (4 * D_A + D_B) // D_B)),
            pl.BlockSpec((tm, D_MODEL), lambda i: (i, 0)),
            pl.BlockSpec(w.shape, lambda i: (0, 0)),
        ],
        out_specs=pl.BlockSpec((tm, D_MODEL), lambda i: (i, 0)),
        out_shape=jax.ShapeDtypeStruct((m, D_MODEL), F32),
        compiler_params=_cparams(("parallel",)),
        name="gated_out_proj",
    )(o_a, u, h, u, x, w)


def _dwconv_ln_kernel(z_ref, st_ref, sg_ref, w_ref, b_ref, lg_ref, lb_ref, y_ref, zext_ref):
    t = pl.program_id(1)
    tt = z_ref.shape[0]

    @pl.when(t == 0)
    def _():
        zext_ref[0:CONV_HALO, :] = st_ref[...]

    zext_ref[CONV_HALO:, :] = z_ref[...]
    for r0 in range(0, tt, CONV_ROWS):
        parts = []
        for c0 in range(0, D_C, CONV_COLS):
            cols = slice(c0, c0 + CONV_COLS)
            acc = jnp.broadcast_to(b_ref[:, cols], (CONV_ROWS, CONV_COLS))
            for k in range(CONV_C):
                off = r0 + CONV_HALO - (CONV_C - 1) + k
                acc = acc + w_ref[k:k + 1, cols] * zext_ref[off:off + CONV_ROWS, cols]
            parts.append(acc)
        acc = jnp.concatenate(parts, axis=-1)
        mu = jnp.mean(acc, axis=-1, keepdims=True)
        cen = acc - mu
        var = jnp.mean(cen * cen, axis=-1, keepdims=True)
        yn = cen * lax.rsqrt(var + EPS) * lg_ref[...] + lb_ref[...]
        y_ref[r0:r0 + CONV_ROWS, :] = (yn * _sigmoid(yn) * sg_ref[r0:r0 + CONV_ROWS, :]).astype(BF16)
    zext_ref[0:CONV_HALO, :] = zext_ref[tt:tt + CONV_HALO, :]


def _dwconv_ln(z, state, sg, dw_w, dw_b, ln_g, ln_b, *, tt):
    b, t, _ = z.shape
    row = lambda a: a.reshape(1, D_C)
    full = lambda a: pl.BlockSpec(a.shape, lambda bi, ti: (0,) * a.ndim)
    args = (dw_w, row(dw_b), row(ln_g), row(ln_b))
    tile = pl.BlockSpec((None, tt, D_C), lambda bi, ti: (bi, ti, 0))
    return pl.pallas_call(
        _dwconv_ln_kernel,
        grid=(b, t // tt),
        in_specs=[tile, pl.BlockSpec((None, CONV_HALO, D_C), lambda bi, ti: (bi, 0, 0)), tile]
        + [full(a) for a in args],
        out_specs=tile,
        out_shape=jax.ShapeDtypeStruct((b, t, D_C), BF16),
        scratch_shapes=[pltpu.VMEM((CONV_HALO + tt, D_C), F32)],
        compiler_params=_cparams(("parallel", "arbitrary")),
        name="dwconv_ln",
    )(z, state, sg, *args)


def _out_norm_kernel(y_ref, w_ref, x_ref, g_ref, o_ref):
    x = x_ref[...] + jnp.dot(y_ref[...], w_ref[...], preferred_element_type=F32)
    ms = jnp.mean(x * x, axis=-1, keepdims=True)
    o_ref[...] = x * lax.rsqrt(ms + EPS) * g_ref[...]


def _out_norm(y, w, x, g, *, tm):
    m = x.shape[0]
    return pl.pallas_call(
        _out_norm_kernel,
        grid=(m // tm,),
        in_specs=[
            pl.BlockSpec((tm, D_C), lambda i: (i, 0)),
            pl.BlockSpec(w.shape, lambda i: (0, 0)),
            pl.BlockSpec((tm, D_MODEL), lambda i: (i, 0)),
            pl.BlockSpec((1, D_MODEL), lambda i: (0, 0)),
        ],
        out_specs=pl.BlockSpec((tm, D_MODEL), lambda i: (i, 0)),
        out_shape=jax.ShapeDtypeStruct((m, D_MODEL), F32),
        compiler_params=_cparams(("parallel",)),
        name="out_proj_final_norm",
    )(y, w, x, g)


def _front_pad_rows(a, rows):
    return jnp.pad(a, ((0, 0), (rows - a.shape[1], 0), (0, 0)))


def _trunk(x, caches, w, *, tm, lru_tt, conv_tt):
    b, t, _ = x.shape
    m = b * t
    x2d = x.reshape(m, D_MODEL)

    u = _norm_matmul(x2d, w["norm_ab"], w["w_in_ab"], tm=tm, tn=512)
    u3 = u.reshape(b, t, IN_AB)
    if caches is None:
        o_a = _attn_prompt(u3, w["bias_prompt"])
        h0 = jnp.zeros((b, 1, D_B), F32)
        lb0 = jnp.zeros((b, LRU_HALO, D_B), F32)
        cb0 = jnp.zeros((b, CONV_HALO, D_C), F32)
        keep = min(BAND_PAST, t)
    else:
        kc, vc, hc, lbc, cbc = caches
        lc = kc.shape[1]
        o_a = _attn_sample(u3, kc.reshape(b, lc, D_A), vc.reshape(b, lc, D_A), w["bias_sample"])
        h0 = hc.reshape(b, 1, D_B)
        lb0 = _front_pad_rows(lbc, LRU_HALO)
        cb0 = _front_pad_rows(cbc, CONV_HALO)
        keep = t
    h = _rglru(u3, lb0, h0, w["lru_conv_w"], w["lru_conv_b"], w["lru_w_a"], w["lru_b_a"],
               w["lru_w_x"], w["lru_b_x"], w["lru_lambda"], tt=lru_tt)
    x1 = _gated_out(o_a.reshape(m, D_A), u, h.reshape(m, D_B), x2d, w["w_out_ab"], tm=min(tm, 256))

    new_k = u3[:, t - keep:, D_A:2 * D_A].reshape(b, keep, N_HEADS, HEAD_DIM)
    new_v = u3[:, t - keep:, 2 * D_A:3 * D_A].reshape(b, keep, N_HEADS, HEAD_DIM)
    new_h = h[:, t - 1]
    new_lb = u3[:, t - (CONV_B - 1):, 4 * D_A:4 * D_A + D_B]

    z, sg = _norm_glu(x1, w["norm_cv"], w["w_in_cv"], tm=tm, tn=512)
    z3 = z.reshape(b, t, D_C)
    y = _dwconv_ln(z3, cb0, sg.reshape(b, t, D_C), w["dw_w"], w["dw_b"], w["ln_g"], w["ln_b"], tt=conv_tt)
    out = _out_norm(y.reshape(m, D_C), w["w_out_cv"], x1, w["final_norm"], tm=min(tm, 256))
    new_cb = z3[:, t - (CONV_C - 1):]

    return (out.reshape(b, t, D_MODEL), new_k[None], new_v[None], new_h[None], new_lb[None], new_cb[None])


def kernel(x_prompt, x_sample, cache_attn_k, cache_attn_v, state_lru_h, state_lru_conv, state_conv, norm_ab, w_in_ab, w_out_ab, rel_bias, lru_conv_w, lru_conv_b, lru_w_a, lru_b_a, lru_w_x, lru_b_x, lru_lambda, norm_cv, w_in_cv, w_out_cv, dw_w, dw_b, ln_g, ln_b, final_norm):
    assert norm_ab.shape[0] == 1 and norm_cv.shape[0] == 1, "one even and one odd layer"
    t_s = x_sample.shape[1]
    l_c = cache_attn_k.shape[2]
    w = {
        "norm_ab": norm_ab[0].reshape(1, D_MODEL),
        "w_in_ab": w_in_ab[0].astype(BF16),
        "w_out_ab": w_out_ab[0].astype(BF16),
        "bias_prompt": _rel_bias_tiles(rel_bias[0], Q_TILE, K_WIN, True),
        "bias_sample": _rel_bias_tiles(rel_bias[0], t_s, l_c + t_s, False),
        "lru_conv_w": lru_conv_w[0],
        "lru_conv_b": lru_conv_b[0],
        "lru_w_a": lru_w_a[0].astype(BF16),
        "lru_b_a": lru_b_a[0],
        "lru_w_x": lru_w_x[0].astype(BF16),
        "lru_b_x": lru_b_x[0],
        "lru_lambda": lru_lambda[0],
        "norm_cv": norm_cv[0].reshape(1, D_MODEL),
        "w_in_cv": w_in_cv[0].astype(BF16),
        "w_out_cv": w_out_cv[0].astype(BF16),
        "dw_w": dw_w[0],
        "dw_b": dw_b[0],
        "ln_g": ln_g[0],
        "ln_b": ln_b[0],
        "final_norm": final_norm.reshape(1, D_MODEL),
    }
    y_p, k_p, v_p, h_p, lb_p, cb_p = _trunk(x_prompt, None, w, tm=512, lru_tt=256, conv_tt=256)
    caches = (cache_attn_k[0], cache_attn_v[0], state_lru_h[0], state_lru_conv[0], state_conv[0])
    y_s, k_s, v_s, h_s, lb_s, cb_s = _trunk(x_sample, caches, w, tm=512, lru_tt=t_s, conv_tt=t_s)
    return (y_p, y_s, k_p, v_p, h_p, lb_p, cb_p, k_s, v_s, h_s, lb_s, cb_s)
```

```python
import functools

import jax
import jax.numpy as jnp
from jax import lax
from jax.experimental import pallas as pl
from jax.experimental.pallas import tpu as pltpu

D_MODEL = 2048
CHUNK = 64
LEFT_CHUNKS = 8
BAND_PAST = CHUNK * LEFT_CHUNKS
N_HEADS = 8
HEAD_DIM = 128
D_A = N_HEADS * HEAD_DIM
REL_CLIP = 128
D_B = 1024
N_BLOCKS_B = 8
BLOCK_B = D_B // N_BLOCKS_B
CONV_B = 4
LRU_C = 8.0
D_C = D_MODEL
CONV_C = 31
IN_AB = 4 * D_A + 2 * D_B
IN_CV = 3 * D_C
EPS = 1e-6

F32 = jnp.float32
BF16 = jnp.bfloat16

SUBLANES = 8
Q_TILE = 2 * CHUNK
K_WIN = BAND_PAST + Q_TILE
ATT_BLOCK = 512
CONV_HALO = 32
CONV_ROWS = 32
CONV_COLS = 512
LRU_HALO = 8
VMEM_LIMIT = 56 * 1024 * 1024


def _cparams(sem):
    return pltpu.CompilerParams(dimension_semantics=sem, vmem_limit_bytes=VMEM_LIMIT)


def _sigmoid(x):
    return jax.nn.sigmoid(x)


def _norm_to_bf16(x, g):
    ms = jnp.mean(x * x, axis=-1, keepdims=True)
    return (x * lax.rsqrt(ms + EPS) * g).astype(BF16)


def _norm_matmul_kernel(x_ref, g_ref, w_ref, o_ref, xn_ref):
    @pl.when(pl.program_id(1) == 0)
    def _():
        xn_ref[...] = _norm_to_bf16(x_ref[...], g_ref[...])

    o_ref[...] = jnp.dot(xn_ref[...], w_ref[...], preferred_element_type=F32)


def _norm_matmul(x, g, w, *, tm, tn):
    m, d = x.shape
    n = w.shape[1]
    return pl.pallas_call(
        _norm_matmul_kernel,
        grid=(m // tm, n // tn),
        in_specs=[
            pl.BlockSpec((tm, d), lambda i, j: (i, 0)),
            pl.BlockSpec((1, d), lambda i, j: (0, 0)),
            pl.BlockSpec((d, tn), lambda i, j: (0, j)),
        ],
        out_specs=pl.BlockSpec((tm, tn), lambda i, j: (i, j)),
        out_shape=jax.ShapeDtypeStruct((m, n), F32),
        scratch_shapes=[pltpu.VMEM((tm, d), BF16)],
        compiler_params=_cparams(("parallel", "arbitrary")),
        name="norm_in_proj",
    )(x, g, w)


def _norm_glu_kernel(x_ref, g_ref, wv_ref, wg_ref, wt_ref, z_ref, sg_ref, xn_ref):
    @pl.when(pl.program_id(1) == 0)
    def _():
        xn_ref[...] = _norm_to_bf16(x_ref[...], g_ref[...])

    xn = xn_ref[...]
    val = jnp.dot(xn, wv_ref[...], preferred_element_type=F32)
    glu = jnp.dot(xn, wg_ref[...], preferred_element_type=F32)
    gate = jnp.dot(xn, wt_ref[...], preferred_element_type=F32)
    z_ref[...] = val * _sigmoid(glu)
    sg_ref[...] = gate * _sigmoid(gate)


def _norm_glu(x, g, w, *, tm, tn):
    m, d = x.shape
    nb = D_C // tn
    return pl.pallas_call(
        _norm_glu_kernel,
        grid=(m // tm, nb),
        in_specs=[
            pl.BlockSpec((tm, d), lambda i, j: (i, 0)),
            pl.BlockSpec((1, d), lambda i, j: (0, 0)),
            pl.BlockSpec((d, tn), lambda i, j: (0, j)),
            pl.BlockSpec((d, tn), lambda i, j: (0, j + nb)),
            pl.BlockSpec((d, tn), lambda i, j: (0, j + 2 * nb)),
        ],
        out_specs=[
            pl.BlockSpec((tm, tn), lambda i, j: (i, j)),
            pl.BlockSpec((tm, tn), lambda i, j: (i, j)),
        ],
        out_shape=[jax.ShapeDtypeStruct((m, D_C), F32), jax.ShapeDtypeStruct((m, D_C), F32)],
        scratch_shapes=[pltpu.VMEM((tm, d), BF16)],
        compiler_params=_cparams(("parallel", "arbitrary")),
        name="norm_in_proj_glu",
    )(x, g, w, w, w)


def _attend(q, kw, vw, bias, kpos0):
    s = lax.dot_general(q, kw, (((1,), (1,)), ((), ())), preferred_element_type=F32)
    s = s * (HEAD_DIM ** -0.5) + bias
    if kpos0 is not None:
        col = lax.broadcasted_iota(jnp.int32, s.shape, 1)
        s = jnp.where(col + kpos0 >= 0, s, -jnp.inf)
    m = jnp.max(s, axis=-1, keepdims=True)
    p = jnp.exp(s - m)
    l = jnp.sum(p, axis=-1, keepdims=True)
    o = jnp.dot(p.astype(BF16), vw, preferred_element_type=F32)
    return o / l


def _expand_rel_bias(fvec_ref, bias_ref, n_valid, chunk_masks):
    tq = bias_ref.shape[1]
    i = lax.broadcasted_iota(jnp.int32, (tq, K_WIN), 0)
    j = lax.broadcasted_iota(jnp.int32, (tq, K_WIN), 1)
    if chunk_masks:
        lo = jnp.where(i < CHUNK, 0, CHUNK)
        hi = jnp.where(i < CHUNK, min(BAND_PAST + CHUNK, n_valid), n_valid)
        hidden = (j < lo) | (j >= hi)
    else:
        hidden = j >= n_valid
    for h in range(N_HEADS):
        rows = jnp.broadcast_to(fvec_ref[h:h + 1, :], (tq, Q_TILE + K_WIN))
        skew = pltpu.roll(rows, 0, 1, stride=1, stride_axis=0)
        bias_ref[h] = jnp.where(hidden, -jnp.inf, skew[:, Q_TILE:])


def _attn_prompt_kernel(q_ref, kp_ref, kc_ref, vp_ref, vc_ref, fvec_ref, o_ref, nk_ref, nv_ref,
                        kcat_ref, vcat_ref, bias_ref):
    b = pl.program_id(0)
    t = pl.program_id(1)

    @pl.when((b == 0) & (t == 0))
    def _():
        _expand_rel_bias(fvec_ref, bias_ref, K_WIN, True)

    @pl.when(t == pl.num_programs(1) - 1)
    def _():
        for h in range(N_HEADS):
            c0 = h * HEAD_DIM
            nk_ref[:, h, :] = kc_ref[:, c0:c0 + HEAD_DIM]
            nv_ref[:, h, :] = vc_ref[:, c0:c0 + HEAD_DIM]

    kcat_ref[0:ATT_BLOCK, :] = kp_ref[...].astype(BF16)
    kcat_ref[ATT_BLOCK:, :] = kc_ref[...].astype(BF16)
    vcat_ref[0:ATT_BLOCK, :] = vp_ref[...].astype(BF16)
    vcat_ref[ATT_BLOCK:, :] = vc_ref[...].astype(BF16)
    for qi in range(ATT_BLOCK // Q_TILE):
        r0 = qi * Q_TILE
        kpos0 = t * ATT_BLOCK + r0 - BAND_PAST
        for h in range(N_HEADS):
            c0 = h * HEAD_DIM
            q = q_ref[r0:r0 + Q_TILE, c0:c0 + HEAD_DIM].astype(BF16)
            kw = kcat_ref[r0:r0 + K_WIN, c0:c0 + HEAD_DIM]
            vw = vcat_ref[r0:r0 + K_WIN, c0:c0 + HEAD_DIM]
            o_ref[r0:r0 + Q_TILE, c0:c0 + HEAD_DIM] = _attend(q, kw, vw, bias_ref[h], kpos0)


def _attn_prompt(u, fvec):
    b, s, _ = u.shape
    blk = (None, ATT_BLOCK, D_A)
    prev = lambda col: (lambda bi, t: (bi, jnp.maximum(t - 1, 0), col))
    cur = lambda col: (lambda bi, t: (bi, t, col))
    state = pl.BlockSpec((None, ATT_BLOCK, N_HEADS, HEAD_DIM), lambda bi, t: (bi, 0, 0, 0))
    state_shape = jax.ShapeDtypeStruct((b, ATT_BLOCK, N_HEADS, HEAD_DIM), F32)
    return pl.pallas_call(
        _attn_prompt_kernel,
        grid=(b, s // ATT_BLOCK),
        in_specs=[
            pl.BlockSpec(blk, cur(0)),
            pl.BlockSpec(blk, prev(1)),
            pl.BlockSpec(blk, cur(1)),
            pl.BlockSpec(blk, prev(2)),
            pl.BlockSpec(blk, cur(2)),
            pl.BlockSpec(fvec.shape, lambda bi, t: (0, 0)),
        ],
        out_specs=[pl.BlockSpec(blk, cur(0)), state, state],
        out_shape=[jax.ShapeDtypeStruct((b, s, D_A), F32), state_shape, state_shape],
        scratch_shapes=[
            pltpu.VMEM((2 * ATT_BLOCK, D_A), BF16),
            pltpu.VMEM((2 * ATT_BLOCK, D_A), BF16),
            pltpu.VMEM((N_HEADS, Q_TILE, K_WIN), F32),
        ],
        compiler_params=_cparams(("arbitrary", "arbitrary")),
        name="attn_prompt",
    )(u, u, u, u, u, fvec)


def _attn_sample_kernel(q_ref, kn_ref, vn_ref, kc_ref, vc_ref, fvec_ref, o_ref, nk_ref, nv_ref,
                        kcat_ref, vcat_ref, bias_ref):
    tq = q_ref.shape[0]
    lc = kc_ref.shape[0]

    @pl.when(pl.program_id(0) == 0)
    def _():
        _expand_rel_bias(fvec_ref, bias_ref, lc + tq, False)
        kcat_ref[lc + tq:, :] = jnp.zeros((K_WIN - lc - tq, D_A), BF16)
        vcat_ref[lc + tq:, :] = jnp.zeros((K_WIN - lc - tq, D_A), BF16)

    for h in range(N_HEADS):
        c0 = h * HEAD_DIM
        kcat_ref[0:lc, c0:c0 + HEAD_DIM] = kc_ref[:, h, :].astype(BF16)
        vcat_ref[0:lc, c0:c0 + HEAD_DIM] = vc_ref[:, h, :].astype(BF16)
        nk_ref[:, h, :] = kn_ref[:, c0:c0 + HEAD_DIM]
        nv_ref[:, h, :] = vn_ref[:, c0:c0 + HEAD_DIM]
    kcat_ref[lc:lc + tq, :] = kn_ref[...].astype(BF16)
    vcat_ref[lc:lc + tq, :] = vn_ref[...].astype(BF16)
    for h in range(N_HEADS):
        c0 = h * HEAD_DIM
        q = q_ref[:, c0:c0 + HEAD_DIM].astype(BF16)
        kw = kcat_ref[:, c0:c0 + HEAD_DIM]
        vw = vcat_ref[:, c0:c0 + HEAD_DIM]
        o_ref[:, c0:c0 + HEAD_DIM] = _attend(q, kw, vw, bias_ref[h], None)


def _attn_sample(u, k_cache, v_cache, fvec):
    b, t, _ = u.shape
    lc = k_cache.shape[1]
    assert lc + t <= K_WIN and t <= Q_TILE
    new = lambda col: pl.BlockSpec((None, t, D_A), lambda bi: (bi, 0, col))
    cache = pl.BlockSpec((None, lc, N_HEADS, HEAD_DIM), lambda bi: (bi, 0, 0, 0))
    state = pl.BlockSpec((None, t, N_HEADS, HEAD_DIM), lambda bi: (bi, 0, 0, 0))
    state_shape = jax.ShapeDtypeStruct((b, t, N_HEADS, HEAD_DIM), F32)
    return pl.pallas_call(
        _attn_sample_kernel,
        grid=(b,),
        in_specs=[new(0), new(1), new(2), cache, cache, pl.BlockSpec(fvec.shape, lambda bi: (0, 0))],
        out_specs=[pl.BlockSpec((None, t, D_A), lambda bi: (bi, 0, 0)), state, state],
        out_shape=[jax.ShapeDtypeStruct((b, t, D_A), F32), state_shape, state_shape],
        scratch_shapes=[
            pltpu.VMEM((K_WIN, D_A), BF16),
            pltpu.VMEM((K_WIN, D_A), BF16),
            pltpu.VMEM((N_HEADS, t, K_WIN), F32),
        ],
        compiler_params=_cparams(("arbitrary",)),
        name="attn_sample",
    )(u, u, u, k_cache, v_cache, fvec)


def _rel_bias_vector(rel_table):
    width = Q_TILE + K_WIN
    n_const = BAND_PAST + Q_TILE - REL_CLIP + 1
    n_rev = width - n_const
    assert 0 < n_rev <= 2 * REL_CLIP
    const = jnp.broadcast_to(rel_table[:, 2 * REL_CLIP:], (N_HEADS, n_const))
    rev = lax.rev(rel_table[:, 2 * REL_CLIP - n_rev:2 * REL_CLIP], (1,))
    return jnp.concatenate([const, rev], axis=1).astype(F32)


def _log_sigmoid(x):
    return -(jnp.maximum(-x, 0.0) + jnp.log1p(jnp.exp(-jnp.abs(x))))


def _shift_rows(x, d, fill):
    n = x.shape[0]
    if d % SUBLANES == 0:
        return jnp.concatenate([jnp.full((d, x.shape[1]), fill, x.dtype), x[:n - d]], axis=0)
    rolled = pltpu.roll(x, d, 0)
    row = lax.broadcasted_iota(jnp.int32, x.shape, 0)
    return jnp.where(row < d, fill, rolled)


def _rglru_kernel(xb_ref, lb_ref, h0_ref, cw_ref, cb_ref, wa_ref, ba_ref, wx_ref, bx_ref, lam_ref,
                  h_ref, xext_ref, hc_ref):
    t = pl.program_id(1)
    tt = xb_ref.shape[0]

    @pl.when(t == 0)
    def _():
        xext_ref[0:LRU_HALO, :] = lb_ref[...]
        hc_ref[...] = h0_ref[...]

    xext_ref[LRU_HALO:, :] = xb_ref[...]
    xc = cb_ref[...]
    for k in range(CONV_B):
        off = LRU_HALO - (CONV_B - 1) + k
        xc = xc + cw_ref[k:k + 1, :] * xext_ref[off:off + tt, :]
    xext_ref[0:LRU_HALO, :] = xext_ref[tt:tt + LRU_HALO, :]

    xcb = xc.astype(BF16)
    ra, rx = [], []
    for n in range(N_BLOCKS_B):
        blk = xcb[:, n * BLOCK_B:(n + 1) * BLOCK_B]
        ra.append(jnp.dot(blk, wa_ref[n], preferred_element_type=F32))
        rx.append(jnp.dot(blk, wx_ref[n], preferred_element_type=F32))
    r = _sigmoid(jnp.concatenate(ra, axis=-1) + ba_ref[...])
    i = _sigmoid(jnp.concatenate(rx, axis=-1) + bx_ref[...])
    log_a = LRU_C * r * _log_sigmoid(lam_ref[...])
    a = jnp.exp(log_a)
    th = jnp.tanh(log_a)
    bt = jnp.sqrt(-2.0 * th / (1.0 - th)) * (i * xc)

    d = 1
    while d < tt:
        bt = bt + a * _shift_rows(bt, d, 0.0)
        a = a * _shift_rows(a, d, 1.0)
        d *= 2
    h = a * hc_ref[...] + bt
    h_ref[...] = h
    hc_ref[...] = h[tt - 1:tt, :]


def _rglru(u, lb0, h0, cw, cb, w_a, b_a, w_x, b_x, lam, *, tt):
    b, t, _ = u.shape
    row = lambda a: a.reshape(1, D_B)
    full = lambda a: pl.BlockSpec(a.shape, lambda bi, ti: (0,) * a.ndim)
    args = (cw, row(cb), w_a, row(b_a), w_x, row(b_x), row(lam))
    return pl.pallas_call(
        _rglru_kernel,
        grid=(b, t // tt),
        in_specs=[
            pl.BlockSpec((None, tt, D_B), lambda bi, ti: (bi, ti, 4 * D_A // D_B)),
            pl.BlockSpec((None, LRU_HALO, D_B), lambda bi, ti: (bi, 0, 0)),
            pl.BlockSpec((None, 1, D_B), lambda bi, ti: (bi, 0, 0)),
        ] + [full(a) for a in args],
        out_specs=pl.BlockSpec((None, tt, D_B), lambda bi, ti: (bi, ti, 0)),
        out_shape=jax.ShapeDtypeStruct((b, t, D_B), F32),
        scratch_shapes=[pltpu.VMEM((LRU_HALO + tt, D_B), F32), pltpu.VMEM((1, D_B), F32)],
        compiler_params=_cparams(("parallel", "arbitrary")),
        name="rglru",
    )(u, lb0, h0, *args)


def _gated_out_kernel(oa_ref, ga_ref, h_ref, gb_ref, x_ref, w_ref, o_ref):
    ga = ga_ref[...]
    gb = gb_ref[...]
    ma = (ga * _sigmoid(ga) * oa_ref[...]).astype(BF16)
    mb = (gb * _sigmoid(gb) * h_ref[...]).astype(BF16)
    acc = jnp.dot(ma, w_ref[0:D_A, :], preferred_element_type=F32)
    acc = acc + jnp.dot(mb, w_ref[D_A:, :], preferred_element_type=F32)
    o_ref[...] = x_ref[...] + acc


def _gated_out(o_a, u, h, x, w, *, tm):
    m = x.shape[0]
    return pl.pallas_call(
        _gated_out_kernel,
        grid=(m // tm,),
        in_specs=[
            pl.BlockSpec((tm, D_A), lambda i: (i, 0)),
            pl.BlockSpec((tm, D_A), lambda i: (i, 3)),
            pl.BlockSpec((tm, D_B), lambda i: (i, 0)),
            pl.BlockSpec((tm, D_B), lambda i: (i, (4 * D_A + D_B) // D_B)),
            pl.BlockSpec((tm, D_MODEL), lambda i: (i, 0)),
            pl.BlockSpec(w.shape, lambda i: (0, 0)),
        ],
        out_specs=pl.BlockSpec((tm, D_MODEL), lambda i: (i, 0)),
        out_shape=jax.ShapeDtypeStruct((m, D_MODEL), F32),
        compiler_params=_cparams(("parallel",)),
        name="gated_out_proj",
    )(o_a, u, h, u, x, w)


def _dwconv_ln_kernel(z_ref, st_ref, sg_ref, w_ref, b_ref, lg_ref, lb_ref, y_ref, zext_ref):
    t = pl.program_id(1)
    tt = z_ref.shape[0]

    @pl.when(t == 0)
    def _():
        zext_ref[0:CONV_HALO, :] = st_ref[...]

    zext_ref[CONV_HALO:, :] = z_ref[...]
    for r0 in range(0, tt, CONV_ROWS):
        parts = []
        for c0 in range(0, D_C, CONV_COLS):
            cols = slice(c0, c0 + CONV_COLS)
            acc = jnp.broadcast_to(b_ref[:, cols], (CONV_ROWS, CONV_COLS))
            for phase in range(SUBLANES):
                part = None
                for k in range(CONV_C):
                    off = r0 + CONV_HALO - (CONV_C - 1) + k
                    if off % SUBLANES != phase:
                        continue
                    term = w_ref[k:k + 1, cols] * zext_ref[off:off + CONV_ROWS, cols]
                    part = term if part is None else part + term
                if part is not None:
                    acc = acc + part
            parts.append(acc)
        acc = jnp.concatenate(parts, axis=-1)
        mu = jnp.mean(acc, axis=-1, keepdims=True)
        cen = acc - mu
        var = jnp.mean(cen * cen, axis=-1, keepdims=True)
        yn = cen * lax.rsqrt(var + EPS) * lg_ref[...] + lb_ref[...]
        y_ref[r0:r0 + CONV_ROWS, :] = (yn * _sigmoid(yn) * sg_ref[r0:r0 + CONV_ROWS, :]).astype(BF16)
    zext_ref[0:CONV_HALO, :] = zext_ref[tt:tt + CONV_HALO, :]


def _dwconv_ln(z, state, sg, dw_w, dw_b, ln_g, ln_b, *, tt):
    b, t, _ = z.shape
    row = lambda a: a.reshape(1, D_C)
    full = lambda a: pl.BlockSpec(a.shape, lambda bi, ti: (0,) * a.ndim)
    args = (dw_w, row(dw_b), row(ln_g), row(ln_b))
    tile = pl.BlockSpec((None, tt, D_C), lambda bi, ti: (bi, ti, 0))
    return pl.pallas_call(
        _dwconv_ln_kernel,
        grid=(b, t // tt),
        in_specs=[tile, pl.BlockSpec((None, CONV_HALO, D_C), lambda bi, ti: (bi, 0, 0)), tile]
        + [full(a) for a in args],
        out_specs=tile,
        out_shape=jax.ShapeDtypeStruct((b, t, D_C), BF16),
        scratch_shapes=[pltpu.VMEM((CONV_HALO + tt, D_C), F32)],
        compiler_params=_cparams(("parallel", "arbitrary")),
        name="dwconv_ln",
    )(z, state, sg, *args)


def _out_norm_kernel(y_ref, w_ref, x_ref, g_ref, o_ref):
    x = x_ref[...] + jnp.dot(y_ref[...], w_ref[...], preferred_element_type=F32)
    ms = jnp.mean(x * x, axis=-1, keepdims=True)
    o_ref[...] = x * lax.rsqrt(ms + EPS) * g_ref[...]


def _out_norm(y, w, x, g, *, tm):
    m = x.shape[0]
    return pl.pallas_call(
        _out_norm_kernel,
        grid=(m // tm,),
        in_specs=[
            pl.BlockSpec((tm, D_C), lambda i: (i, 0)),
            pl.BlockSpec(w.shape, lambda i: (0, 0)),
            pl.BlockSpec((tm, D_MODEL), lambda i: (i, 0)),
            pl.BlockSpec((1, D_MODEL), lambda i: (0, 0)),
        ],
        out_specs=pl.BlockSpec((tm, D_MODEL), lambda i: (i, 0)),
        out_shape=jax.ShapeDtypeStruct((m, D_MODEL), F32),
        compiler_params=_cparams(("parallel",)),
        name="out_proj_final_norm",
    )(y, w, x, g)


def _front_pad_rows(a, rows):
    return jnp.pad(a, ((0, 0), (rows - a.shape[1], 0), (0, 0)))


def _trunk(x, caches, w, *, tm, lru_tt, conv_tt):
    b, t, _ = x.shape
    m = b * t
    x2d = x.reshape(m, D_MODEL)

    u = _norm_matmul(x2d, w["norm_ab"], w["w_in_ab"], tm=tm, tn=512)
    u3 = u.reshape(b, t, IN_AB)
    if caches is None:
        assert t % ATT_BLOCK == 0 and ATT_BLOCK == BAND_PAST
        o_a, new_k, new_v = _attn_prompt(u3, w["rel_bias_vec"])
        h0 = jnp.zeros((b, 1, D_B), F32)
        lb0 = jnp.zeros((b, LRU_HALO, D_B), F32)
        cb0 = jnp.zeros((b, CONV_HALO, D_C), F32)
    else:
        kc, vc, hc, lbc, cbc = caches
        o_a, new_k, new_v = _attn_sample(u3, kc, vc, w["rel_bias_vec"])
        h0 = hc.reshape(b, 1, D_B)
        lb0 = _front_pad_rows(lbc, LRU_HALO)
        cb0 = _front_pad_rows(cbc, CONV_HALO)
    h = _rglru(u3, lb0, h0, w["lru_conv_w"], w["lru_conv_b"], w["lru_w_a"], w["lru_b_a"],
               w["lru_w_x"], w["lru_b_x"], w["lru_lambda"], tt=lru_tt)
    x1 = _gated_out(o_a.reshape(m, D_A), u, h.reshape(m, D_B), x2d, w["w_out_ab"], tm=min(tm, 256))

    new_h = h[:, t - 1]
    new_lb = u3[:, t - (CONV_B - 1):, 4 * D_A:4 * D_A + D_B]

    z, sg = _norm_glu(x1, w["norm_cv"], w["w_in_cv"], tm=tm, tn=512)
    z3 = z.reshape(b, t, D_C)
    y = _dwconv_ln(z3, cb0, sg.reshape(b, t, D_C), w["dw_w"], w["dw_b"], w["ln_g"], w["ln_b"], tt=conv_tt)
    out = _out_norm(y.reshape(m, D_C), w["w_out_cv"], x1, w["final_norm"], tm=min(tm, 256))
    new_cb = z3[:, t - (CONV_C - 1):]

    return (out.reshape(b, t, D_MODEL), new_k[None], new_v[None], new_h[None], new_lb[None], new_cb[None])


def kernel(x_prompt, x_sample, cache_attn_k, cache_attn_v, state_lru_h, state_lru_conv, state_conv, norm_ab, w_in_ab, w_out_ab, rel_bias, lru_conv_w, lru_conv_b, lru_w_a, lru_b_a, lru_w_x, lru_b_x, lru_lambda, norm_cv, w_in_cv, w_out_cv, dw_w, dw_b, ln_g, ln_b, final_norm):
    assert norm_ab.shape[0] == 1 and norm_cv.shape[0] == 1, "one even and one odd layer"
    t_s = x_sample.shape[1]
    w = {
        "norm_ab": norm_ab[0].reshape(1, D_MODEL),
        "w_in_ab": w_in_ab[0].astype(BF16),
        "w_out_ab": w_out_ab[0].astype(BF16),
        "rel_bias_vec": _rel_bias_vector(rel_bias[0]),
        "lru_conv_w": lru_conv_w[0],
        "lru_conv_b": lru_conv_b[0],
        "lru_w_a": lru_w_a[0].astype(BF16),
        "lru_b_a": lru_b_a[0],
        "lru_w_x": lru_w_x[0].astype(BF16),
        "lru_b_x": lru_b_x[0],
        "lru_lambda": lru_lambda[0],
        "norm_cv": norm_cv[0].reshape(1, D_MODEL),
        "w_in_cv": w_in_cv[0].astype(BF16),
        "w_out_cv": w_out_cv[0].astype(BF16),
        "dw_w": dw_w[0],
        "dw_b": dw_b[0],
        "ln_g": ln_g[0],
        "ln_b": ln_b[0],
        "final_norm": final_norm.reshape(1, D_MODEL),
    }
    y_p, k_p, v_p, h_p, lb_p, cb_p = _trunk(x_prompt, None, w, tm=512, lru_tt=256, conv_tt=256)
    caches = (cache_attn_k[0], cache_attn_v[0], state_lru_h[0], state_lru_conv[0], state_conv[0])
    y_s, k_s, v_s, h_s, lb_s, cb_s = _trunk(x_sample, caches, w, tm=512, lru_tt=t_s, conv_tt=t_s)
    return (y_p, y_s, k_p, v_p, h_p, lb_p, cb_p, k_s, v_s, h_s, lb_s, cb_s)
```

```python
import functools

import jax
import jax.numpy as jnp
from jax import lax
from jax.experimental import pallas as pl
from jax.experimental.pallas import tpu as pltpu

D_MODEL = 2048
CHUNK = 64
LEFT_CHUNKS = 8
BAND_PAST = CHUNK * LEFT_CHUNKS
N_HEADS = 8
HEAD_DIM = 128
D_A = N_HEADS * HEAD_DIM
REL_CLIP = 128
D_B = 1024
N_BLOCKS_B = 8
BLOCK_B = D_B // N_BLOCKS_B
CONV_B = 4
LRU_C = 8.0
D_C = D_MODEL
CONV_C = 31
IN_AB = 4 * D_A + 2 * D_B
IN_CV = 3 * D_C
EPS = 1e-6

F32 = jnp.float32
BF16 = jnp.bfloat16

SUBLANES = 8
Q_TILE = 2 * CHUNK
K_WIN = BAND_PAST + Q_TILE
ATT_BLOCK = 512
CONV_HALO = 32
CONV_ROWS = 32
CONV_COLS = 512
LRU_HALO = 8
VMEM_LIMIT = 56 * 1024 * 1024


def _cparams(sem):
    return pltpu.CompilerParams(dimension_semantics=sem, vmem_limit_bytes=VMEM_LIMIT)


def _sigmoid(x):
    return jax.nn.sigmoid(x)


def _norm_to_bf16(x, g):
    ms = jnp.mean(x * x, axis=-1, keepdims=True)
    return (x * lax.rsqrt(ms + EPS) * g).astype(BF16)


def _norm_matmul_kernel(x_ref, g_ref, w_ref, o_ref, xn_ref):
    @pl.when(pl.program_id(1) == 0)
    def _():
        xn_ref[...] = _norm_to_bf16(x_ref[...], g_ref[...])

    o_ref[...] = jnp.dot(xn_ref[...], w_ref[...], preferred_element_type=F32)


def _norm_matmul(x, g, w, *, tm, tn):
    m, d = x.shape
    n = w.shape[1]
    return pl.pallas_call(
        _norm_matmul_kernel,
        grid=(m // tm, n // tn),
        in_specs=[
            pl.BlockSpec((tm, d), lambda i, j: (i, 0)),
            pl.BlockSpec((1, d), lambda i, j: (0, 0)),
            pl.BlockSpec((d, tn), lambda i, j: (0, j)),
        ],
        out_specs=pl.BlockSpec((tm, tn), lambda i, j: (i, j)),
        out_shape=jax.ShapeDtypeStruct((m, n), F32),
        scratch_shapes=[pltpu.VMEM((tm, d), BF16)],
        compiler_params=_cparams(("parallel", "arbitrary")),
        name="norm_in_proj",
    )(x, g, w)


def _norm_glu_kernel(x_ref, g_ref, wv_ref, wg_ref, wt_ref, z_ref, sg_ref, xn_ref):
    @pl.when(pl.program_id(1) == 0)
    def _():
        xn_ref[...] = _norm_to_bf16(x_ref[...], g_ref[...])

    xn = xn_ref[...]
    val = jnp.dot(xn, wv_ref[...], preferred_element_type=F32)
    glu = jnp.dot(xn, wg_ref[...], preferred_element_type=F32)
    gate = jnp.dot(xn, wt_ref[...], preferred_element_type=F32)
    z_ref[...] = val * _sigmoid(glu)
    sg_ref[...] = gate * _sigmoid(gate)


def _norm_glu(x, g, w, *, tm, tn):
    m, d = x.shape
    nb = D_C // tn
    return pl.pallas_call(
        _norm_glu_kernel,
        grid=(m // tm, nb),
        in_specs=[
            pl.BlockSpec((tm, d), lambda i, j: (i, 0)),
            pl.BlockSpec((1, d), lambda i, j: (0, 0)),
            pl.BlockSpec((d, tn), lambda i, j: (0, j)),
            pl.BlockSpec((d, tn), lambda i, j: (0, j + nb)),
            pl.BlockSpec((d, tn), lambda i, j: (0, j + 2 * nb)),
        ],
        out_specs=[
            pl.BlockSpec((tm, tn), lambda i, j: (i, j)),
            pl.BlockSpec((tm, tn), lambda i, j: (i, j)),
        ],
        out_shape=[jax.ShapeDtypeStruct((m, D_C), F32), jax.ShapeDtypeStruct((m, D_C), F32)],
        scratch_shapes=[pltpu.VMEM((tm, d), BF16)],
        compiler_params=_cparams(("parallel", "arbitrary")),
        name="norm_in_proj_glu",
    )(x, g, w, w, w)


def _attend(q, kw, vw, bias, kpos0):
    s = lax.dot_general(q, kw, (((1,), (1,)), ((), ())), preferred_element_type=F32)
    s = s * (HEAD_DIM ** -0.5) + bias
    if kpos0 is not None:
        col = lax.broadcasted_iota(jnp.int32, s.shape, 1)
        s = jnp.where(col + kpos0 >= 0, s, -jnp.inf)
    m = jnp.max(s, axis=-1, keepdims=True)
    p = jnp.exp(s - m)
    l = jnp.sum(p, axis=-1, keepdims=True)
    o = jnp.dot(p.astype(BF16), vw, preferred_element_type=F32)
    return o / l


def _expand_rel_bias(fvec_ref, bias_ref, n_valid, chunk_masks):
    tq = bias_ref.shape[1]
    i = lax.broadcasted_iota(jnp.int32, (tq, K_WIN), 0)
    j = lax.broadcasted_iota(jnp.int32, (tq, K_WIN), 1)
    if chunk_masks:
        lo = jnp.where(i < CHUNK, 0, CHUNK)
        hi = jnp.where(i < CHUNK, min(BAND_PAST + CHUNK, n_valid), n_valid)
        hidden = (j < lo) | (j >= hi)
    else:
        hidden = j >= n_valid
    for h in range(N_HEADS):
        rows = jnp.broadcast_to(fvec_ref[h:h + 1, :], (tq, Q_TILE + K_WIN))
        skew = pltpu.roll(rows, 0, 1, stride=1, stride_axis=0)
        bias_ref[h] = jnp.where(hidden, -jnp.inf, skew[:, Q_TILE:])


def _attn_prompt_kernel(q_ref, kp_ref, kc_ref, vp_ref, vc_ref, fvec_ref, o_ref, nk_ref, nv_ref,
                        kcat_ref, vcat_ref, bias_ref):
    b = pl.program_id(0)
    t = pl.program_id(1)

    @pl.when((b == 0) & (t == 0))
    def _():
        _expand_rel_bias(fvec_ref, bias_ref, K_WIN, True)

    @pl.when(t == pl.num_programs(1) - 1)
    def _():
        for h in range(N_HEADS):
            c0 = h * HEAD_DIM
            nk_ref[:, h, :] = kc_ref[:, c0:c0 + HEAD_DIM]
            nv_ref[:, h, :] = vc_ref[:, c0:c0 + HEAD_DIM]

    kcat_ref[0:ATT_BLOCK, :] = kp_ref[...].astype(BF16)
    kcat_ref[ATT_BLOCK:, :] = kc_ref[...].astype(BF16)
    vcat_ref[0:ATT_BLOCK, :] = vp_ref[...].astype(BF16)
    vcat_ref[ATT_BLOCK:, :] = vc_ref[...].astype(BF16)
    for qi in range(ATT_BLOCK // Q_TILE):
        r0 = qi * Q_TILE
        kpos0 = t * ATT_BLOCK + r0 - BAND_PAST
        for h in range(N_HEADS):
            c0 = h * HEAD_DIM
            q = q_ref[r0:r0 + Q_TILE, c0:c0 + HEAD_DIM].astype(BF16)
            kw = kcat_ref[r0:r0 + K_WIN, c0:c0 + HEAD_DIM]
            vw = vcat_ref[r0:r0 + K_WIN, c0:c0 + HEAD_DIM]
            o_ref[r0:r0 + Q_TILE, c0:c0 + HEAD_DIM] = _attend(q, kw, vw, bias_ref[h], kpos0)


def _attn_prompt(u, fvec):
    b, s, _ = u.shape
    blk = (None, ATT_BLOCK, D_A)
    prev = lambda col: (lambda bi, t: (bi, jnp.maximum(t - 1, 0), col))
    cur = lambda col: (lambda bi, t: (bi, t, col))
    state = pl.BlockSpec((None, ATT_BLOCK, N_HEADS, HEAD_DIM), lambda bi, t: (bi, 0, 0, 0))
    state_shape = jax.ShapeDtypeStruct((b, ATT_BLOCK, N_HEADS, HEAD_DIM), F32)
    return pl.pallas_call(
        _attn_prompt_kernel,
        grid=(b, s // ATT_BLOCK),
        in_specs=[
            pl.BlockSpec(blk, cur(0)),
            pl.BlockSpec(blk, prev(1)),
            pl.BlockSpec(blk, cur(1)),
            pl.BlockSpec(blk, prev(2)),
            pl.BlockSpec(blk, cur(2)),
            pl.BlockSpec(fvec.shape, lambda bi, t: (0, 0)),
        ],
        out_specs=[pl.BlockSpec(blk, cur(0)), state, state],
        out_shape=[jax.ShapeDtypeStruct((b, s, D_A), F32), state_shape, state_shape],
        scratch_shapes=[
            pltpu.VMEM((2 * ATT_BLOCK, D_A), BF16),
            pltpu.VMEM((2 * ATT_BLOCK, D_A), BF16),
            pltpu.VMEM((N_HEADS, Q_TILE, K_WIN), F32),
        ],
        compiler_params=_cparams(("arbitrary", "arbitrary")),
        name="attn_prompt",
    )(u, u, u, u, u, fvec)


def _attn_sample_kernel(q_ref, kn_ref, vn_ref, kc_ref, vc_ref, fvec_ref, o_ref, nk_ref, nv_ref,
                        kcat_ref, vcat_ref, bias_ref):
    tq = q_ref.shape[0]
    lc = kc_ref.shape[0]

    @pl.when(pl.program_id(0) == 0)
    def _():
        _expand_rel_bias(fvec_ref, bias_ref, lc + tq, False)
        kcat_ref[lc + tq:, :] = jnp.zeros((K_WIN - lc - tq, D_A), BF16)
        vcat_ref[lc + tq:, :] = jnp.zeros((K_WIN - lc - tq, D_A), BF16)

    kcat_ref[lc:lc + tq, :] = kn_ref[...].astype(BF16)
    vcat_ref[lc:lc + tq, :] = vn_ref[...].astype(BF16)
    for h in range(N_HEADS):
        c0 = h * HEAD_DIM
        nk_ref[:, h, :] = kn_ref[:, c0:c0 + HEAD_DIM]
        nv_ref[:, h, :] = vn_ref[:, c0:c0 + HEAD_DIM]
        kcat_ref[0:lc, c0:c0 + HEAD_DIM] = kc_ref[:, h, :].astype(BF16)
        vcat_ref[0:lc, c0:c0 + HEAD_DIM] = vc_ref[:, h, :].astype(BF16)
        q = q_ref[:, c0:c0 + HEAD_DIM].astype(BF16)
        kw = kcat_ref[:, c0:c0 + HEAD_DIM]
        vw = vcat_ref[:, c0:c0 + HEAD_DIM]
        o_ref[:, c0:c0 + HEAD_DIM] = _attend(q, kw, vw, bias_ref[h], None)


def _attn_sample(u, k_cache, v_cache, fvec):
    b, t, _ = u.shape
    lc = k_cache.shape[1]
    assert lc + t <= K_WIN and t <= Q_TILE
    new = lambda col: pl.BlockSpec((None, t, D_A), lambda bi: (bi, 0, col))
    cache = pl.BlockSpec((None, lc, N_HEADS, HEAD_DIM), lambda bi: (bi, 0, 0, 0))
    state = pl.BlockSpec((None, t, N_HEADS, HEAD_DIM), lambda bi: (bi, 0, 0, 0))
    state_shape = jax.ShapeDtypeStruct((b, t, N_HEADS, HEAD_DIM), F32)
    return pl.pallas_call(
        _attn_sample_kernel,
        grid=(b,),
        in_specs=[new(0), new(1), new(2), cache, cache, pl.BlockSpec(fvec.shape, lambda bi: (0, 0))],
        out_specs=[pl.BlockSpec((None, t, D_A), lambda bi: (bi, 0, 0)), state, state],
        out_shape=[jax.ShapeDtypeStruct((b, t, D_A), F32), state_shape, state_shape],
        scratch_shapes=[
            pltpu.VMEM((K_WIN, D_A), BF16),
            pltpu.VMEM((K_WIN, D_A), BF16),
            pltpu.VMEM((N_HEADS, t, K_WIN), F32),
        ],
        compiler_params=_cparams(("arbitrary",)),
        name="attn_sample",
    )(u, u, u, k_cache, v_cache, fvec)


def _rel_bias_vector(rel_table):
    width = Q_TILE + K_WIN
    n_const = BAND_PAST + Q_TILE - REL_CLIP + 1
    n_rev = width - n_const
    assert 0 < n_rev <= 2 * REL_CLIP
    const = jnp.broadcast_to(rel_table[:, 2 * REL_CLIP:], (N_HEADS, n_const))
    rev = lax.rev(rel_table[:, 2 * REL_CLIP - n_rev:2 * REL_CLIP], (1,))
    return jnp.concatenate([const, rev], axis=1).astype(F32)


def _log_sigmoid(x):
    return -(jnp.maximum(-x, 0.0) + jnp.log1p(jnp.exp(-jnp.abs(x))))


def _shift_rows(x, d, fill):
    n = x.shape[0]
    if d % SUBLANES == 0:
        return jnp.concatenate([jnp.full((d, x.shape[1]), fill, x.dtype), x[:n - d]], axis=0)
    rolled = pltpu.roll(x, d, 0)
    row = lax.broadcasted_iota(jnp.int32, x.shape, 0)
    return jnp.where(row < d, fill, rolled)


def _rglru_kernel(xb_ref, lb_ref, h0_ref, cw_ref, cb_ref, wa_ref, ba_ref, wx_ref, bx_ref, lam_ref,
                  h_ref, xext_ref, hc_ref):
    t = pl.program_id(1)
    tt = xb_ref.shape[0]

    @pl.when(t == 0)
    def _():
        xext_ref[0:LRU_HALO, :] = lb_ref[...]
        hc_ref[...] = h0_ref[...]

    xext_ref[LRU_HALO:, :] = xb_ref[...]
    xc = cb_ref[...]
    for k in range(CONV_B):
        off = LRU_HALO - (CONV_B - 1) + k
        xc = xc + cw_ref[k:k + 1, :] * xext_ref[off:off + tt, :]
    xext_ref[0:LRU_HALO, :] = xext_ref[tt:tt + LRU_HALO, :]

    xcb = xc.astype(BF16)
    ra, rx = [], []
    for n in range(N_BLOCKS_B):
        blk = xcb[:, n * BLOCK_B:(n + 1) * BLOCK_B]
        ra.append(jnp.dot(blk, wa_ref[n], preferred_element_type=F32))
        rx.append(jnp.dot(blk, wx_ref[n], preferred_element_type=F32))
    r = _sigmoid(jnp.concatenate(ra, axis=-1) + ba_ref[...])
    i = _sigmoid(jnp.concatenate(rx, axis=-1) + bx_ref[...])
    log_a = LRU_C * r * _log_sigmoid(lam_ref[...])
    a = jnp.exp(log_a)
    th = jnp.tanh(log_a)
    bt = jnp.sqrt(-2.0 * th / (1.0 - th)) * (i * xc)

    d = 1
    while d < tt:
        bt = bt + a * _shift_rows(bt, d, 0.0)
        a = a * _shift_rows(a, d, 1.0)
        d *= 2
    h = a * hc_ref[...] + bt
    h_ref[...] = h
    hc_ref[...] = h[tt - 1:tt, :]


def _rglru(u, lb0, h0, cw, cb, w_a, b_a, w_x, b_x, lam, *, tt):
    b, t, _ = u.shape
    row = lambda a: a.reshape(1, D_B)
    full = lambda a: pl.BlockSpec(a.shape, lambda bi, ti: (0,) * a.ndim)
    args = (cw, row(cb), w_a, row(b_a), w_x, row(b_x), row(lam))
    return pl.pallas_call(
        _rglru_kernel,
        grid=(b, t // tt),
        in_specs=[
            pl.BlockSpec((None, tt, D_B), lambda bi, ti: (bi, ti, 4 * D_A // D_B)),
            pl.BlockSpec((None, LRU_HALO, D_B), lambda bi, ti: (bi, 0, 0)),
            pl.BlockSpec((None, 1, D_B), lambda bi, ti: (bi, 0, 0)),
        ] + [full(a) for a in args],
        out_specs=pl.BlockSpec((None, tt, D_B), lambda bi, ti: (bi, ti, 0)),
        out_shape=jax.ShapeDtypeStruct((b, t, D_B), F32),
        scratch_shapes=[pltpu.VMEM((LRU_HALO + tt, D_B), F32), pltpu.VMEM((1, D_B), F32)],
        compiler_params=_cparams(("parallel", "arbitrary")),
        name="rglru",
    )(u, lb0, h0, *args)


def _gated_out_kernel(oa_ref, ga_ref, h_ref, gb_ref, x_ref, w_ref, o_ref):
    ga = ga_ref[...]
    gb = gb_ref[...]
    ma = (ga * _sigmoid(ga) * oa_ref[...]).astype(BF16)
    mb = (gb * _sigmoid(gb) * h_ref[...]).astype(BF16)
    acc = jnp.dot(ma, w_ref[0:D_A, :], preferred_element_type=F32)
    acc = acc + jnp.dot(mb, w_ref[D_A:, :], preferred_element_type=F32)
    o_ref[...] = x_ref[...] + acc


def _gated_out(o_a, u, h, x, w, *, tm):
    m = x.shape[0]
    return pl.pallas_call(
        _gated_out_kernel,
        grid=(m // tm,),
        in_specs=[
            pl.BlockSpec((tm, D_A), lambda i: (i, 0)),
            pl.BlockSpec((tm, D_A), lambda i: (i, 3)),
            pl.BlockSpec((tm, D_B), lambda i: (i, 0)),
            pl.BlockSpec((tm, D_B), lambda i: (i, (4 * D_A + D_B) // D_B)),
            pl.BlockSpec((tm, D_MODEL), lambda i: (i, 0)),
            pl.BlockSpec(w.shape, lambda i: (0, 0), pipeline_mode=pl.Buffered(1)),
        ],
        out_specs=pl.BlockSpec((tm, D_MODEL), lambda i: (i, 0)),
        out_shape=jax.ShapeDtypeStruct((m, D_MODEL), F32),
        compiler_params=_cparams(("parallel",)),
        name="gated_out_proj",
    )(o_a, u, h, u, x, w)


def _dwconv_ln_kernel(z_ref, st_ref, sg_ref, w_ref, b_ref, lg_ref, lb_ref, y_ref, zext_ref):
    t = pl.program_id(1)
    tt = z_ref.shape[0]

    @pl.when(t == 0)
    def _():
        zext_ref[0:CONV_HALO, :] = st_ref[...]

    zext_ref[CONV_HALO:, :] = z_ref[...]
    for r0 in range(0, tt, CONV_ROWS):
        parts = []
        for c0 in range(0, D_C, CONV_COLS):
            cols = slice(c0, c0 + CONV_COLS)
            acc = jnp.broadcast_to(b_ref[:, cols], (CONV_ROWS, CONV_COLS))
            for phase in range(SUBLANES):
                part = None
                for k in range(CONV_C):
                    off = r0 + CONV_HALO - (CONV_C - 1) + k
                    if off % SUBLANES != phase:
                        continue
                    term = w_ref[k:k + 1, cols] * zext_ref[off:off + CONV_ROWS, cols]
                    part = term if part is None else part + term
                if part is not None:
                    acc = acc + part
            parts.append(acc)
        acc = jnp.concatenate(parts, axis=-1)
        mu = jnp.mean(acc, axis=-1, keepdims=True)
        cen = acc - mu
        var = jnp.mean(cen * cen, axis=-1, keepdims=True)
        yn = cen * lax.rsqrt(var + EPS) * lg_ref[...] + lb_ref[...]
        y_ref[r0:r0 + CONV_ROWS, :] = (yn * _sigmoid(yn) * sg_ref[r0:r0 + CONV_ROWS, :]).astype(BF16)
    zext_ref[0:CONV_HALO, :] = zext_ref[tt:tt + CONV_HALO, :]


def _dwconv_ln(z, state, sg, dw_w, dw_b, ln_g, ln_b, *, tt):
    b, t, _ = z.shape
    row = lambda a: a.reshape(1, D_C)
    full = lambda a: pl.BlockSpec(a.shape, lambda bi, ti: (0,) * a.ndim)
    args = (dw_w, row(dw_b), row(ln_g), row(ln_b))
    tile = pl.BlockSpec((None, tt, D_C), lambda bi, ti: (bi, ti, 0))
    return pl.pallas_call(
        _dwconv_ln_kernel,
        grid=(b, t // tt),
        in_specs=[tile, pl.BlockSpec((None, CONV_HALO, D_C), lambda bi, ti: (bi, 0, 0)), tile]
        + [full(a) for a in args],
        out_specs=tile,
        out_shape=jax.ShapeDtypeStruct((b, t, D_C), BF16),
        scratch_shapes=[pltpu.VMEM((CONV_HALO + tt, D_C), F32)],
        compiler_params=_cparams(("parallel", "arbitrary")),
        name="dwconv_ln",
    )(z, state, sg, *args)


def _out_norm_kernel(y_ref, w_ref, x_ref, g_ref, o_ref):
    x = x_ref[...] + jnp.dot(y_ref[...], w_ref[...], preferred_element_type=F32)
    ms = jnp.mean(x * x, axis=-1, keepdims=True)
    o_ref[...] = x * lax.rsqrt(ms + EPS) * g_ref[...]


def _out_norm(y, w, x, g, *, tm):
    m = x.shape[0]
    return pl.pallas_call(
        _out_norm_kernel,
        grid=(m // tm,),
        in_specs=[
            pl.BlockSpec((tm, D_C), lambda i: (i, 0)),
            pl.BlockSpec(w.shape, lambda i: (0, 0), pipeline_mode=pl.Buffered(1)),
            pl.BlockSpec((tm, D_MODEL), lambda i: (i, 0)),
            pl.BlockSpec((1, D_MODEL), lambda i: (0, 0)),
        ],
        out_specs=pl.BlockSpec((tm, D_MODEL), lambda i: (i, 0)),
        out_shape=jax.ShapeDtypeStruct((m, D_MODEL), F32),
        compiler_params=_cparams(("parallel",)),
        name="out_proj_final_norm",
    )(y, w, x, g)


def _front_pad_rows(a, rows):
    return jnp.pad(a, ((0, 0), (rows - a.shape[1], 0), (0, 0)))


def _trunk(x, caches, w, *, tm, tm_out, lru_tt, conv_tt):
    b, t, _ = x.shape
    m = b * t
    x2d = x.reshape(m, D_MODEL)

    u = _norm_matmul(x2d, w["norm_ab"], w["w_in_ab"], tm=tm, tn=512)
    u3 = u.reshape(b, t, IN_AB)
    if caches is None:
        assert t % ATT_BLOCK == 0 and ATT_BLOCK == BAND_PAST
        o_a, new_k, new_v = _attn_prompt(u3, w["rel_bias_vec"])
        h0 = jnp.zeros((b, 1, D_B), F32)
        lb0 = jnp.zeros((b, LRU_HALO, D_B), F32)
        cb0 = jnp.zeros((b, CONV_HALO, D_C), F32)
    else:
        kc, vc, hc, lbc, cbc = caches
        o_a, new_k, new_v = _attn_sample(u3, kc, vc, w["rel_bias_vec"])
        h0 = hc.reshape(b, 1, D_B)
        lb0 = _front_pad_rows(lbc, LRU_HALO)
        cb0 = _front_pad_rows(cbc, CONV_HALO)
    h = _rglru(u3, lb0, h0, w["lru_conv_w"], w["lru_conv_b"], w["lru_w_a"], w["lru_b_a"],
               w["lru_w_x"], w["lru_b_x"], w["lru_lambda"], tt=lru_tt)
    x1 = _gated_out(o_a.reshape(m, D_A), u, h.reshape(m, D_B), x2d, w["w_out_ab"], tm=tm_out)

    new_h = h[:, t - 1]
    new_lb = u3[:, t - (CONV_B - 1):, 4 * D_A:4 * D_A + D_B]

    z, sg = _norm_glu(x1, w["norm_cv"], w["w_in_cv"], tm=tm, tn=512)
    z3 = z.reshape(b, t, D_C)
    y = _dwconv_ln(z3, cb0, sg.reshape(b, t, D_C), w["dw_w"], w["dw_b"], w["ln_g"], w["ln_b"], tt=conv_tt)
    out = _out_norm(y.reshape(m, D_C), w["w_out_cv"], x1, w["final_norm"], tm=tm_out)
    new_cb = z3[:, t - (CONV_C - 1):]

    return (out.reshape(b, t, D_MODEL), new_k[None], new_v[None], new_h[None], new_lb[None], new_cb[None])


def kernel(x_prompt, x_sample, cache_attn_k, cache_attn_v, state_lru_h, state_lru_conv, state_conv, norm_ab, w_in_ab, w_out_ab, rel_bias, lru_conv_w, lru_conv_b, lru_w_a, lru_b_a, lru_w_x, lru_b_x, lru_lambda, norm_cv, w_in_cv, w_out_cv, dw_w, dw_b, ln_g, ln_b, final_norm):
    assert norm_ab.shape[0] == 1 and norm_cv.shape[0] == 1, "one even and one odd layer"
    t_s = x_sample.shape[1]
    w = {
        "norm_ab": norm_ab[0].reshape(1, D_MODEL),
        "w_in_ab": w_in_ab[0].astype(BF16),
        "w_out_ab": w_out_ab[0].astype(BF16),
        "rel_bias_vec": _rel_bias_vector(rel_bias[0]),
        "lru_conv_w": lru_conv_w[0],
        "lru_conv_b": lru_conv_b[0],
        "lru_w_a": lru_w_a[0].astype(BF16),
        "lru_b_a": lru_b_a[0],
        "lru_w_x": lru_w_x[0].astype(BF16),
        "lru_b_x": lru_b_x[0],
        "lru_lambda": lru_lambda[0],
        "norm_cv": norm_cv[0].reshape(1, D_MODEL),
        "w_in_cv": w_in_cv[0].astype(BF16),
        "w_out_cv": w_out_cv[0].astype(BF16),
        "dw_w": dw_w[0],
        "dw_b": dw_b[0],
        "ln_g": ln_g[0],
        "ln_b": ln_b[0],
        "final_norm": final_norm.reshape(1, D_MODEL),
    }
    y_p, k_p, v_p, h_p, lb_p, cb_p = _trunk(x_prompt, None, w, tm=1024, tm_out=512, lru_tt=256, conv_tt=256)
    caches = (cache_attn_k[0], cache_attn_v[0], state_lru_h[0], state_lru_conv[0], state_conv[0])
    y_s, k_s, v_s, h_s, lb_s, cb_s = _trunk(x_sample, caches, w, tm=512, tm_out=512, lru_tt=t_s, conv_tt=t_s)
    return (y_p, y_s, k_p, v_p, h_p, lb_p, cb_p, k_s, v_s, h_s, lb_s, cb_s)
```

```python
import functools

import jax
import jax.numpy as jnp
from jax import lax
from jax.experimental import pallas as pl
from jax.experimental.pallas import tpu as pltpu

D_MODEL = 2048
CHUNK = 64
LEFT_CHUNKS = 8
BAND_PAST = CHUNK * LEFT_CHUNKS
N_HEADS = 8
HEAD_DIM = 128
D_A = N_HEADS * HEAD_DIM
REL_CLIP = 128
D_B = 1024
N_BLOCKS_B = 8
BLOCK_B = D_B // N_BLOCKS_B
CONV_B = 4
LRU_C = 8.0
D_C = D_MODEL
CONV_C = 31
IN_AB = 4 * D_A + 2 * D_B
IN_CV = 3 * D_C
EPS = 1e-6

F32 = jnp.float32
BF16 = jnp.bfloat16

SUBLANES = 8
Q_TILE = 2 * CHUNK
K_WIN = BAND_PAST + Q_TILE
ATT_BLOCK = 512
CONV_HALO = 32
CONV_ROWS = 32
CONV_COLS = 512
LRU_HALO = 8
VMEM_LIMIT = 56 * 1024 * 1024


def _cparams(sem):
    return pltpu.CompilerParams(dimension_semantics=sem, vmem_limit_bytes=VMEM_LIMIT)


def _sigmoid(x):
    return jax.nn.sigmoid(x)


def _norm_to_bf16(x, g):
    ms = jnp.mean(x * x, axis=-1, keepdims=True)
    return (x * lax.rsqrt(ms + EPS) * g).astype(BF16)


def _norm_matmul_kernel(x_ref, g_ref, w_ref, o_ref, xn_ref):
    @pl.when(pl.program_id(1) == 0)
    def _():
        xn_ref[...] = _norm_to_bf16(x_ref[...], g_ref[...])

    o_ref[...] = jnp.dot(xn_ref[...], w_ref[...], preferred_element_type=F32)


def _norm_matmul(x, g, w, *, tm, tn):
    m, d = x.shape
    n = w.shape[1]
    return pl.pallas_call(
        _norm_matmul_kernel,
        grid=(m // tm, n // tn),
        in_specs=[
            pl.BlockSpec((tm, d), lambda i, j: (i, 0)),
            pl.BlockSpec((1, d), lambda i, j: (0, 0)),
            pl.BlockSpec((d, tn), lambda i, j: (0, j)),
        ],
        out_specs=pl.BlockSpec((tm, tn), lambda i, j: (i, j)),
        out_shape=jax.ShapeDtypeStruct((m, n), F32),
        scratch_shapes=[pltpu.VMEM((tm, d), BF16)],
        compiler_params=_cparams(("parallel", "arbitrary")),
        name="norm_in_proj",
    )(x, g, w)


def _dwconv_rows(zext_ref, base, w_ref, b_ref):
    acc = jnp.broadcast_to(b_ref[...], (CONV_ROWS, b_ref.shape[1]))
    for phase in range(SUBLANES):
        part = None
        for k in range(CONV_C):
            off = base + CONV_HALO - (CONV_C - 1) + k
            if off % SUBLANES != phase:
                continue
            term = w_ref[k:k + 1, :] * zext_ref[off:off + CONV_ROWS, :]
            part = term if part is None else part + term
        if part is not None:
            acc = acc + part
    return acc


def _norm_glu_conv_kernel(x_ref, g_ref, wv_ref, wg_ref, wt_ref, st_ref, cw_ref, cb_ref,
                          cv_ref, sg_ref, tail_ref, xn_ref, zext_ref, carry_ref, *, seg, tiles_per_batch):
    i = pl.program_id(0)
    j = pl.program_id(1)
    n_seg = x_ref.shape[0] // seg
    ext = CONV_HALO + seg

    @pl.when(j == 0)
    def _():
        xn_ref[...] = _norm_to_bf16(x_ref[...], g_ref[...])

    xn = xn_ref[...]
    gate = jnp.dot(xn, wt_ref[...], preferred_element_type=F32)
    sg_ref[...] = (gate * _sigmoid(gate)).astype(BF16)
    val = jnp.dot(xn, wv_ref[...], preferred_element_type=F32)
    glu = jnp.dot(xn, wg_ref[...], preferred_element_type=F32)
    z = val * _sigmoid(glu)

    carried = tiles_per_batch > 1
    if carried:
        @pl.when(i % tiles_per_batch == 0)
        def _():
            carry_ref[j] = st_ref[0]

    for s in range(n_seg):
        base = s * ext
        zext_ref[base:base + CONV_HALO, :] = carry_ref[j] if carried else st_ref[s]
        zext_ref[base + CONV_HALO:base + ext, :] = z[s * seg:(s + 1) * seg]
        for r0 in range(0, seg, CONV_ROWS):
            cv_ref[s * seg + r0:s * seg + r0 + CONV_ROWS, :] = _dwconv_rows(zext_ref, base + r0, cw_ref, cb_ref)
        tail = zext_ref[base + seg:base + ext, :]
        tail_ref[s] = tail
        if carried:
            carry_ref[j] = tail


def _norm_glu_conv(x, g, w, state, dw_w, dw_b, *, t, tm, tn):
    m, d = x.shape
    nb = D_C // tn
    seg = min(t, tm)
    assert tm % seg == 0 and t % seg == 0 and seg % CONV_ROWS == 0
    tiles_per_batch = t // seg
    bpt = tm // seg if tiles_per_batch == 1 else 1
    batch_blk = (lambda i: i // tiles_per_batch) if tiles_per_batch > 1 else (lambda i: i)
    state_spec = pl.BlockSpec((bpt, CONV_HALO, tn), lambda i, j: (batch_blk(i), 0, j))
    body = functools.partial(_norm_glu_conv_kernel, seg=seg, tiles_per_batch=tiles_per_batch)
    return pl.pallas_call(
        body,
        grid=(m // tm, nb),
        in_specs=[
            pl.BlockSpec((tm, d), lambda i, j: (i, 0)),
            pl.BlockSpec((1, d), lambda i, j: (0, 0)),
            pl.BlockSpec((d, tn), lambda i, j: (0, j)),
            pl.BlockSpec((d, tn), lambda i, j: (0, j + nb)),
            pl.BlockSpec((d, tn), lambda i, j: (0, j + 2 * nb)),
            state_spec,
            pl.BlockSpec((CONV_C, tn), lambda i, j: (0, j)),
            pl.BlockSpec((1, tn), lambda i, j: (0, j)),
        ],
        out_specs=[
            pl.BlockSpec((tm, tn), lambda i, j: (i, j)),
            pl.BlockSpec((tm, tn), lambda i, j: (i, j)),
            pl.BlockSpec((tm // seg, CONV_HALO, tn), lambda i, j: (i, 0, j)),
        ],
        out_shape=[
            jax.ShapeDtypeStruct((m, D_C), F32),
            jax.ShapeDtypeStruct((m, D_C), BF16),
            jax.ShapeDtypeStruct((m // seg, CONV_HALO, D_C), F32),
        ],
        scratch_shapes=[
            pltpu.VMEM((tm, d), BF16),
            pltpu.VMEM(((tm // seg) * (CONV_HALO + seg), tn), F32),
            pltpu.VMEM((nb, CONV_HALO, tn), F32),
        ],
        compiler_params=_cparams(("arbitrary", "arbitrary")),
        name="norm_in_proj_glu_conv",
    )(x, g, w, w, w, state, dw_w, dw_b.reshape(1, D_C))


def _attend(q, kw, vw, bias, kpos0):
    s = lax.dot_general(q, kw, (((1,), (1,)), ((), ())), preferred_element_type=F32)
    s = s * (HEAD_DIM ** -0.5) + bias
    if kpos0 is not None:
        col = lax.broadcasted_iota(jnp.int32, s.shape, 1)
        s = jnp.where(col + kpos0 >= 0, s, -jnp.inf)
    m = jnp.max(s, axis=-1, keepdims=True)
    p = jnp.exp(s - m)
    l = jnp.sum(p, axis=-1, keepdims=True)
    o = jnp.dot(p.astype(BF16), vw, preferred_element_type=F32)
    return o / l


def _expand_rel_bias(fvec_ref, bias_ref, n_valid, chunk_masks):
    tq = bias_ref.shape[1]
    i = lax.broadcasted_iota(jnp.int32, (tq, K_WIN), 0)
    j = lax.broadcasted_iota(jnp.int32, (tq, K_WIN), 1)
    if chunk_masks:
        lo = jnp.where(i < CHUNK, 0, CHUNK)
        hi = jnp.where(i < CHUNK, min(BAND_PAST + CHUNK, n_valid), n_valid)
        hidden = (j < lo) | (j >= hi)
    else:
        hidden = j >= n_valid
    for h in range(N_HEADS):
        rows = jnp.broadcast_to(fvec_ref[h:h + 1, :], (tq, Q_TILE + K_WIN))
        skew = pltpu.roll(rows, 0, 1, stride=1, stride_axis=0)
        bias_ref[h] = jnp.where(hidden, -jnp.inf, skew[:, Q_TILE:])


def _attn_prompt_kernel(q_ref, kp_ref, kc_ref, vp_ref, vc_ref, fvec_ref, o_ref, nk_ref, nv_ref,
                        kcat_ref, vcat_ref, bias_ref):
    b = pl.program_id(0)
    t = pl.program_id(1)

    @pl.when((b == 0) & (t == 0))
    def _():
        _expand_rel_bias(fvec_ref, bias_ref, K_WIN, True)

    @pl.when(t == pl.num_programs(1) - 1)
    def _():
        for h in range(N_HEADS):
            c0 = h * HEAD_DIM
            nk_ref[:, h, :] = kc_ref[:, c0:c0 + HEAD_DIM]
            nv_ref[:, h, :] = vc_ref[:, c0:c0 + HEAD_DIM]

    kcat_ref[0:ATT_BLOCK, :] = kp_ref[...].astype(BF16)
    kcat_ref[ATT_BLOCK:, :] = kc_ref[...].astype(BF16)
    vcat_ref[0:ATT_BLOCK, :] = vp_ref[...].astype(BF16)
    vcat_ref[ATT_BLOCK:, :] = vc_ref[...].astype(BF16)
    for qi in range(ATT_BLOCK // Q_TILE):
        r0 = qi * Q_TILE
        kpos0 = t * ATT_BLOCK + r0 - BAND_PAST
        for h in range(N_HEADS):
            c0 = h * HEAD_DIM
            q = q_ref[r0:r0 + Q_TILE, c0:c0 + HEAD_DIM].astype(BF16)
            kw = kcat_ref[r0:r0 + K_WIN, c0:c0 + HEAD_DIM]
            vw = vcat_ref[r0:r0 + K_WIN, c0:c0 + HEAD_DIM]
            o_ref[r0:r0 + Q_TILE, c0:c0 + HEAD_DIM] = _attend(q, kw, vw, bias_ref[h], kpos0)


def _attn_prompt(u, fvec):
    b, s, _ = u.shape
    blk = (None, ATT_BLOCK, D_A)
    prev = lambda col: (lambda bi, t: (bi, jnp.maximum(t - 1, 0), col))
    cur = lambda col: (lambda bi, t: (bi, t, col))
    state = pl.BlockSpec((None, ATT_BLOCK, N_HEADS, HEAD_DIM), lambda bi, t: (bi, 0, 0, 0))
    state_shape = jax.ShapeDtypeStruct((b, ATT_BLOCK, N_HEADS, HEAD_DIM), F32)
    return pl.pallas_call(
        _attn_prompt_kernel,
        grid=(b, s // ATT_BLOCK),
        in_specs=[
            pl.BlockSpec(blk, cur(0)),
            pl.BlockSpec(blk, prev(1)),
            pl.BlockSpec(blk, cur(1)),
            pl.BlockSpec(blk, prev(2)),
            pl.BlockSpec(blk, cur(2)),
            pl.BlockSpec(fvec.shape, lambda bi, t: (0, 0)),
        ],
        out_specs=[pl.BlockSpec(blk, cur(0)), state, state],
        out_shape=[jax.ShapeDtypeStruct((b, s, D_A), F32), state_shape, state_shape],
        scratch_shapes=[
            pltpu.VMEM((2 * ATT_BLOCK, D_A), BF16),
            pltpu.VMEM((2 * ATT_BLOCK, D_A), BF16),
            pltpu.VMEM((N_HEADS, Q_TILE, K_WIN), F32),
        ],
        compiler_params=_cparams(("arbitrary", "arbitrary")),
        name="attn_prompt",
    )(u, u, u, u, u, fvec)


def _attn_sample_kernel(q_ref, kn_ref, vn_ref, kc_ref, vc_ref, fvec_ref, o_ref, nk_ref, nv_ref,
                        kcat_ref, vcat_ref, bias_ref):
    tq = q_ref.shape[0]
    lc = kc_ref.shape[0]

    @pl.when(pl.program_id(0) == 0)
    def _():
        _expand_rel_bias(fvec_ref, bias_ref, lc + tq, False)
        kcat_ref[lc + tq:, :] = jnp.zeros((K_WIN - lc - tq, D_A), BF16)
        vcat_ref[lc + tq:, :] = jnp.zeros((K_WIN - lc - tq, D_A), BF16)

    kcat_ref[lc:lc + tq, :] = kn_ref[...].astype(BF16)
    vcat_ref[lc:lc + tq, :] = vn_ref[...].astype(BF16)
    for h in range(N_HEADS):
        c0 = h * HEAD_DIM
        nk_ref[:, h, :] = kn_ref[:, c0:c0 + HEAD_DIM]
        nv_ref[:, h, :] = vn_ref[:, c0:c0 + HEAD_DIM]
        kcat_ref[0:lc, c0:c0 + HEAD_DIM] = kc_ref[:, h, :].astype(BF16)
        vcat_ref[0:lc, c0:c0 + HEAD_DIM] = vc_ref[:, h, :].astype(BF16)
        q = q_ref[:, c0:c0 + HEAD_DIM].astype(BF16)
        kw = kcat_ref[:, c0:c0 + HEAD_DIM]
        vw = vcat_ref[:, c0:c0 + HEAD_DIM]
        o_ref[:, c0:c0 + HEAD_DIM] = _attend(q, kw, vw, bias_ref[h], None)


def _attn_sample(u, k_cache, v_cache, fvec):
    b, t, _ = u.shape
    lc = k_cache.shape[1]
    assert lc + t <= K_WIN and t <= Q_TILE
    new = lambda col: pl.BlockSpec((None, t, D_A), lambda bi: (bi, 0, col))
    cache = pl.BlockSpec((None, lc, N_HEADS, HEAD_DIM), lambda bi: (bi, 0, 0, 0))
    state = pl.BlockSpec((None, t, N_HEADS, HEAD_DIM), lambda bi: (bi, 0, 0, 0))
    state_shape = jax.ShapeDtypeStruct((b, t, N_HEADS, HEAD_DIM), F32)
    return pl.pallas_call(
        _attn_sample_kernel,
        grid=(b,),
        in_specs=[new(0), new(1), new(2), cache, cache, pl.BlockSpec(fvec.shape, lambda bi: (0, 0))],
        out_specs=[pl.BlockSpec((None, t, D_A), lambda bi: (bi, 0, 0)), state, state],
        out_shape=[jax.ShapeDtypeStruct((b, t, D_A), F32), state_shape, state_shape],
        scratch_shapes=[
            pltpu.VMEM((K_WIN, D_A), BF16),
            pltpu.VMEM((K_WIN, D_A), BF16),
            pltpu.VMEM((N_HEADS, t, K_WIN), F32),
        ],
        compiler_params=_cparams(("arbitrary",)),
        name="attn_sample",
    )(u, u, u, k_cache, v_cache, fvec)


def _rel_bias_vector(rel_table):
    width = Q_TILE + K_WIN
    n_const = BAND_PAST + Q_TILE - REL_CLIP + 1
    n_rev = width - n_const
    assert 0 < n_rev <= 2 * REL_CLIP
    const = jnp.broadcast_to(rel_table[:, 2 * REL_CLIP:], (N_HEADS, n_const))
    rev = lax.rev(rel_table[:, 2 * REL_CLIP - n_rev:2 * REL_CLIP], (1,))
    return jnp.concatenate([const, rev], axis=1).astype(F32)


def _log_sigmoid(x):
    return -(jnp.maximum(-x, 0.0) + jnp.log1p(jnp.exp(-jnp.abs(x))))


def _shift_rows(x, d, fill):
    n = x.shape[0]
    if d % SUBLANES == 0:
        return jnp.concatenate([jnp.full((d, x.shape[1]), fill, x.dtype), x[:n - d]], axis=0)
    rolled = pltpu.roll(x, d, 0)
    row = lax.broadcasted_iota(jnp.int32, x.shape, 0)
    return jnp.where(row < d, fill, rolled)


def _rglru_kernel(xb_ref, lb_ref, h0_ref, cw_ref, cb_ref, wa_ref, ba_ref, wx_ref, bx_ref, lam_ref,
                  h_ref, xext_ref, hc_ref):
    t = pl.program_id(1)
    tt = xb_ref.shape[0]

    @pl.when(t == 0)
    def _():
        xext_ref[0:LRU_HALO, :] = lb_ref[...]
        hc_ref[...] = h0_ref[...]

    xext_ref[LRU_HALO:, :] = xb_ref[...]
    xc = cb_ref[...]
    for k in range(CONV_B):
        off = LRU_HALO - (CONV_B - 1) + k
        xc = xc + cw_ref[k:k + 1, :] * xext_ref[off:off + tt, :]
    xext_ref[0:LRU_HALO, :] = xext_ref[tt:tt + LRU_HALO, :]

    xcb = xc.astype(BF16)
    ra, rx = [], []
    for n in range(N_BLOCKS_B):
        blk = xcb[:, n * BLOCK_B:(n + 1) * BLOCK_B]
        ra.append(jnp.dot(blk, wa_ref[n], preferred_element_type=F32))
        rx.append(jnp.dot(blk, wx_ref[n], preferred_element_type=F32))
    r = _sigmoid(jnp.concatenate(ra, axis=-1) + ba_ref[...])
    i = _sigmoid(jnp.concatenate(rx, axis=-1) + bx_ref[...])
    log_a = LRU_C * r * _log_sigmoid(lam_ref[...])
    a = jnp.exp(log_a)
    th = jnp.tanh(log_a)
    bt = jnp.sqrt(-2.0 * th / (1.0 - th)) * (i * xc)

    d = 1
    while d < tt:
        bt = bt + a * _shift_rows(bt, d, 0.0)
        a = a * _shift_rows(a, d, 1.0)
        d *= 2
    h = a * hc_ref[...] + bt
    h_ref[...] = h
    hc_ref[...] = h[tt - 1:tt, :]


def _rglru(u, lb0, h0, cw, cb, w_a, b_a, w_x, b_x, lam, *, tt):
    b, t, _ = u.shape
    row = lambda a: a.reshape(1, D_B)
    full = lambda a: pl.BlockSpec(a.shape, lambda bi, ti: (0,) * a.ndim)
    args = (cw, row(cb), w_a, row(b_a), w_x, row(b_x), row(lam))
    return pl.pallas_call(
        _rglru_kernel,
        grid=(b, t // tt),
        in_specs=[
            pl.BlockSpec((None, tt, D_B), lambda bi, ti: (bi, ti, 4 * D_A // D_B)),
            pl.BlockSpec((None, LRU_HALO, D_B), lambda bi, ti: (bi, 0, 0)),
            pl.BlockSpec((None, 1, D_B), lambda bi, ti: (bi, 0, 0)),
        ] + [full(a) for a in args],
        out_specs=pl.BlockSpec((None, tt, D_B), lambda bi, ti: (bi, ti, 0)),
        out_shape=jax.ShapeDtypeStruct((b, t, D_B), F32),
        scratch_shapes=[pltpu.VMEM((LRU_HALO + tt, D_B), F32), pltpu.VMEM((1, D_B), F32)],
        compiler_params=_cparams(("parallel", "arbitrary")),
        name="rglru",
    )(u, lb0, h0, *args)


def _gated_out_kernel(oa_ref, ga_ref, h_ref, gb_ref, x_ref, w_ref, o_ref):
    ga = ga_ref[...]
    gb = gb_ref[...]
    ma = (ga * _sigmoid(ga) * oa_ref[...]).astype(BF16)
    mb = (gb * _sigmoid(gb) * h_ref[...]).astype(BF16)
    acc = jnp.dot(ma, w_ref[0:D_A, :], preferred_element_type=F32)
    acc = acc + jnp.dot(mb, w_ref[D_A:, :], preferred_element_type=F32)
    o_ref[...] = x_ref[...] + acc


def _gated_out(o_a, u, h, x, w, *, tm):
    m = x.shape[0]
    return pl.pallas_call(
        _gated_out_kernel,
        grid=(m // tm,),
        in_specs=[
            pl.BlockSpec((tm, D_A), lambda i: (i, 0)),
            pl.BlockSpec((tm, D_A), lambda i: (i, 3)),
            pl.BlockSpec((tm, D_B), lambda i: (i, 0)),
            pl.BlockSpec((tm, D_B), lambda i: (i, (4 * D_A + D_B) // D_B)),
            pl.BlockSpec((tm, D_MODEL), lambda i: (i, 0)),
            pl.BlockSpec(w.shape, lambda i: (0, 0), pipeline_mode=pl.Buffered(1)),
        ],
        out_specs=pl.BlockSpec((tm, D_MODEL), lambda i: (i, 0)),
        out_shape=jax.ShapeDtypeStruct((m, D_MODEL), F32),
        compiler_params=_cparams(("parallel",)),
        name="gated_out_proj",
    )(o_a, u, h, u, x, w)


def _ln_out_norm_kernel(cv_ref, sg_ref, lg_ref, lb_ref, w_ref, x_ref, g_ref, o_ref):
    acc = cv_ref[...]
    mu = jnp.mean(acc, axis=-1, keepdims=True)
    cen = acc - mu
    var = jnp.mean(cen * cen, axis=-1, keepdims=True)
    yn = cen * lax.rsqrt(var + EPS) * lg_ref[...] + lb_ref[...]
    y = (yn * _sigmoid(yn) * sg_ref[...].astype(F32)).astype(BF16)
    x = x_ref[...] + jnp.dot(y, w_ref[...], preferred_element_type=F32)
    ms = jnp.mean(x * x, axis=-1, keepdims=True)
    o_ref[...] = x * lax.rsqrt(ms + EPS) * g_ref[...]


def _ln_out_norm(cv, sg, ln_g, ln_b, w, x, g, *, tm):
    m = x.shape[0]
    row = pl.BlockSpec((1, D_C), lambda i: (0, 0))
    return pl.pallas_call(
        _ln_out_norm_kernel,
        grid=(m // tm,),
        in_specs=[
            pl.BlockSpec((tm, D_C), lambda i: (i, 0)),
            pl.BlockSpec((tm, D_C), lambda i: (i, 0)),
            row,
            row,
            pl.BlockSpec(w.shape, lambda i: (0, 0), pipeline_mode=pl.Buffered(1)),
            pl.BlockSpec((tm, D_MODEL), lambda i: (i, 0)),
            pl.BlockSpec((1, D_MODEL), lambda i: (0, 0)),
        ],
        out_specs=pl.BlockSpec((tm, D_MODEL), lambda i: (i, 0)),
        out_shape=jax.ShapeDtypeStruct((m, D_MODEL), F32),
        compiler_params=_cparams(("parallel",)),
        name="ln_out_proj_final_norm",
    )(cv, sg, ln_g.reshape(1, D_C), ln_b.reshape(1, D_C), w, x, g)


def _front_pad_rows(a, rows):
    return jnp.pad(a, ((0, 0), (rows - a.shape[1], 0), (0, 0)))


def _trunk(x, caches, w, *, tm, tm_out, lru_tt, conv_tt):
    b, t, _ = x.shape
    m = b * t
    x2d = x.reshape(m, D_MODEL)

    u = _norm_matmul(x2d, w["norm_ab"], w["w_in_ab"], tm=tm, tn=512)
    u3 = u.reshape(b, t, IN_AB)
    if caches is None:
        assert t % ATT_BLOCK == 0 and ATT_BLOCK == BAND_PAST
        o_a, new_k, new_v = _attn_prompt(u3, w["rel_bias_vec"])
        h0 = jnp.zeros((b, 1, D_B), F32)
        lb0 = jnp.zeros((b, LRU_HALO, D_B), F32)
        cb0 = jnp.zeros((b, CONV_HALO, D_C), F32)
    else:
        kc, vc, hc, lbc, cbc = caches
        o_a, new_k, new_v = _attn_sample(u3, kc, vc, w["rel_bias_vec"])
        h0 = hc.reshape(b, 1, D_B)
        lb0 = _front_pad_rows(lbc, LRU_HALO)
        cb0 = _front_pad_rows(cbc, CONV_HALO)
    h = _rglru(u3, lb0, h0, w["lru_conv_w"], w["lru_conv_b"], w["lru_w_a"], w["lru_b_a"],
               w["lru_w_x"], w["lru_b_x"], w["lru_lambda"], tt=lru_tt)
    x1 = _gated_out(o_a.reshape(m, D_A), u, h.reshape(m, D_B), x2d, w["w_out_ab"], tm=tm_out)

    new_h = h[:, t - 1]
    new_lb = u3[:, t - (CONV_B - 1):, 4 * D_A:4 * D_A + D_B]

    cv, sg, z_tail = _norm_glu_conv(x1, w["norm_cv"], w["w_in_cv"], cb0, w["dw_w"], w["dw_b"], t=t, tm=tm, tn=512)
    out = _ln_out_norm(cv, sg, w["ln_g"], w["ln_b"], w["w_out_cv"], x1, w["final_norm"], tm=tm_out)
    segs_per_batch = z_tail.shape[0] // b
    new_cb = z_tail[segs_per_batch - 1::segs_per_batch, CONV_HALO - (CONV_C - 1):]

    return (out.reshape(b, t, D_MODEL), new_k[None], new_v[None], new_h[None], new_lb[None], new_cb[None])


def kernel(x_prompt, x_sample, cache_attn_k, cache_attn_v, state_lru_h, state_lru_conv, state_conv, norm_ab, w_in_ab, w_out_ab, rel_bias, lru_conv_w, lru_conv_b, lru_w_a, lru_b_a, lru_w_x, lru_b_x, lru_lambda, norm_cv, w_in_cv, w_out_cv, dw_w, dw_b, ln_g, ln_b, final_norm):
    assert norm_ab.shape[0] == 1 and norm_cv.shape[0] == 1, "one even and one odd layer"
    t_s = x_sample.shape[1]
    w = {
        "norm_ab": norm_ab[0].reshape(1, D_MODEL),
        "w_in_ab": w_in_ab[0].astype(BF16),
        "w_out_ab": w_out_ab[0].astype(BF16),
        "rel_bias_vec": _rel_bias_vector(rel_bias[0]),
        "lru_conv_w": lru_conv_w[0],
        "lru_conv_b": lru_conv_b[0],
        "lru_w_a": lru_w_a[0].astype(BF16),
        "lru_b_a": lru_b_a[0],
        "lru_w_x": lru_w_x[0].astype(BF16),
        "lru_b_x": lru_b_x[0],
        "lru_lambda": lru_lambda[0],
        "norm_cv": norm_cv[0].reshape(1, D_MODEL),
        "w_in_cv": w_in_cv[0].astype(BF16),
        "w_out_cv": w_out_cv[0].astype(BF16),
        "dw_w": dw_w[0],
        "dw_b": dw_b[0],
        "ln_g": ln_g[0],
        "ln_b": ln_b[0],
        "final_norm": final_norm.reshape(1, D_MODEL),
    }
    y_p, k_p, v_p, h_p, lb_p, cb_p = _trunk(x_prompt, None, w, tm=1024, tm_out=512, lru_tt=256, conv_tt=256)
    caches = (cache_attn_k[0], cache_attn_v[0], state_lru_h[0], state_lru_conv[0], state_conv[0])
    y_s, k_s, v_s, h_s, lb_s, cb_s = _trunk(x_sample, caches, w, tm=512, tm_out=512, lru_tt=t_s, conv_tt=t_s)
    return (y_p, y_s, k_p, v_p, h_p, lb_p, cb_p, k_s, v_s, h_s, lb_s, cb_s)
```

```python
import functools

import jax
import jax.numpy as jnp
from jax import lax
from jax.experimental import pallas as pl
from jax.experimental.pallas import tpu as pltpu

D_MODEL = 2048
CHUNK = 64
LEFT_CHUNKS = 8
BAND_PAST = CHUNK * LEFT_CHUNKS
N_HEADS = 8
HEAD_DIM = 128
D_A = N_HEADS * HEAD_DIM
REL_CLIP = 128
D_B = 1024
N_BLOCKS_B = 8
BLOCK_B = D_B // N_BLOCKS_B
CONV_B = 4
LRU_C = 8.0
D_C = D_MODEL
CONV_C = 31
IN_AB = 4 * D_A + 2 * D_B
IN_CV = 3 * D_C
EPS = 1e-6

F32 = jnp.float32
BF16 = jnp.bfloat16

SUBLANES = 8
LANES = 128
Q_TILE = 2 * CHUNK
K_WIN = BAND_PAST + Q_TILE
ATT_BLOCK = 512
CONV_HALO = 32
CONV_ROWS = 128
LRU_HALO = 8
VMEM_LIMIT = 56 * 1024 * 1024


def _cparams(sem):
    return pltpu.CompilerParams(dimension_semantics=sem, vmem_limit_bytes=VMEM_LIMIT)


def _sigmoid(x):
    return jax.nn.sigmoid(x)


def _norm_to_bf16(x, g):
    ms = jnp.mean(x * x, axis=-1, keepdims=True)
    return (x * lax.rsqrt(ms + EPS) * g).astype(BF16)


def _norm_matmul_kernel(x_ref, g_ref, w_ref, o_ref, xn_ref):
    @pl.when(pl.program_id(1) == 0)
    def _():
        xn_ref[...] = _norm_to_bf16(x_ref[...], g_ref[...])

    o_ref[...] = jnp.dot(xn_ref[...], w_ref[...], preferred_element_type=F32)


def _norm_matmul(x, g, w, *, tm, tn):
    m, d = x.shape
    n = w.shape[1]
    return pl.pallas_call(
        _norm_matmul_kernel,
        grid=(m // tm, n // tn),
        in_specs=[
            pl.BlockSpec((tm, d), lambda i, j: (i, 0)),
            pl.BlockSpec((1, d), lambda i, j: (0, 0)),
            pl.BlockSpec((d, tn), lambda i, j: (0, j)),
        ],
        out_specs=pl.BlockSpec((tm, tn), lambda i, j: (i, j)),
        out_shape=jax.ShapeDtypeStruct((m, n), F32),
        scratch_shapes=[pltpu.VMEM((tm, d), BF16)],
        compiler_params=_cparams(("parallel", "arbitrary")),
        name="norm_in_proj",
    )(x, g, w)


def _dwconv_rows(zext_ref, base, rows, lane0, w_ref, b_ref):
    lanes = slice(lane0, lane0 + LANES)
    acc = jnp.broadcast_to(b_ref[...], (rows, LANES))
    for phase in range(SUBLANES):
        win_rows = rows if phase == 0 else rows + SUBLANES
        part = None
        for k in range(CONV_C):
            off = base + CONV_HALO - (CONV_C - 1) + k
            if off % SUBLANES != phase:
                continue
            tap = w_ref[k]
            win = zext_ref[off - phase:off - phase + win_rows, lanes]
            term = win.reshape(win_rows // SUBLANES, SUBLANES, LANES) * tap[None]
            part = term if part is None else part + term
        if part is None:
            continue
        part = part.reshape(win_rows, LANES)
        if phase:
            part = pltpu.roll(part, win_rows - phase, 0)[:rows]
        acc = acc + part
    return acc


def _norm_glu_conv_kernel(x_ref, g_ref, wv_ref, wg_ref, wt_ref, st_ref, cw_ref, cb_ref,
                          cv_ref, sg_ref, tail_ref, xn_ref, zext_ref, carry_ref, *, seg, tiles_per_batch):
    i = pl.program_id(0)
    j = pl.program_id(1)
    n_seg = x_ref.shape[0] // seg
    ext = CONV_HALO + seg

    @pl.when(j == 0)
    def _():
        xn_ref[...] = _norm_to_bf16(x_ref[...], g_ref[...])

    xn = xn_ref[...]
    gate = jnp.dot(xn, wt_ref[...], preferred_element_type=F32)
    sg_ref[...] = (gate * _sigmoid(gate)).astype(BF16)
    val = jnp.dot(xn, wv_ref[...], preferred_element_type=F32)
    glu = jnp.dot(xn, wg_ref[...], preferred_element_type=F32)
    z = val * _sigmoid(glu)

    carried = tiles_per_batch > 1
    if carried:
        @pl.when(i % tiles_per_batch == 0)
        def _():
            carry_ref[j] = st_ref[0]

    rows = min(seg, CONV_ROWS)
    for s in range(n_seg):
        base = s * ext
        zext_ref[base:base + CONV_HALO, :] = carry_ref[j] if carried else st_ref[s]
        zext_ref[base + CONV_HALO:base + ext, :] = z[s * seg:(s + 1) * seg]
        for c in range(cw_ref.shape[0]):
            for r0 in range(0, seg, rows):
                out_rows = slice(s * seg + r0, s * seg + r0 + rows)
                cv_ref[out_rows, c * LANES:(c + 1) * LANES] = _dwconv_rows(
                    zext_ref, base + r0, rows, c * LANES, cw_ref.at[c], cb_ref.at[c])
        tail = zext_ref[base + seg:base + ext, :]
        tail_ref[s] = tail
        if carried:
            carry_ref[j] = tail


def _norm_glu_conv(x, g, w, state, dw_w, dw_b, *, t, tm, tn):
    m, d = x.shape
    nb = D_C // tn
    seg = min(t, tm)
    assert tm % seg == 0 and t % seg == 0 and seg % min(seg, CONV_ROWS) == 0 and seg % SUBLANES == 0
    tiles_per_batch = t // seg
    dw_w = dw_w.reshape(CONV_C, D_C // LANES, 1, LANES).transpose(1, 0, 2, 3)
    dw_w = jnp.broadcast_to(dw_w, (D_C // LANES, CONV_C, SUBLANES, LANES))
    dw_b = dw_b.reshape(D_C // LANES, 1, LANES)
    bpt = tm // seg if tiles_per_batch == 1 else 1
    batch_blk = (lambda i: i // tiles_per_batch) if tiles_per_batch > 1 else (lambda i: i)
    state_spec = pl.BlockSpec((bpt, CONV_HALO, tn), lambda i, j: (batch_blk(i), 0, j))
    body = functools.partial(_norm_glu_conv_kernel, seg=seg, tiles_per_batch=tiles_per_batch)
    return pl.pallas_call(
        body,
        grid=(m // tm, nb),
        in_specs=[
            pl.BlockSpec((tm, d), lambda i, j: (i, 0)),
            pl.BlockSpec((1, d), lambda i, j: (0, 0)),
            pl.BlockSpec((d, tn), lambda i, j: (0, j)),
            pl.BlockSpec((d, tn), lambda i, j: (0, j + nb)),
            pl.BlockSpec((d, tn), lambda i, j: (0, j + 2 * nb)),
            state_spec,
            pl.BlockSpec((tn // LANES, CONV_C, SUBLANES, LANES), lambda i, j: (j, 0, 0, 0)),
            pl.BlockSpec((tn // LANES, 1, LANES), lambda i, j: (j, 0, 0)),
        ],
        out_specs=[
            pl.BlockSpec((tm, tn), lambda i, j: (i, j)),
            pl.BlockSpec((tm, tn), lambda i, j: (i, j)),
            pl.BlockSpec((tm // seg, CONV_HALO, tn), lambda i, j: (i, 0, j)),
        ],
        out_shape=[
            jax.ShapeDtypeStruct((m, D_C), F32),
            jax.ShapeDtypeStruct((m, D_C), BF16),
            jax.ShapeDtypeStruct((m // seg, CONV_HALO, D_C), F32),
        ],
        scratch_shapes=[
            pltpu.VMEM((tm, d), BF16),
            pltpu.VMEM(((tm // seg) * (CONV_HALO + seg), tn), F32),
            pltpu.VMEM((nb, CONV_HALO, tn), F32),
        ],
        compiler_params=_cparams(("arbitrary", "arbitrary")),
        name="norm_in_proj_glu_conv",
    )(x, g, w, w, w, state, dw_w, dw_b)


def _attend(q, kw, vw, bias, kpos0):
    s = lax.dot_general(q, kw, (((1,), (1,)), ((), ())), preferred_element_type=F32)
    s = s * (HEAD_DIM ** -0.5) + bias
    if kpos0 is not None:
        col = lax.broadcasted_iota(jnp.int32, s.shape, 1)
        s = jnp.where(col + kpos0 >= 0, s, -jnp.inf)
    m = jnp.max(s, axis=-1, keepdims=True)
    p = jnp.exp(s - m)
    l = jnp.sum(p, axis=-1, keepdims=True)
    o = jnp.dot(p.astype(BF16), vw, preferred_element_type=F32)
    return o / l


def _heads_to_rows(x_ref):
    heads = [x_ref[:, h * HEAD_DIM:(h + 1) * HEAD_DIM] for h in range(N_HEADS)]
    return jnp.swapaxes(jnp.stack(heads, axis=0), 0, 1)


def _expand_rel_bias(fvec_ref, bias_ref, n_valid, chunk_masks):
    tq = bias_ref.shape[1]
    i = lax.broadcasted_iota(jnp.int32, (tq, K_WIN), 0)
    j = lax.broadcasted_iota(jnp.int32, (tq, K_WIN), 1)
    if chunk_masks:
        lo = jnp.where(i < CHUNK, 0, CHUNK)
        hi = jnp.where(i < CHUNK, min(BAND_PAST + CHUNK, n_valid), n_valid)
        hidden = (j < lo) | (j >= hi)
    else:
        hidden = j >= n_valid
    for h in range(N_HEADS):
        rows = jnp.broadcast_to(fvec_ref[h:h + 1, :], (tq, Q_TILE + K_WIN))
        skew = pltpu.roll(rows, 0, 1, stride=1, stride_axis=0)
        bias_ref[h] = jnp.where(hidden, -jnp.inf, skew[:, Q_TILE:])


def _attn_prompt_kernel(q_ref, kp_ref, kc_ref, vp_ref, vc_ref, fvec_ref, o_ref, nk_ref, nv_ref,
                        kcat_ref, vcat_ref, bias_ref):
    b = pl.program_id(0)
    t = pl.program_id(1)

    @pl.when((b == 0) & (t == 0))
    def _():
        _expand_rel_bias(fvec_ref, bias_ref, K_WIN, True)

    @pl.when(t == pl.num_programs(1) - 1)
    def _():
        nk_ref[...] = _heads_to_rows(kc_ref)
        nv_ref[...] = _heads_to_rows(vc_ref)

    kcat_ref[0:ATT_BLOCK, :] = kp_ref[...].astype(BF16)
    kcat_ref[ATT_BLOCK:, :] = kc_ref[...].astype(BF16)
    vcat_ref[0:ATT_BLOCK, :] = vp_ref[...].astype(BF16)
    vcat_ref[ATT_BLOCK:, :] = vc_ref[...].astype(BF16)
    for qi in range(ATT_BLOCK // Q_TILE):
        r0 = qi * Q_TILE
        kpos0 = t * ATT_BLOCK + r0 - BAND_PAST
        for h in range(N_HEADS):
            c0 = h * HEAD_DIM
            q = q_ref[r0:r0 + Q_TILE, c0:c0 + HEAD_DIM].astype(BF16)
            kw = kcat_ref[r0:r0 + K_WIN, c0:c0 + HEAD_DIM]
            vw = vcat_ref[r0:r0 + K_WIN, c0:c0 + HEAD_DIM]
            o_ref[r0:r0 + Q_TILE, c0:c0 + HEAD_DIM] = _attend(q, kw, vw, bias_ref[h], kpos0)


def _attn_prompt(u, fvec):
    b, s, _ = u.shape
    blk = (None, ATT_BLOCK, D_A)
    prev = lambda col: (lambda bi, t: (bi, jnp.maximum(t - 1, 0), col))
    cur = lambda col: (lambda bi, t: (bi, t, col))
    state = pl.BlockSpec((None, ATT_BLOCK, N_HEADS, HEAD_DIM), lambda bi, t: (bi, 0, 0, 0))
    state_shape = jax.ShapeDtypeStruct((b, ATT_BLOCK, N_HEADS, HEAD_DIM), F32)
    return pl.pallas_call(
        _attn_prompt_kernel,
        grid=(b, s // ATT_BLOCK),
        in_specs=[
            pl.BlockSpec(blk, cur(0)),
            pl.BlockSpec(blk, prev(1)),
            pl.BlockSpec(blk, cur(1)),
            pl.BlockSpec(blk, prev(2)),
            pl.BlockSpec(blk, cur(2)),
            pl.BlockSpec(fvec.shape, lambda bi, t: (0, 0)),
        ],
        out_specs=[pl.BlockSpec(blk, cur(0)), state, state],
        out_shape=[jax.ShapeDtypeStruct((b, s, D_A), F32), state_shape, state_shape],
        scratch_shapes=[
            pltpu.VMEM((2 * ATT_BLOCK, D_A), BF16),
            pltpu.VMEM((2 * ATT_BLOCK, D_A), BF16),
            pltpu.VMEM((N_HEADS, Q_TILE, K_WIN), F32),
        ],
        compiler_params=_cparams(("arbitrary", "arbitrary")),
        name="attn_prompt",
    )(u, u, u, u, u, fvec)


def _attn_sample_kernel(q_ref, kn_ref, vn_ref, kc_ref, vc_ref, fvec_ref, o_ref, nk_ref, nv_ref,
                        kcat_ref, vcat_ref, bias_ref):
    tq = q_ref.shape[0]
    lc = kc_ref.shape[0]

    @pl.when(pl.program_id(0) == 0)
    def _():
        _expand_rel_bias(fvec_ref, bias_ref, lc + tq, False)
        kcat_ref[lc + tq:, :] = jnp.zeros((K_WIN - lc - tq, D_A), BF16)
        vcat_ref[lc + tq:, :] = jnp.zeros((K_WIN - lc - tq, D_A), BF16)

    kcat_ref[lc:lc + tq, :] = kn_ref[...].astype(BF16)
    vcat_ref[lc:lc + tq, :] = vn_ref[...].astype(BF16)
    nk_ref[...] = _heads_to_rows(kn_ref)
    nv_ref[...] = _heads_to_rows(vn_ref)
    kch = jnp.swapaxes(kc_ref[...], 0, 1)
    vch = jnp.swapaxes(vc_ref[...], 0, 1)
    for h in range(N_HEADS):
        c0 = h * HEAD_DIM
        kcat_ref[0:lc, c0:c0 + HEAD_DIM] = kch[h].astype(BF16)
        vcat_ref[0:lc, c0:c0 + HEAD_DIM] = vch[h].astype(BF16)
        q = q_ref[:, c0:c0 + HEAD_DIM].astype(BF16)
        kw = kcat_ref[:, c0:c0 + HEAD_DIM]
        vw = vcat_ref[:, c0:c0 + HEAD_DIM]
        o_ref[:, c0:c0 + HEAD_DIM] = _attend(q, kw, vw, bias_ref[h], None)


def _attn_sample(u, k_cache, v_cache, fvec):
    b, t, _ = u.shape
    lc = k_cache.shape[1]
    assert lc + t <= K_WIN and t <= Q_TILE
    new = lambda col: pl.BlockSpec((None, t, D_A), lambda bi: (bi, 0, col))
    cache = pl.BlockSpec((None, lc, N_HEADS, HEAD_DIM), lambda bi: (bi, 0, 0, 0))
    state = pl.BlockSpec((None, t, N_HEADS, HEAD_DIM), lambda bi: (bi, 0, 0, 0))
    state_shape = jax.ShapeDtypeStruct((b, t, N_HEADS, HEAD_DIM), F32)
    return pl.pallas_call(
        _attn_sample_kernel,
        grid=(b,),
        in_specs=[new(0), new(1), new(2), cache, cache, pl.BlockSpec(fvec.shape, lambda bi: (0, 0))],
        out_specs=[pl.BlockSpec((None, t, D_A), lambda bi: (bi, 0, 0)), state, state],
        out_shape=[jax.ShapeDtypeStruct((b, t, D_A), F32), state_shape, state_shape],
        scratch_shapes=[
            pltpu.VMEM((K_WIN, D_A), BF16),
            pltpu.VMEM((K_WIN, D_A), BF16),
            pltpu.VMEM((N_HEADS, t, K_WIN), F32),
        ],
        compiler_params=_cparams(("arbitrary",)),
        name="attn_sample",
    )(u, u, u, k_cache, v_cache, fvec)


def _rel_bias_vector(rel_table):
    width = Q_TILE + K_WIN
    n_const = BAND_PAST + Q_TILE - REL_CLIP + 1
    n_rev = width - n_const
    assert 0 < n_rev <= 2 * REL_CLIP
    const = jnp.broadcast_to(rel_table[:, 2 * REL_CLIP:], (N_HEADS, n_const))
    rev = lax.rev(rel_table[:, 2 * REL_CLIP - n_rev:2 * REL_CLIP], (1,))
    return jnp.concatenate([const, rev], axis=1).astype(F32)


def _log_sigmoid(x):
    return -(jnp.maximum(-x, 0.0) + jnp.log1p(jnp.exp(-jnp.abs(x))))


def _shift_rows(x, d, fill):
    n = x.shape[0]
    if d % SUBLANES == 0:
        return jnp.concatenate([jnp.full((d, x.shape[1]), fill, x.dtype), x[:n - d]], axis=0)
    rolled = pltpu.roll(x, d, 0)
    row = lax.broadcasted_iota(jnp.int32, x.shape, 0)
    return jnp.where(row < d, fill, rolled)


def _rglru_kernel(xb_ref, lb_ref, h0_ref, cw_ref, cb_ref, wa_ref, ba_ref, wx_ref, bx_ref, lam_ref,
                  h_ref, xext_ref, hc_ref):
    t = pl.program_id(1)
    tt = xb_ref.shape[0]

    @pl.when(t == 0)
    def _():
        xext_ref[0:LRU_HALO, :] = lb_ref[...]
        hc_ref[...] = h0_ref[...]

    xext_ref[LRU_HALO:, :] = xb_ref[...]
    xc = cb_ref[...]
    for k in range(CONV_B):
        off = LRU_HALO - (CONV_B - 1) + k
        xc = xc + cw_ref[k:k + 1, :] * xext_ref[off:off + tt, :]
    xext_ref[0:LRU_HALO, :] = xext_ref[tt:tt + LRU_HALO, :]

    xcb = xc.astype(BF16)
    ra, rx = [], []
    for n in range(N_BLOCKS_B):
        blk = xcb[:, n * BLOCK_B:(n + 1) * BLOCK_B]
        ra.append(jnp.dot(blk, wa_ref[n], preferred_element_type=F32))
        rx.append(jnp.dot(blk, wx_ref[n], preferred_element_type=F32))
    r = _sigmoid(jnp.concatenate(ra, axis=-1) + ba_ref[...])
    i = _sigmoid(jnp.concatenate(rx, axis=-1) + bx_ref[...])
    log_a = LRU_C * r * _log_sigmoid(lam_ref[...])
    a = jnp.exp(log_a)
    th = jnp.tanh(log_a)
    bt = jnp.sqrt(-2.0 * th / (1.0 - th)) * (i * xc)

    d = 1
    while d < tt:
        bt = bt + a * _shift_rows(bt, d, 0.0)
        a = a * _shift_rows(a, d, 1.0)
        d *= 2
    h = a * hc_ref[...] + bt
    h_ref[...] = h
    hc_ref[...] = h[tt - 1:tt, :]


def _rglru(u, lb0, h0, cw, cb, w_a, b_a, w_x, b_x, lam, *, tt):
    b, t, _ = u.shape
    row = lambda a: a.reshape(1, D_B)
    full = lambda a: pl.BlockSpec(a.shape, lambda bi, ti: (0,) * a.ndim)
    args = (cw, row(cb), w_a, row(b_a), w_x, row(b_x), row(lam))
    return pl.pallas_call(
        _rglru_kernel,
        grid=(b, t // tt),
        in_specs=[
            pl.BlockSpec((None, tt, D_B), lambda bi, ti: (bi, ti, 4 * D_A // D_B)),
            pl.BlockSpec((None, LRU_HALO, D_B), lambda bi, ti: (bi, 0, 0)),
            pl.BlockSpec((None, 1, D_B), lambda bi, ti: (bi, 0, 0)),
        ] + [full(a) for a in args],
        out_specs=pl.BlockSpec((None, tt, D_B), lambda bi, ti: (bi, ti, 0)),
        out_shape=jax.ShapeDtypeStruct((b, t, D_B), F32),
        scratch_shapes=[pltpu.VMEM((LRU_HALO + tt, D_B), F32), pltpu.VMEM((1, D_B), F32)],
        compiler_params=_cparams(("parallel", "arbitrary")),
        name="rglru",
    )(u, lb0, h0, *args)


def _gated_out_kernel(oa_ref, ga_ref, h_ref, gb_ref, x_ref, w_ref, o_ref):
    ga = ga_ref[...]
    gb = gb_ref[...]
    ma = (ga * _sigmoid(ga) * oa_ref[...]).astype(BF16)
    mb = (gb * _sigmoid(gb) * h_ref[...]).astype(BF16)
    acc = jnp.dot(ma, w_ref[0:D_A, :], preferred_element_type=F32)
    acc = acc + jnp.dot(mb, w_ref[D_A:, :], preferred_element_type=F32)
    o_ref[...] = x_ref[...] + acc


def _gated_out(o_a, u, h, x, w, *, tm):
    m = x.shape[0]
    return pl.pallas_call(
        _gated_out_kernel,
        grid=(m // tm,),
        in_specs=[
            pl.BlockSpec((tm, D_A), lambda i: (i, 0)),
            pl.BlockSpec((tm, D_A), lambda i: (i, 3)),
            pl.BlockSpec((tm, D_B), lambda i: (i, 0)),
            pl.BlockSpec((tm, D_B), lambda i: (i, (4 * D_A + D_B) // D_B)),
            pl.BlockSpec((tm, D_MODEL), lambda i: (i, 0)),
            pl.BlockSpec(w.shape, lambda i: (0, 0), pipeline_mode=pl.Buffered(1)),
        ],
        out_specs=pl.BlockSpec((tm, D_MODEL), lambda i: (i, 0)),
        out_shape=jax.ShapeDtypeStruct((m, D_MODEL), F32),
        compiler_params=_cparams(("parallel",)),
        name="gated_out_proj",
    )(o_a, u, h, u, x, w)


def _ln_out_norm_kernel(cv_ref, sg_ref, lg_ref, lb_ref, w_ref, x_ref, g_ref, o_ref):
    acc = cv_ref[...]
    mu = jnp.mean(acc, axis=-1, keepdims=True)
    cen = acc - mu
    var = jnp.mean(cen * cen, axis=-1, keepdims=True)
    yn = cen * lax.rsqrt(var + EPS) * lg_ref[...] + lb_ref[...]
    y = (yn * _sigmoid(yn) * sg_ref[...].astype(F32)).astype(BF16)
    x = x_ref[...] + jnp.dot(y, w_ref[...], preferred_element_type=F32)
    ms = jnp.mean(x * x, axis=-1, keepdims=True)
    o_ref[...] = x * lax.rsqrt(ms + EPS) * g_ref[...]


def _ln_out_norm(cv, sg, ln_g, ln_b, w, x, g, *, tm):
    m = x.shape[0]
    row = pl.BlockSpec((1, D_C), lambda i: (0, 0))
    return pl.pallas_call(
        _ln_out_norm_kernel,
        grid=(m // tm,),
        in_specs=[
            pl.BlockSpec((tm, D_C), lambda i: (i, 0)),
            pl.BlockSpec((tm, D_C), lambda i: (i, 0)),
            row,
            row,
            pl.BlockSpec(w.shape, lambda i: (0, 0), pipeline_mode=pl.Buffered(1)),
            pl.BlockSpec((tm, D_MODEL), lambda i: (i, 0)),
            pl.BlockSpec((1, D_MODEL), lambda i: (0, 0)),
        ],
        out_specs=pl.BlockSpec((tm, D_MODEL), lambda i: (i, 0)),
        out_shape=jax.ShapeDtypeStruct((m, D_MODEL), F32),
        compiler_params=_cparams(("parallel",)),
        name="ln_out_proj_final_norm",
    )(cv, sg, ln_g.reshape(1, D_C), ln_b.reshape(1, D_C), w, x, g)


def _front_pad_rows(a, rows):
    return jnp.pad(a, ((0, 0), (rows - a.shape[1], 0), (0, 0)))


def _trunk(x, caches, w, *, tm, tn_in, tm_out, lru_tt):
    b, t, _ = x.shape
    m = b * t
    x2d = x.reshape(m, D_MODEL)

    u = _norm_matmul(x2d, w["norm_ab"], w["w_in_ab"], tm=tm, tn=tn_in)
    u3 = u.reshape(b, t, IN_AB)
    if caches is None:
        assert t % ATT_BLOCK == 0 and ATT_BLOCK == BAND_PAST
        o_a, new_k, new_v = _attn_prompt(u3, w["rel_bias_vec"])
        h0 = jnp.zeros((b, 1, D_B), F32)
        lb0 = jnp.zeros((b, LRU_HALO, D_B), F32)
        cb0 = jnp.zeros((b, CONV_HALO, D_C), F32)
    else:
        kc, vc, hc, lbc, cbc = caches
        o_a, new_k, new_v = _attn_sample(u3, kc, vc, w["rel_bias_vec"])
        h0 = hc.reshape(b, 1, D_B)
        lb0 = _front_pad_rows(lbc, LRU_HALO)
        cb0 = _front_pad_rows(cbc, CONV_HALO)
    h = _rglru(u3, lb0, h0, w["lru_conv_w"], w["lru_conv_b"], w["lru_w_a"], w["lru_b_a"],
               w["lru_w_x"], w["lru_b_x"], w["lru_lambda"], tt=lru_tt)
    x1 = _gated_out(o_a.reshape(m, D_A), u, h.reshape(m, D_B), x2d, w["w_out_ab"], tm=tm_out)

    new_h = h[:, t - 1]
    new_lb = u3[:, t - (CONV_B - 1):, 4 * D_A:4 * D_A + D_B]

    cv, sg, z_tail = _norm_glu_conv(x1, w["norm_cv"], w["w_in_cv"], cb0, w["dw_w"], w["dw_b"], t=t, tm=tm, tn=512)
    out = _ln_out_norm(cv, sg, w["ln_g"], w["ln_b"], w["w_out_cv"], x1, w["final_norm"], tm=tm_out)
    segs_per_batch = z_tail.shape[0] // b
    new_cb = z_tail[segs_per_batch - 1::segs_per_batch, CONV_HALO - (CONV_C - 1):]

    return (out.reshape(b, t, D_MODEL), new_k[None], new_v[None], new_h[None], new_lb[None], new_cb[None])


def kernel(x_prompt, x_sample, cache_attn_k, cache_attn_v, state_lru_h, state_lru_conv, state_conv, norm_ab, w_in_ab, w_out_ab, rel_bias, lru_conv_w, lru_conv_b, lru_w_a, lru_b_a, lru_w_x, lru_b_x, lru_lambda, norm_cv, w_in_cv, w_out_cv, dw_w, dw_b, ln_g, ln_b, final_norm):
    assert norm_ab.shape[0] == 1 and norm_cv.shape[0] == 1, "one even and one odd layer"
    t_s = x_sample.shape[1]
    w = {
        "norm_ab": norm_ab[0].reshape(1, D_MODEL),
        "w_in_ab": w_in_ab[0].astype(BF16),
        "w_out_ab": w_out_ab[0].astype(BF16),
        "rel_bias_vec": _rel_bias_vector(rel_bias[0]),
        "lru_conv_w": lru_conv_w[0],
        "lru_conv_b": lru_conv_b[0],
        "lru_w_a": lru_w_a[0].astype(BF16),
        "lru_b_a": lru_b_a[0],
        "lru_w_x": lru_w_x[0].astype(BF16),
        "lru_b_x": lru_b_x[0],
        "lru_lambda": lru_lambda[0],
        "norm_cv": norm_cv[0].reshape(1, D_MODEL),
        "w_in_cv": w_in_cv[0].astype(BF16),
        "w_out_cv": w_out_cv[0].astype(BF16),
        "dw_w": dw_w[0],
        "dw_b": dw_b[0],
        "ln_g": ln_g[0],
        "ln_b": ln_b[0],
        "final_norm": final_norm.reshape(1, D_MODEL),
    }
    y_p, k_p, v_p, h_p, lb_p, cb_p = _trunk(x_prompt, None, w, tm=1024, tn_in=1536, tm_out=512, lru_tt=256)
    caches = (cache_attn_k[0], cache_attn_v[0], state_lru_h[0], state_lru_conv[0], state_conv[0])
    y_s, k_s, v_s, h_s, lb_s, cb_s = _trunk(x_sample, caches, w, tm=512, tn_in=1536, tm_out=512, lru_tt=t_s)
    return (y_p, y_s, k_p, v_p, h_p, lb_p, cb_p, k_s, v_s, h_s, lb_s, cb_s)
```

```python
import functools

import jax
import jax.numpy as jnp
from jax import lax
from jax.experimental import pallas as pl
from jax.experimental.pallas import tpu as pltpu

D_MODEL = 2048
CHUNK = 64
LEFT_CHUNKS = 8
BAND_PAST = CHUNK * LEFT_CHUNKS
N_HEADS = 8
HEAD_DIM = 128
D_A = N_HEADS * HEAD_DIM
REL_CLIP = 128
D_B = 1024
N_BLOCKS_B = 8
BLOCK_B = D_B // N_BLOCKS_B
CONV_B = 4
LRU_C = 8.0
D_C = D_MODEL
CONV_C = 31
IN_AB = 4 * D_A + 2 * D_B
IN_CV = 3 * D_C
EPS = 1e-6

F32 = jnp.float32
BF16 = jnp.bfloat16

SUBLANES = 8
LANES = 128
Q_TILE = 2 * CHUNK
K_WIN = BAND_PAST + Q_TILE
ATT_BLOCK = 512
CONV_HALO = 32
CONV_ROWS = 128
LRU_HALO = 8
VMEM_LIMIT = 56 * 1024 * 1024


def _cparams(sem):
    return pltpu.CompilerParams(dimension_semantics=sem, vmem_limit_bytes=VMEM_LIMIT)


def _sigmoid(x):
    return jax.nn.sigmoid(x)


def _norm_to_bf16(x, g):
    ms = jnp.mean(x * x, axis=-1, keepdims=True)
    return (x * lax.rsqrt(ms + EPS) * g).astype(BF16)


def _norm_matmul_kernel(x_ref, g_ref, w_ref, o_ref, xn_ref):
    @pl.when(pl.program_id(1) == 0)
    def _():
        xn_ref[...] = _norm_to_bf16(x_ref[...], g_ref[...])

    o_ref[...] = jnp.dot(xn_ref[...], w_ref[...], preferred_element_type=F32)


def _norm_matmul(x, g, w, *, tm, tn):
    m, d = x.shape
    n = w.shape[1]
    return pl.pallas_call(
        _norm_matmul_kernel,
        grid=(m // tm, n // tn),
        in_specs=[
            pl.BlockSpec((tm, d), lambda i, j: (i, 0)),
            pl.BlockSpec((1, d), lambda i, j: (0, 0)),
            pl.BlockSpec((d, tn), lambda i, j: (0, j)),
        ],
        out_specs=pl.BlockSpec((tm, tn), lambda i, j: (i, j)),
        out_shape=jax.ShapeDtypeStruct((m, n), F32),
        scratch_shapes=[pltpu.VMEM((tm, d), BF16)],
        compiler_params=_cparams(("parallel", "arbitrary")),
        name="norm_in_proj",
    )(x, g, w)


def _dwconv_rows(zext_ref, base, rows, lane0, w_ref, b_ref):
    lanes = slice(lane0, lane0 + LANES)
    acc = jnp.broadcast_to(b_ref[...], (rows, LANES))
    for phase in range(SUBLANES):
        win_rows = rows if phase == 0 else rows + SUBLANES
        part = None
        for k in range(CONV_C):
            off = base + CONV_HALO - (CONV_C - 1) + k
            if off % SUBLANES != phase:
                continue
            tap = w_ref[k]
            win = zext_ref[off - phase:off - phase + win_rows, lanes]
            term = win.reshape(win_rows // SUBLANES, SUBLANES, LANES) * tap[None]
            part = term if part is None else part + term
        if part is None:
            continue
        part = part.reshape(win_rows, LANES)
        if phase:
            part = pltpu.roll(part, win_rows - phase, 0)[:rows]
        acc = acc + part
    return acc


def _norm_glu_conv_kernel(x_ref, g_ref, wv_ref, wg_ref, wt_ref, st_ref, cw_ref, cb_ref,
                          cv_ref, sg_ref, tail_ref, xn_ref, zext_ref, carry_ref, *, seg, tiles_per_batch):
    i = pl.program_id(0)
    j = pl.program_id(1)
    n_seg = x_ref.shape[0] // seg
    ext = CONV_HALO + seg

    @pl.when(j == 0)
    def _():
        xn_ref[...] = _norm_to_bf16(x_ref[...], g_ref[...])

    xn = xn_ref[...]
    gate = jnp.dot(xn, wt_ref[...], preferred_element_type=F32)
    sg_ref[...] = (gate * _sigmoid(gate)).astype(BF16)
    val = jnp.dot(xn, wv_ref[...], preferred_element_type=F32)
    glu = jnp.dot(xn, wg_ref[...], preferred_element_type=F32)
    z = val * _sigmoid(glu)

    carried = tiles_per_batch > 1
    if carried:
        @pl.when(i % tiles_per_batch == 0)
        def _():
            carry_ref[j] = st_ref[0]

    rows = min(seg, CONV_ROWS)
    for s in range(n_seg):
        base = s * ext
        zext_ref[base:base + CONV_HALO, :] = carry_ref[j] if carried else st_ref[s]
        zext_ref[base + CONV_HALO:base + ext, :] = z[s * seg:(s + 1) * seg]
        for c in range(cw_ref.shape[0]):
            for r0 in range(0, seg, rows):
                out_rows = slice(s * seg + r0, s * seg + r0 + rows)
                cv_ref[out_rows, c * LANES:(c + 1) * LANES] = _dwconv_rows(
                    zext_ref, base + r0, rows, c * LANES, cw_ref.at[c], cb_ref.at[c])
        tail = zext_ref[base + seg:base + ext, :]
        tail_ref[s] = tail
        if carried:
            carry_ref[j] = tail


def _norm_glu_conv(x, g, w, state, dw_w, dw_b, *, t, tm, tn):
    m, d = x.shape
    nb = D_C // tn
    seg = min(t, tm)
    assert tm % seg == 0 and t % seg == 0 and seg % min(seg, CONV_ROWS) == 0 and seg % SUBLANES == 0
    tiles_per_batch = t // seg
    dw_w = dw_w.reshape(CONV_C, D_C // LANES, 1, LANES).transpose(1, 0, 2, 3)
    dw_w = jnp.broadcast_to(dw_w, (D_C // LANES, CONV_C, SUBLANES, LANES))
    dw_b = dw_b.reshape(D_C // LANES, 1, LANES)
    bpt = tm // seg if tiles_per_batch == 1 else 1
    batch_blk = (lambda i: i // tiles_per_batch) if tiles_per_batch > 1 else (lambda i: i)
    state_spec = pl.BlockSpec((bpt, CONV_HALO, tn), lambda i, j: (batch_blk(i), 0, j))
    body = functools.partial(_norm_glu_conv_kernel, seg=seg, tiles_per_batch=tiles_per_batch)
    return pl.pallas_call(
        body,
        grid=(m // tm, nb),
        in_specs=[
            pl.BlockSpec((tm, d), lambda i, j: (i, 0)),
            pl.BlockSpec((1, d), lambda i, j: (0, 0)),
            pl.BlockSpec((d, tn), lambda i, j: (0, j)),
            pl.BlockSpec((d, tn), lambda i, j: (0, j + nb)),
            pl.BlockSpec((d, tn), lambda i, j: (0, j + 2 * nb)),
            state_spec,
            pl.BlockSpec((tn // LANES, CONV_C, SUBLANES, LANES), lambda i, j: (j, 0, 0, 0)),
            pl.BlockSpec((tn // LANES, 1, LANES), lambda i, j: (j, 0, 0)),
        ],
        out_specs=[
            pl.BlockSpec((tm, tn), lambda i, j: (i, j)),
            pl.BlockSpec((tm, tn), lambda i, j: (i, j)),
            pl.BlockSpec((tm // seg, CONV_HALO, tn), lambda i, j: (i, 0, j)),
        ],
        out_shape=[
            jax.ShapeDtypeStruct((m, D_C), F32),
            jax.ShapeDtypeStruct((m, D_C), BF16),
            jax.ShapeDtypeStruct((m // seg, CONV_HALO, D_C), F32),
        ],
        scratch_shapes=[
            pltpu.VMEM((tm, d), BF16),
            pltpu.VMEM(((tm // seg) * (CONV_HALO + seg), tn), F32),
            pltpu.VMEM((nb, CONV_HALO, tn), F32),
        ],
        compiler_params=_cparams(("arbitrary", "arbitrary")),
        name="norm_in_proj_glu_conv",
    )(x, g, w, w, w, state, dw_w, dw_b)


def _attend_heads(q_ref, kcat_ref, vcat_ref, bias_ref, o_ref, q_rows, k_rows, kpos0):
    heads = [slice(h * HEAD_DIM, (h + 1) * HEAD_DIM) for h in range(N_HEADS)]
    scores = []
    for cols in heads:
        q = q_ref[q_rows, cols].astype(BF16)
        scores.append(lax.dot_general(q, kcat_ref[k_rows, cols], (((1,), (1,)), ((), ())),
                                      preferred_element_type=F32))
    probs = []
    for h, s in enumerate(scores):
        s = s * (HEAD_DIM ** -0.5) + bias_ref[h]
        if kpos0 is not None:
            col = lax.broadcasted_iota(jnp.int32, s.shape, 1)
            s = jnp.where(col + kpos0 >= 0, s, -jnp.inf)
        m = jnp.max(s, axis=-1, keepdims=True)
        p = jnp.exp(s - m)
        probs.append((p.astype(BF16), jnp.sum(p, axis=-1, keepdims=True)))
    for cols, (p, l) in zip(heads, probs):
        o_ref[q_rows, cols] = jnp.dot(p, vcat_ref[k_rows, cols], preferred_element_type=F32) / l


def _heads_to_rows(x_ref):
    heads = [x_ref[:, h * HEAD_DIM:(h + 1) * HEAD_DIM] for h in range(N_HEADS)]
    return jnp.swapaxes(jnp.stack(heads, axis=0), 0, 1)


def _expand_rel_bias(fvec_ref, bias_ref, n_valid, chunk_masks):
    tq = bias_ref.shape[1]
    i = lax.broadcasted_iota(jnp.int32, (tq, K_WIN), 0)
    j = lax.broadcasted_iota(jnp.int32, (tq, K_WIN), 1)
    if chunk_masks:
        lo = jnp.where(i < CHUNK, 0, CHUNK)
        hi = jnp.where(i < CHUNK, min(BAND_PAST + CHUNK, n_valid), n_valid)
        hidden = (j < lo) | (j >= hi)
    else:
        hidden = j >= n_valid
    for h in range(N_HEADS):
        rows = jnp.broadcast_to(fvec_ref[h:h + 1, :], (tq, Q_TILE + K_WIN))
        skew = pltpu.roll(rows, 0, 1, stride=1, stride_axis=0)
        bias_ref[h] = jnp.where(hidden, -jnp.inf, skew[:, Q_TILE:])


def _attn_prompt_kernel(q_ref, kp_ref, kc_ref, vp_ref, vc_ref, fvec_ref, o_ref, nk_ref, nv_ref,
                        kcat_ref, vcat_ref, bias_ref):
    b = pl.program_id(0)
    t = pl.program_id(1)

    @pl.when((b == 0) & (t == 0))
    def _():
        _expand_rel_bias(fvec_ref, bias_ref, K_WIN, True)

    @pl.when(t == pl.num_programs(1) - 1)
    def _():
        nk_ref[...] = _heads_to_rows(kc_ref)
        nv_ref[...] = _heads_to_rows(vc_ref)

    kcat_ref[0:ATT_BLOCK, :] = kp_ref[...].astype(BF16)
    kcat_ref[ATT_BLOCK:, :] = kc_ref[...].astype(BF16)
    vcat_ref[0:ATT_BLOCK, :] = vp_ref[...].astype(BF16)
    vcat_ref[ATT_BLOCK:, :] = vc_ref[...].astype(BF16)
    for r0 in range(0, ATT_BLOCK, Q_TILE):
        kpos0 = t * ATT_BLOCK + r0 - BAND_PAST
        _attend_heads(q_ref, kcat_ref, vcat_ref, bias_ref, o_ref,
                      slice(r0, r0 + Q_TILE), slice(r0, r0 + K_WIN), kpos0)


def _attn_prompt(u, fvec):
    b, s, _ = u.shape
    blk = (None, ATT_BLOCK, D_A)
    prev = lambda col: (lambda bi, t: (bi, jnp.maximum(t - 1, 0), col))
    cur = lambda col: (lambda bi, t: (bi, t, col))
    state = pl.BlockSpec((None, ATT_BLOCK, N_HEADS, HEAD_DIM), lambda bi, t: (bi, 0, 0, 0))
    state_shape = jax.ShapeDtypeStruct((b, ATT_BLOCK, N_HEADS, HEAD_DIM), F32)
    return pl.pallas_call(
        _attn_prompt_kernel,
        grid=(b, s // ATT_BLOCK),
        in_specs=[
            pl.BlockSpec(blk, cur(0)),
            pl.BlockSpec(blk, prev(1)),
            pl.BlockSpec(blk, cur(1)),
            pl.BlockSpec(blk, prev(2)),
            pl.BlockSpec(blk, cur(2)),
            pl.BlockSpec(fvec.shape, lambda bi, t: (0, 0)),
        ],
        out_specs=[pl.BlockSpec(blk, cur(0)), state, state],
        out_shape=[jax.ShapeDtypeStruct((b, s, D_A), F32), state_shape, state_shape],
        scratch_shapes=[
            pltpu.VMEM((2 * ATT_BLOCK, D_A), BF16),
            pltpu.VMEM((2 * ATT_BLOCK, D_A), BF16),
            pltpu.VMEM((N_HEADS, Q_TILE, K_WIN), F32),
        ],
        compiler_params=_cparams(("arbitrary", "arbitrary")),
        name="attn_prompt",
    )(u, u, u, u, u, fvec)


def _attn_sample_kernel(q_ref, kn_ref, vn_ref, kc_ref, vc_ref, fvec_ref, o_ref, nk_ref, nv_ref,
                        kcat_ref, vcat_ref, bias_ref):
    tq = q_ref.shape[0]
    lc = kc_ref.shape[0]

    @pl.when(pl.program_id(0) == 0)
    def _():
        _expand_rel_bias(fvec_ref, bias_ref, lc + tq, False)
        kcat_ref[lc + tq:, :] = jnp.zeros((K_WIN - lc - tq, D_A), BF16)
        vcat_ref[lc + tq:, :] = jnp.zeros((K_WIN - lc - tq, D_A), BF16)

    kcat_ref[lc:lc + tq, :] = kn_ref[...].astype(BF16)
    vcat_ref[lc:lc + tq, :] = vn_ref[...].astype(BF16)
    nk_ref[...] = _heads_to_rows(kn_ref)
    nv_ref[...] = _heads_to_rows(vn_ref)
    kch = jnp.swapaxes(kc_ref[...], 0, 1)
    vch = jnp.swapaxes(vc_ref[...], 0, 1)
    for h in range(N_HEADS):
        c0 = h * HEAD_DIM
        kcat_ref[0:lc, c0:c0 + HEAD_DIM] = kch[h].astype(BF16)
        vcat_ref[0:lc, c0:c0 + HEAD_DIM] = vch[h].astype(BF16)
    _attend_heads(q_ref, kcat_ref, vcat_ref, bias_ref, o_ref, slice(None), slice(None), None)


def _attn_sample(u, k_cache, v_cache, fvec):
    b, t, _ = u.shape
    lc = k_cache.shape[1]
    assert lc + t <= K_WIN and t <= Q_TILE
    new = lambda col: pl.BlockSpec((None, t, D_A), lambda bi: (bi, 0, col))
    cache = pl.BlockSpec((None, lc, N_HEADS, HEAD_DIM), lambda bi: (bi, 0, 0, 0))
    state = pl.BlockSpec((None, t, N_HEADS, HEAD_DIM), lambda bi: (bi, 0, 0, 0))
    state_shape = jax.ShapeDtypeStruct((b, t, N_HEADS, HEAD_DIM), F32)
    return pl.pallas_call(
        _attn_sample_kernel,
        grid=(b,),
        in_specs=[new(0), new(1), new(2), cache, cache, pl.BlockSpec(fvec.shape, lambda bi: (0, 0))],
        out_specs=[pl.BlockSpec((None, t, D_A), lambda bi: (bi, 0, 0)), state, state],
        out_shape=[jax.ShapeDtypeStruct((b, t, D_A), F32), state_shape, state_shape],
        scratch_shapes=[
            pltpu.VMEM((K_WIN, D_A), BF16),
            pltpu.VMEM((K_WIN, D_A), BF16),
            pltpu.VMEM((N_HEADS, t, K_WIN), F32),
        ],
        compiler_params=_cparams(("arbitrary",)),
        name="attn_sample",
    )(u, u, u, k_cache, v_cache, fvec)


def _rel_bias_vector(rel_table):
    width = Q_TILE + K_WIN
    n_const = BAND_PAST + Q_TILE - REL_CLIP + 1
    n_rev = width - n_const
    assert 0 < n_rev <= 2 * REL_CLIP
    const = jnp.broadcast_to(rel_table[:, 2 * REL_CLIP:], (N_HEADS, n_const))
    rev = lax.rev(rel_table[:, 2 * REL_CLIP - n_rev:2 * REL_CLIP], (1,))
    return jnp.concatenate([const, rev], axis=1).astype(F32)


def _log_sigmoid(x):
    return -(jnp.maximum(-x, 0.0) + jnp.log1p(jnp.exp(-jnp.abs(x))))


def _shift_rows(x, d, fill):
    n = x.shape[0]
    if d % SUBLANES == 0:
        return jnp.concatenate([jnp.full((d, x.shape[1]), fill, x.dtype), x[:n - d]], axis=0)
    rolled = pltpu.roll(x, d, 0)
    row = lax.broadcasted_iota(jnp.int32, x.shape, 0)
    return jnp.where(row < d, fill, rolled)


def _rglru_kernel(xb_ref, lb_ref, h0_ref, cw_ref, cb_ref, wa_ref, ba_ref, wx_ref, bx_ref, lam_ref,
                  h_ref, xext_ref, hc_ref):
    t = pl.program_id(1)
    tt = xb_ref.shape[0]

    @pl.when(t == 0)
    def _():
        xext_ref[0:LRU_HALO, :] = lb_ref[...]
        hc_ref[...] = h0_ref[...]

    xext_ref[LRU_HALO:, :] = xb_ref[...]
    xc = cb_ref[...] + cw_ref[CONV_B - 1:CONV_B, :] * xb_ref[...]
    win = xext_ref[...]
    for k in range(CONV_B - 1):
        off = LRU_HALO - (CONV_B - 1) + k
        xc = xc + pltpu.roll(cw_ref[k:k + 1, :] * win, tt + LRU_HALO - off, 0)[:tt]
    xext_ref[0:LRU_HALO, :] = xext_ref[tt:tt + LRU_HALO, :]

    xcb = xc.astype(BF16)
    ra, rx = [], []
    for n in range(N_BLOCKS_B):
        blk = xcb[:, n * BLOCK_B:(n + 1) * BLOCK_B]
        ra.append(jnp.dot(blk, wa_ref[n], preferred_element_type=F32))
        rx.append(jnp.dot(blk, wx_ref[n], preferred_element_type=F32))
    r = _sigmoid(jnp.concatenate(ra, axis=-1) + ba_ref[...])
    i = _sigmoid(jnp.concatenate(rx, axis=-1) + bx_ref[...])
    log_a = LRU_C * r * _log_sigmoid(lam_ref[...])
    a = jnp.exp(log_a)
    th = jnp.tanh(log_a)
    bt = jnp.sqrt(-2.0 * th / (1.0 - th)) * (i * xc)

    d = 1
    while d < tt:
        bt = bt + a * _shift_rows(bt, d, 0.0)
        a = a * _shift_rows(a, d, 1.0)
        d *= 2
    h = a * hc_ref[...] + bt
    h_ref[...] = h
    hc_ref[...] = h[tt - 1:tt, :]


def _rglru(u, lb0, h0, cw, cb, w_a, b_a, w_x, b_x, lam, *, tt):
    b, t, _ = u.shape
    row = lambda a: a.reshape(1, D_B)
    full = lambda a: pl.BlockSpec(a.shape, lambda bi, ti: (0,) * a.ndim)
    args = (cw, row(cb), w_a, row(b_a), w_x, row(b_x), row(lam))
    return pl.pallas_call(
        _rglru_kernel,
        grid=(b, t // tt),
        in_specs=[
            pl.BlockSpec((None, tt, D_B), lambda bi, ti: (bi, ti, 4 * D_A // D_B)),
            pl.BlockSpec((None, LRU_HALO, D_B), lambda bi, ti: (bi, 0, 0)),
            pl.BlockSpec((None, 1, D_B), lambda bi, ti: (bi, 0, 0)),
        ] + [full(a) for a in args],
        out_specs=pl.BlockSpec((None, tt, D_B), lambda bi, ti: (bi, ti, 0)),
        out_shape=jax.ShapeDtypeStruct((b, t, D_B), F32),
        scratch_shapes=[pltpu.VMEM((LRU_HALO + tt, D_B), F32), pltpu.VMEM((1, D_B), F32)],
        compiler_params=_cparams(("parallel", "arbitrary")),
        name="rglru",
    )(u, lb0, h0, *args)


def _gated_out_kernel(oa_ref, ga_ref, h_ref, gb_ref, x_ref, w_ref, o_ref):
    ga = ga_ref[...]
    gb = gb_ref[...]
    ma = (ga * _sigmoid(ga) * oa_ref[...]).astype(BF16)
    mb = (gb * _sigmoid(gb) * h_ref[...]).astype(BF16)
    acc = jnp.dot(ma, w_ref[0:D_A, :], preferred_element_type=F32)
    acc = acc + jnp.dot(mb, w_ref[D_A:, :], preferred_element_type=F32)
    o_ref[...] = x_ref[...] + acc


def _gated_out(o_a, u, h, x, w, *, tm):
    m = x.shape[0]
    return pl.pallas_call(
        _gated_out_kernel,
        grid=(m // tm,),
        in_specs=[
            pl.BlockSpec((tm, D_A), lambda i: (i, 0)),
            pl.BlockSpec((tm, D_A), lambda i: (i, 3)),
            pl.BlockSpec((tm, D_B), lambda i: (i, 0)),
            pl.BlockSpec((tm, D_B), lambda i: (i, (4 * D_A + D_B) // D_B)),
            pl.BlockSpec((tm, D_MODEL), lambda i: (i, 0)),
            pl.BlockSpec(w.shape, lambda i: (0, 0), pipeline_mode=pl.Buffered(1)),
        ],
        out_specs=pl.BlockSpec((tm, D_MODEL), lambda i: (i, 0)),
        out_shape=jax.ShapeDtypeStruct((m, D_MODEL), F32),
        compiler_params=_cparams(("parallel",)),
        name="gated_out_proj",
    )(o_a, u, h, u, x, w)


def _ln_out_norm_kernel(cv_ref, sg_ref, lg_ref, lb_ref, w_ref, x_ref, g_ref, o_ref):
    acc = cv_ref[...]
    mu = jnp.mean(acc, axis=-1, keepdims=True)
    cen = acc - mu
    var = jnp.mean(cen * cen, axis=-1, keepdims=True)
    yn = cen * lax.rsqrt(var + EPS) * lg_ref[...] + lb_ref[...]
    y = (yn * _sigmoid(yn) * sg_ref[...].astype(F32)).astype(BF16)
    x = x_ref[...] + jnp.dot(y, w_ref[...], preferred_element_type=F32)
    ms = jnp.mean(x * x, axis=-1, keepdims=True)
    o_ref[...] = x * lax.rsqrt(ms + EPS) * g_ref[...]


def _ln_out_norm(cv, sg, ln_g, ln_b, w, x, g, *, tm):
    m = x.shape[0]
    row = pl.BlockSpec((1, D_C), lambda i: (0, 0))
    return pl.pallas_call(
        _ln_out_norm_kernel,
        grid=(m // tm,),
        in_specs=[
            pl.BlockSpec((tm, D_C), lambda i: (i, 0)),
            pl.BlockSpec((tm, D_C), lambda i: (i, 0)),
            row,
            row,
            pl.BlockSpec(w.shape, lambda i: (0, 0), pipeline_mode=pl.Buffered(1)),
            pl.BlockSpec((tm, D_MODEL), lambda i: (i, 0)),
            pl.BlockSpec((1, D_MODEL), lambda i: (0, 0)),
        ],
        out_specs=pl.BlockSpec((tm, D_MODEL), lambda i: (i, 0)),
        out_shape=jax.ShapeDtypeStruct((m, D_MODEL), F32),
        compiler_params=_cparams(("parallel",)),
        name="ln_out_proj_final_norm",
    )(cv, sg, ln_g.reshape(1, D_C), ln_b.reshape(1, D_C), w, x, g)


def _front_pad_rows(a, rows):
    return jnp.pad(a, ((0, 0), (rows - a.shape[1], 0), (0, 0)))


def _trunk(x, caches, w, *, tm, tn_in, tm_out, lru_tt):
    b, t, _ = x.shape
    m = b * t
    x2d = x.reshape(m, D_MODEL)

    u = _norm_matmul(x2d, w["norm_ab"], w["w_in_ab"], tm=tm, tn=tn_in)
    u3 = u.reshape(b, t, IN_AB)
    if caches is None:
        assert t % ATT_BLOCK == 0 and ATT_BLOCK == BAND_PAST
        o_a, new_k, new_v = _attn_prompt(u3, w["rel_bias_vec"])
        h0 = jnp.zeros((b, 1, D_B), F32)
        lb0 = jnp.zeros((b, LRU_HALO, D_B), F32)
        cb0 = jnp.zeros((b, CONV_HALO, D_C), F32)
    else:
        kc, vc, hc, lbc, cbc = caches
        o_a, new_k, new_v = _attn_sample(u3, kc, vc, w["rel_bias_vec"])
        h0 = hc.reshape(b, 1, D_B)
        lb0 = _front_pad_rows(lbc, LRU_HALO)
        cb0 = _front_pad_rows(cbc, CONV_HALO)
    h = _rglru(u3, lb0, h0, w["lru_conv_w"], w["lru_conv_b"], w["lru_w_a"], w["lru_b_a"],
               w["lru_w_x"], w["lru_b_x"], w["lru_lambda"], tt=lru_tt)
    x1 = _gated_out(o_a.reshape(m, D_A), u, h.reshape(m, D_B), x2d, w["w_out_ab"], tm=tm_out)

    new_h = h[:, t - 1]
    new_lb = u3[:, t - (CONV_B - 1):, 4 * D_A:4 * D_A + D_B]

    cv, sg, z_tail = _norm_glu_conv(x1, w["norm_cv"], w["w_in_cv"], cb0, w["dw_w"], w["dw_b"], t=t, tm=tm, tn=512)
    out = _ln_out_norm(cv, sg, w["ln_g"], w["ln_b"], w["w_out_cv"], x1, w["final_norm"], tm=tm_out)
    segs_per_batch = z_tail.shape[0] // b
    new_cb = z_tail[segs_per_batch - 1::segs_per_batch, CONV_HALO - (CONV_C - 1):]

    return (out.reshape(b, t, D_MODEL), new_k[None], new_v[None], new_h[None], new_lb[None], new_cb[None])


def kernel(x_prompt, x_sample, cache_attn_k, cache_attn_v, state_lru_h, state_lru_conv, state_conv, norm_ab, w_in_ab, w_out_ab, rel_bias, lru_conv_w, lru_conv_b, lru_w_a, lru_b_a, lru_w_x, lru_b_x, lru_lambda, norm_cv, w_in_cv, w_out_cv, dw_w, dw_b, ln_g, ln_b, final_norm):
    assert norm_ab.shape[0] == 1 and norm_cv.shape[0] == 1, "one even and one odd layer"
    t_s = x_sample.shape[1]
    w = {
        "norm_ab": norm_ab[0].reshape(1, D_MODEL),
        "w_in_ab": w_in_ab[0].astype(BF16),
        "w_out_ab": w_out_ab[0].astype(BF16),
        "rel_bias_vec": _rel_bias_vector(rel_bias[0]),
        "lru_conv_w": lru_conv_w[0],
        "lru_conv_b": lru_conv_b[0],
        "lru_w_a": lru_w_a[0].astype(BF16),
        "lru_b_a": lru_b_a[0],
        "lru_w_x": lru_w_x[0].astype(BF16),
        "lru_b_x": lru_b_x[0],
        "lru_lambda": lru_lambda[0],
        "norm_cv": norm_cv[0].reshape(1, D_MODEL),
        "w_in_cv": w_in_cv[0].astype(BF16),
        "w_out_cv": w_out_cv[0].astype(BF16),
        "dw_w": dw_w[0],
        "dw_b": dw_b[0],
        "ln_g": ln_g[0],
        "ln_b": ln_b[0],
        "final_norm": final_norm.reshape(1, D_MODEL),
    }
    y_p, k_p, v_p, h_p, lb_p, cb_p = _trunk(x_prompt, None, w, tm=1024, tn_in=1536, tm_out=512, lru_tt=256)
    caches = (cache_attn_k[0], cache_attn_v[0], state_lru_h[0], state_lru_conv[0], state_conv[0])
    y_s, k_s, v_s, h_s, lb_s, cb_s = _trunk(x_sample, caches, w, tm=512, tn_in=1536, tm_out=512, lru_tt=t_s)
    return (y_p, y_s, k_p, v_p, h_p, lb_p, cb_p, k_s, v_s, h_s, lb_s, cb_s)
```

```python
import functools

import jax
import jax.numpy as jnp
from jax import lax
from jax.experimental import pallas as pl
from jax.experimental.pallas import tpu as pltpu

D_MODEL = 2048
CHUNK = 64
LEFT_CHUNKS = 8
BAND_PAST = CHUNK * LEFT_CHUNKS
N_HEADS = 8
HEAD_DIM = 128
D_A = N_HEADS * HEAD_DIM
REL_CLIP = 128
D_B = 1024
N_BLOCKS_B = 8
BLOCK_B = D_B // N_BLOCKS_B
CONV_B = 4
LRU_C = 8.0
D_C = D_MODEL
CONV_C = 31
IN_AB = 4 * D_A + 2 * D_B
IN_CV = 3 * D_C
EPS = 1e-6

F32 = jnp.float32
BF16 = jnp.bfloat16

SUBLANES = 8
LANES = 128
Q_TILE = 2 * CHUNK
K_WIN = BAND_PAST + Q_TILE
ATT_BLOCK = 512
CONV_HALO = 32
CONV_ROWS = 128
LRU_HALO = 8
VMEM_LIMIT = 56 * 1024 * 1024


def _cparams(sem):
    return pltpu.CompilerParams(dimension_semantics=sem, vmem_limit_bytes=VMEM_LIMIT)


def _sigmoid(x):
    return 0.5 * jnp.tanh(0.5 * x) + 0.5


def _norm_to_bf16(x, g):
    ms = jnp.mean(x * x, axis=-1, keepdims=True)
    return (x * lax.rsqrt(ms + EPS) * g).astype(BF16)


def _norm_matmul_kernel(x_ref, g_ref, w_ref, o_ref, xn_ref):
    @pl.when(pl.program_id(1) == 0)
    def _():
        xn_ref[...] = _norm_to_bf16(x_ref[...], g_ref[...])

    o_ref[...] = jnp.dot(xn_ref[...], w_ref[...], preferred_element_type=F32)


def _norm_matmul(x, g, w, *, tm, tn):
    m, d = x.shape
    n = w.shape[1]
    return pl.pallas_call(
        _norm_matmul_kernel,
        grid=(m // tm, n // tn),
        in_specs=[
            pl.BlockSpec((tm, d), lambda i, j: (i, 0)),
            pl.BlockSpec((1, d), lambda i, j: (0, 0)),
            pl.BlockSpec((d, tn), lambda i, j: (0, j)),
        ],
        out_specs=pl.BlockSpec((tm, tn), lambda i, j: (i, j)),
        out_shape=jax.ShapeDtypeStruct((m, n), F32),
        scratch_shapes=[pltpu.VMEM((tm, d), BF16)],
        compiler_params=_cparams(("parallel", "arbitrary")),
        name="norm_in_proj",
    )(x, g, w)


def _dwconv_rows(zext_ref, base, rows, lane0, w_ref, b_ref):
    lanes = slice(lane0, lane0 + LANES)
    acc = jnp.broadcast_to(b_ref[...], (rows, LANES))
    for phase in range(SUBLANES):
        win_rows = rows if phase == 0 else rows + SUBLANES
        part = None
        for k in range(CONV_C):
            off = base + CONV_HALO - (CONV_C - 1) + k
            if off % SUBLANES != phase:
                continue
            tap = w_ref[k]
            win = zext_ref[off - phase:off - phase + win_rows, lanes]
            term = win.reshape(win_rows // SUBLANES, SUBLANES, LANES) * tap[None]
            part = term if part is None else part + term
        if part is None:
            continue
        part = part.reshape(win_rows, LANES)
        if phase:
            part = pltpu.roll(part, win_rows - phase, 0)[:rows]
        acc = acc + part
    return acc


def _norm_glu_conv_kernel(x_ref, g_ref, wv_ref, wg_ref, wt_ref, st_ref, cw_ref, cb_ref,
                          cv_ref, sg_ref, tail_ref, xn_ref, zext_ref, carry_ref, *, seg, tiles_per_batch):
    i = pl.program_id(0)
    j = pl.program_id(1)
    n_seg = x_ref.shape[0] // seg
    ext = CONV_HALO + seg

    @pl.when(j == 0)
    def _():
        xn_ref[...] = _norm_to_bf16(x_ref[...], g_ref[...])

    xn = xn_ref[...]
    gate = jnp.dot(xn, wt_ref[...], preferred_element_type=F32)
    sg_ref[...] = (gate * _sigmoid(gate)).astype(BF16)
    val = jnp.dot(xn, wv_ref[...], preferred_element_type=F32)
    glu = jnp.dot(xn, wg_ref[...], preferred_element_type=F32)
    z = val * _sigmoid(glu)

    carried = tiles_per_batch > 1
    if carried:
        @pl.when(i % tiles_per_batch == 0)
        def _():
            carry_ref[j] = st_ref[0]

    rows = min(seg, CONV_ROWS)
    for s in range(n_seg):
        base = s * ext
        zext_ref[base:base + CONV_HALO, :] = carry_ref[j] if carried else st_ref[s]
        zext_ref[base + CONV_HALO:base + ext, :] = z[s * seg:(s + 1) * seg]
        for c in range(cw_ref.shape[0]):
            for r0 in range(0, seg, rows):
                out_rows = slice(s * seg + r0, s * seg + r0 + rows)
                cv_ref[out_rows, c * LANES:(c + 1) * LANES] = _dwconv_rows(
                    zext_ref, base + r0, rows, c * LANES, cw_ref.at[c], cb_ref.at[c])
        tail = zext_ref[base + seg:base + ext, :]
        tail_ref[s] = tail
        if carried:
            carry_ref[j] = tail


def _norm_glu_conv(x, g, w, state, dw_w, dw_b, *, t, tm, tn):
    m, d = x.shape
    nb = D_C // tn
    seg = min(t, tm)
    assert tm % seg == 0 and t % seg == 0 and seg % min(seg, CONV_ROWS) == 0 and seg % SUBLANES == 0
    tiles_per_batch = t // seg
    dw_w = dw_w.reshape(CONV_C, D_C // LANES, 1, LANES).transpose(1, 0, 2, 3)
    dw_w = jnp.broadcast_to(dw_w, (D_C // LANES, CONV_C, SUBLANES, LANES))
    dw_b = dw_b.reshape(D_C // LANES, 1, LANES)
    bpt = tm // seg if tiles_per_batch == 1 else 1
    batch_blk = (lambda i: i // tiles_per_batch) if tiles_per_batch > 1 else (lambda i: i)
    state_spec = pl.BlockSpec((bpt, CONV_HALO, tn), lambda i, j: (batch_blk(i), 0, j))
    body = functools.partial(_norm_glu_conv_kernel, seg=seg, tiles_per_batch=tiles_per_batch)
    return pl.pallas_call(
        body,
        grid=(m // tm, nb),
        in_specs=[
            pl.BlockSpec((tm, d), lambda i, j: (i, 0)),
            pl.BlockSpec((1, d), lambda i, j: (0, 0)),
            pl.BlockSpec((d, tn), lambda i, j: (0, j)),
            pl.BlockSpec((d, tn), lambda i, j: (0, j + nb)),
            pl.BlockSpec((d, tn), lambda i, j: (0, j + 2 * nb)),
            state_spec,
            pl.BlockSpec((tn // LANES, CONV_C, SUBLANES, LANES), lambda i, j: (j, 0, 0, 0)),
            pl.BlockSpec((tn // LANES, 1, LANES), lambda i, j: (j, 0, 0)),
        ],
        out_specs=[
            pl.BlockSpec((tm, tn), lambda i, j: (i, j)),
            pl.BlockSpec((tm, tn), lambda i, j: (i, j)),
            pl.BlockSpec((tm // seg, CONV_HALO, tn), lambda i, j: (i, 0, j)),
        ],
        out_shape=[
            jax.ShapeDtypeStruct((m, D_C), F32),
            jax.ShapeDtypeStruct((m, D_C), BF16),
            jax.ShapeDtypeStruct((m // seg, CONV_HALO, D_C), F32),
        ],
        scratch_shapes=[
            pltpu.VMEM((tm, d), BF16),
            pltpu.VMEM(((tm // seg) * (CONV_HALO + seg), tn), F32),
            pltpu.VMEM((nb, CONV_HALO, tn), F32),
        ],
        compiler_params=_cparams(("arbitrary", "arbitrary")),
        name="norm_in_proj_glu_conv",
    )(x, g, w, w, w, state, dw_w, dw_b)


def _attend_heads(q_ref, kcat_ref, vcat_ref, bias_ref, o_ref, q_rows, k_rows, kpos0):
    heads = [slice(h * HEAD_DIM, (h + 1) * HEAD_DIM) for h in range(N_HEADS)]
    scores = []
    for cols in heads:
        q = q_ref[q_rows, cols].astype(BF16)
        scores.append(lax.dot_general(q, kcat_ref[k_rows, cols], (((1,), (1,)), ((), ())),
                                      preferred_element_type=F32))
    if kpos0 is not None:
        col = lax.broadcasted_iota(jnp.int32, (1, scores[0].shape[1]), 1)
        before_start = jnp.where(col + kpos0 >= 0, 0.0, -jnp.inf)
    probs = []
    for h, s in enumerate(scores):
        s = s * (HEAD_DIM ** -0.5) + bias_ref[h]
        if kpos0 is not None:
            s = s + before_start
        m = jnp.max(s, axis=-1, keepdims=True)
        p = jnp.exp(s - m)
        probs.append((p.astype(BF16), jnp.sum(p, axis=-1, keepdims=True)))
    for cols, (p, l) in zip(heads, probs):
        o_ref[q_rows, cols] = jnp.dot(p, vcat_ref[k_rows, cols], preferred_element_type=F32) / l


def _heads_to_rows(x_ref):
    heads = [x_ref[:, h * HEAD_DIM:(h + 1) * HEAD_DIM] for h in range(N_HEADS)]
    return jnp.swapaxes(jnp.stack(heads, axis=0), 0, 1)


def _expand_rel_bias(fvec_ref, bias_ref, n_valid, chunk_masks):
    tq = bias_ref.shape[1]
    i = lax.broadcasted_iota(jnp.int32, (tq, K_WIN), 0)
    j = lax.broadcasted_iota(jnp.int32, (tq, K_WIN), 1)
    if chunk_masks:
        lo = jnp.where(i < CHUNK, 0, CHUNK)
        hi = jnp.where(i < CHUNK, min(BAND_PAST + CHUNK, n_valid), n_valid)
        hidden = (j < lo) | (j >= hi)
    else:
        hidden = j >= n_valid
    for h in range(N_HEADS):
        rows = jnp.broadcast_to(fvec_ref[h:h + 1, :], (tq, Q_TILE + K_WIN))
        skew = pltpu.roll(rows, 0, 1, stride=1, stride_axis=0)
        bias_ref[h] = jnp.where(hidden, -jnp.inf, skew[:, Q_TILE:])


def _attn_prompt_kernel(q_ref, kp_ref, kc_ref, vp_ref, vc_ref, fvec_ref, o_ref, nk_ref, nv_ref,
                        kcat_ref, vcat_ref, bias_ref):
    b = pl.program_id(0)
    t = pl.program_id(1)

    @pl.when((b == 0) & (t == 0))
    def _():
        _expand_rel_bias(fvec_ref, bias_ref, K_WIN, True)

    @pl.when(t == pl.num_programs(1) - 1)
    def _():
        nk_ref[...] = _heads_to_rows(kc_ref)
        nv_ref[...] = _heads_to_rows(vc_ref)

    kcat_ref[0:ATT_BLOCK, :] = kp_ref[...].astype(BF16)
    kcat_ref[ATT_BLOCK:, :] = kc_ref[...].astype(BF16)
    vcat_ref[0:ATT_BLOCK, :] = vp_ref[...].astype(BF16)
    vcat_ref[ATT_BLOCK:, :] = vc_ref[...].astype(BF16)
    for r0 in range(0, ATT_BLOCK, Q_TILE):
        kpos0 = t * ATT_BLOCK + r0 - BAND_PAST
        _attend_heads(q_ref, kcat_ref, vcat_ref, bias_ref, o_ref,
                      slice(r0, r0 + Q_TILE), slice(r0, r0 + K_WIN), kpos0)


def _attn_prompt(u, fvec):
    b, s, _ = u.shape
    blk = (None, ATT_BLOCK, D_A)
    prev = lambda col: (lambda bi, t: (bi, jnp.maximum(t - 1, 0), col))
    cur = lambda col: (lambda bi, t: (bi, t, col))
    state = pl.BlockSpec((None, ATT_BLOCK, N_HEADS, HEAD_DIM), lambda bi, t: (bi, 0, 0, 0))
    state_shape = jax.ShapeDtypeStruct((b, ATT_BLOCK, N_HEADS, HEAD_DIM), F32)
    return pl.pallas_call(
        _attn_prompt_kernel,
        grid=(b, s // ATT_BLOCK),
        in_specs=[
            pl.BlockSpec(blk, cur(0)),
            pl.BlockSpec(blk, prev(1)),
            pl.BlockSpec(blk, cur(1)),
            pl.BlockSpec(blk, prev(2)),
            pl.BlockSpec(blk, cur(2)),
            pl.BlockSpec(fvec.shape, lambda bi, t: (0, 0)),
        ],
        out_specs=[pl.BlockSpec(blk, cur(0)), state, state],
        out_shape=[jax.ShapeDtypeStruct((b, s, D_A), F32), state_shape, state_shape],
        scratch_shapes=[
            pltpu.VMEM((2 * ATT_BLOCK, D_A), BF16),
            pltpu.VMEM((2 * ATT_BLOCK, D_A), BF16),
            pltpu.VMEM((N_HEADS, Q_TILE, K_WIN), F32),
        ],
        compiler_params=_cparams(("arbitrary", "arbitrary")),
        name="attn_prompt",
    )(u, u, u, u, u, fvec)


def _attn_sample_kernel(q_ref, kn_ref, vn_ref, kc_ref, vc_ref, fvec_ref, o_ref, nk_ref, nv_ref,
                        kcat_ref, vcat_ref, bias_ref):
    tq = q_ref.shape[0]
    lc = kc_ref.shape[0]

    @pl.when(pl.program_id(0) == 0)
    def _():
        _expand_rel_bias(fvec_ref, bias_ref, lc + tq, False)
        kcat_ref[lc + tq:, :] = jnp.zeros((K_WIN - lc - tq, D_A), BF16)
        vcat_ref[lc + tq:, :] = jnp.zeros((K_WIN - lc - tq, D_A), BF16)

    kcat_ref[lc:lc + tq, :] = kn_ref[...].astype(BF16)
    vcat_ref[lc:lc + tq, :] = vn_ref[...].astype(BF16)
    nk_ref[...] = _heads_to_rows(kn_ref)
    nv_ref[...] = _heads_to_rows(vn_ref)
    kch = jnp.swapaxes(kc_ref[...], 0, 1)
    vch = jnp.swapaxes(vc_ref[...], 0, 1)
    for h in range(N_HEADS):
        c0 = h * HEAD_DIM
        kcat_ref[0:lc, c0:c0 + HEAD_DIM] = kch[h].astype(BF16)
        vcat_ref[0:lc, c0:c0 + HEAD_DIM] = vch[h].astype(BF16)
    _attend_heads(q_ref, kcat_ref, vcat_ref, bias_ref, o_ref, slice(None), slice(None), None)


def _attn_sample(u, k_cache, v_cache, fvec):
    b, t, _ = u.shape
    lc = k_cache.shape[1]
    assert lc + t <= K_WIN and t <= Q_TILE
    new = lambda col: pl.BlockSpec((None, t, D_A), lambda bi: (bi, 0, col))
    cache = pl.BlockSpec((None, lc, N_HEADS, HEAD_DIM), lambda bi: (bi, 0, 0, 0))
    state = pl.BlockSpec((None, t, N_HEADS, HEAD_DIM), lambda bi: (bi, 0, 0, 0))
    state_shape = jax.ShapeDtypeStruct((b, t, N_HEADS, HEAD_DIM), F32)
    return pl.pallas_call(
        _attn_sample_kernel,
        grid=(b,),
        in_specs=[new(0), new(1), new(2), cache, cache, pl.BlockSpec(fvec.shape, lambda bi: (0, 0))],
        out_specs=[pl.BlockSpec((None, t, D_A), lambda bi: (bi, 0, 0)), state, state],
        out_shape=[jax.ShapeDtypeStruct((b, t, D_A), F32), state_shape, state_shape],
        scratch_shapes=[
            pltpu.VMEM((K_WIN, D_A), BF16),
            pltpu.VMEM((K_WIN, D_A), BF16),
            pltpu.VMEM((N_HEADS, t, K_WIN), F32),
        ],
        compiler_params=_cparams(("arbitrary",)),
        name="attn_sample",
    )(u, u, u, k_cache, v_cache, fvec)


def _rel_bias_vector(rel_table):
    width = Q_TILE + K_WIN
    n_const = BAND_PAST + Q_TILE - REL_CLIP + 1
    n_rev = width - n_const
    assert 0 < n_rev <= 2 * REL_CLIP
    const = jnp.broadcast_to(rel_table[:, 2 * REL_CLIP:], (N_HEADS, n_const))
    rev = lax.rev(rel_table[:, 2 * REL_CLIP - n_rev:2 * REL_CLIP], (1,))
    return jnp.concatenate([const, rev], axis=1).astype(F32)


def _log_sigmoid(x):
    return -(jnp.maximum(-x, 0.0) + jnp.log1p(jnp.exp(-jnp.abs(x))))


def _shift_rows(x, d, fill):
    n = x.shape[0]
    if d % SUBLANES == 0:
        return jnp.concatenate([jnp.full((d, x.shape[1]), fill, x.dtype), x[:n - d]], axis=0)
    rolled = pltpu.roll(x, d, 0)
    row = lax.broadcasted_iota(jnp.int32, x.shape, 0)
    return jnp.where(row < d, fill, rolled)


def _rglru_kernel(xb_ref, lb_ref, h0_ref, cw_ref, cb_ref, wa_ref, ba_ref, wx_ref, bx_ref, lam_ref,
                  h_ref, xext_ref, hc_ref):
    t = pl.program_id(1)
    tt = xb_ref.shape[0]

    @pl.when(t == 0)
    def _():
        xext_ref[0:LRU_HALO, :] = lb_ref[...]
        hc_ref[...] = h0_ref[...]

    xext_ref[LRU_HALO:, :] = xb_ref[...]
    xc = cb_ref[...] + cw_ref[CONV_B - 1:CONV_B, :] * xb_ref[...]
    win = xext_ref[...]
    for k in range(CONV_B - 1):
        off = LRU_HALO - (CONV_B - 1) + k
        xc = xc + pltpu.roll(cw_ref[k:k + 1, :] * win, tt + LRU_HALO - off, 0)[:tt]
    xext_ref[0:LRU_HALO, :] = xext_ref[tt:tt + LRU_HALO, :]

    xcb = xc.astype(BF16)
    ra, rx = [], []
    for n in range(N_BLOCKS_B):
        blk = xcb[:, n * BLOCK_B:(n + 1) * BLOCK_B]
        ra.append(jnp.dot(blk, wa_ref[n], preferred_element_type=F32))
        rx.append(jnp.dot(blk, wx_ref[n], preferred_element_type=F32))
    r = _sigmoid(jnp.concatenate(ra, axis=-1) + ba_ref[...])
    i = _sigmoid(jnp.concatenate(rx, axis=-1) + bx_ref[...])
    log_a = LRU_C * r * _log_sigmoid(lam_ref[...])
    a = jnp.exp(log_a)
    th = jnp.tanh(log_a)
    bt = jnp.sqrt(-2.0 * th / (1.0 - th)) * (i * xc)

    d = 1
    while d < tt:
        bt = bt + a * _shift_rows(bt, d, 0.0)
        a = a * _shift_rows(a, d, 1.0)
        d *= 2
    h = a * hc_ref[...] + bt
    h_ref[...] = h
    hc_ref[...] = h[tt - 1:tt, :]


def _rglru(u, lb0, h0, cw, cb, w_a, b_a, w_x, b_x, lam, *, tt):
    b, t, _ = u.shape
    row = lambda a: a.reshape(1, D_B)
    full = lambda a: pl.BlockSpec(a.shape, lambda bi, ti: (0,) * a.ndim)
    args = (cw, row(cb), w_a, row(b_a), w_x, row(b_x), row(lam))
    return pl.pallas_call(
        _rglru_kernel,
        grid=(b, t // tt),
        in_specs=[
            pl.BlockSpec((None, tt, D_B), lambda bi, ti: (bi, ti, 4 * D_A // D_B)),
            pl.BlockSpec((None, LRU_HALO, D_B), lambda bi, ti: (bi, 0, 0)),
            pl.BlockSpec((None, 1, D_B), lambda bi, ti: (bi, 0, 0)),
        ] + [full(a) for a in args],
        out_specs=pl.BlockSpec((None, tt, D_B), lambda bi, ti: (bi, ti, 0)),
        out_shape=jax.ShapeDtypeStruct((b, t, D_B), F32),
        scratch_shapes=[pltpu.VMEM((LRU_HALO + tt, D_B), F32), pltpu.VMEM((1, D_B), F32)],
        compiler_params=_cparams(("parallel", "arbitrary")),
        name="rglru",
    )(u, lb0, h0, *args)


def _gated_out_kernel(oa_ref, ga_ref, h_ref, gb_ref, x_ref, w_ref, o_ref):
    ga = ga_ref[...]
    gb = gb_ref[...]
    ma = (ga * _sigmoid(ga) * oa_ref[...]).astype(BF16)
    mb = (gb * _sigmoid(gb) * h_ref[...]).astype(BF16)
    acc = jnp.dot(ma, w_ref[0:D_A, :], preferred_element_type=F32)
    acc = acc + jnp.dot(mb, w_ref[D_A:, :], preferred_element_type=F32)
    o_ref[...] = x_ref[...] + acc


def _gated_out(o_a, u, h, x, w, *, tm):
    m = x.shape[0]
    return pl.pallas_call(
        _gated_out_kernel,
        grid=(m // tm,),
        in_specs=[
            pl.BlockSpec((tm, D_A), lambda i: (i, 0)),
            pl.BlockSpec((tm, D_A), lambda i: (i, 3)),
            pl.BlockSpec((tm, D_B), lambda i: (i, 0)),
            pl.BlockSpec((tm, D_B), lambda i: (i, (4 * D_A + D_B) // D_B)),
            pl.BlockSpec((tm, D_MODEL), lambda i: (i, 0)),
            pl.BlockSpec(w.shape, lambda i: (0, 0), pipeline_mode=pl.Buffered(1)),
        ],
        out_specs=pl.BlockSpec((tm, D_MODEL), lambda i: (i, 0)),
        out_shape=jax.ShapeDtypeStruct((m, D_MODEL), F32),
        compiler_params=_cparams(("parallel",)),
        name="gated_out_proj",
    )(o_a, u, h, u, x, w)


def _ln_out_norm_kernel(cv_ref, sg_ref, lg_ref, lb_ref, w_ref, x_ref, g_ref, o_ref):
    acc = cv_ref[...]
    mu = jnp.mean(acc, axis=-1, keepdims=True)
    cen = acc - mu
    var = jnp.mean(cen * cen, axis=-1, keepdims=True)
    yn = cen * lax.rsqrt(var + EPS) * lg_ref[...] + lb_ref[...]
    y = (yn * _sigmoid(yn) * sg_ref[...].astype(F32)).astype(BF16)
    x = x_ref[...] + jnp.dot(y, w_ref[...], preferred_element_type=F32)
    ms = jnp.mean(x * x, axis=-1, keepdims=True)
    o_ref[...] = x * lax.rsqrt(ms + EPS) * g_ref[...]


def _ln_out_norm(cv, sg, ln_g, ln_b, w, x, g, *, tm):
    m = x.shape[0]
    row = pl.BlockSpec((1, D_C), lambda i: (0, 0))
    return pl.pallas_call(
        _ln_out_norm_kernel,
        grid=(m // tm,),
        in_specs=[
            pl.BlockSpec((tm, D_C), lambda i: (i, 0)),
            pl.BlockSpec((tm, D_C), lambda i: (i, 0)),
            row,
            row,
            pl.BlockSpec(w.shape, lambda i: (0, 0), pipeline_mode=pl.Buffered(1)),
            pl.BlockSpec((tm, D_MODEL), lambda i: (i, 0)),
            pl.BlockSpec((1, D_MODEL), lambda i: (0, 0)),
        ],
        out_specs=pl.BlockSpec((tm, D_MODEL), lambda i: (i, 0)),
        out_shape=jax.ShapeDtypeStruct((m, D_MODEL), F32),
        compiler_params=_cparams(("parallel",)),
        name="ln_out_proj_final_norm",
    )(cv, sg, ln_g.reshape(1, D_C), ln_b.reshape(1, D_C), w, x, g)


def _front_pad_rows(a, rows):
    return jnp.pad(a, ((0, 0), (rows - a.shape[1], 0), (0, 0)))


def _trunk(x, caches, w, *, tm, tn_in, tm_out, lru_tt):
    b, t, _ = x.shape
    m = b * t
    x2d = x.reshape(m, D_MODEL)

    u = _norm_matmul(x2d, w["norm_ab"], w["w_in_ab"], tm=tm, tn=tn_in)
    u3 = u.reshape(b, t, IN_AB)
    if caches is None:
        assert t % ATT_BLOCK == 0 and ATT_BLOCK == BAND_PAST
        o_a, new_k, new_v = _attn_prompt(u3, w["rel_bias_vec"])
        h0 = jnp.zeros((b, 1, D_B), F32)
        lb0 = jnp.zeros((b, LRU_HALO, D_B), F32)
        cb0 = jnp.zeros((b, CONV_HALO, D_C), F32)
    else:
        kc, vc, hc, lbc, cbc = caches
        o_a, new_k, new_v = _attn_sample(u3, kc, vc, w["rel_bias_vec"])
        h0 = hc.reshape(b, 1, D_B)
        lb0 = _front_pad_rows(lbc, LRU_HALO)
        cb0 = _front_pad_rows(cbc, CONV_HALO)
    h = _rglru(u3, lb0, h0, w["lru_conv_w"], w["lru_conv_b"], w["lru_w_a"], w["lru_b_a"],
               w["lru_w_x"], w["lru_b_x"], w["lru_lambda"], tt=lru_tt)
    x1 = _gated_out(o_a.reshape(m, D_A), u, h.reshape(m, D_B), x2d, w["w_out_ab"], tm=tm_out)

    new_h = h[:, t - 1]
    new_lb = u3[:, t - (CONV_B - 1):, 4 * D_A:4 * D_A + D_B]

    cv, sg, z_tail = _norm_glu_conv(x1, w["norm_cv"], w["w_in_cv"], cb0, w["dw_w"], w["dw_b"], t=t, tm=tm, tn=512)
    out = _ln_out_norm(cv, sg, w["ln_g"], w["ln_b"], w["w_out_cv"], x1, w["final_norm"], tm=tm_out)
    segs_per_batch = z_tail.shape[0] // b
    new_cb = z_tail[segs_per_batch - 1::segs_per_batch, CONV_HALO - (CONV_C - 1):]

    return (out.reshape(b, t, D_MODEL), new_k[None], new_v[None], new_h[None], new_lb[None], new_cb[None])


def kernel(x_prompt, x_sample, cache_attn_k, cache_attn_v, state_lru_h, state_lru_conv, state_conv, norm_ab, w_in_ab, w_out_ab, rel_bias, lru_conv_w, lru_conv_b, lru_w_a, lru_b_a, lru_w_x, lru_b_x, lru_lambda, norm_cv, w_in_cv, w_out_cv, dw_w, dw_b, ln_g, ln_b, final_norm):
    assert norm_ab.shape[0] == 1 and norm_cv.shape[0] == 1, "one even and one odd layer"
    t_s = x_sample.shape[1]
    w = {
        "norm_ab": norm_ab[0].reshape(1, D_MODEL),
        "w_in_ab": w_in_ab[0].astype(BF16),
        "w_out_ab": w_out_ab[0].astype(BF16),
        "rel_bias_vec": _rel_bias_vector(rel_bias[0]),
        "lru_conv_w": lru_conv_w[0],
        "lru_conv_b": lru_conv_b[0],
        "lru_w_a": lru_w_a[0].astype(BF16),
        "lru_b_a": lru_b_a[0],
        "lru_w_x": lru_w_x[0].astype(BF16),
        "lru_b_x": lru_b_x[0],
        "lru_lambda": lru_lambda[0],
        "norm_cv": norm_cv[0].reshape(1, D_MODEL),
        "w_in_cv": w_in_cv[0].astype(BF16),
        "w_out_cv": w_out_cv[0].astype(BF16),
        "dw_w": dw_w[0],
        "dw_b": dw_b[0],
        "ln_g": ln_g[0],
        "ln_b": ln_b[0],
        "final_norm": final_norm.reshape(1, D_MODEL),
    }
    y_p, k_p, v_p, h_p, lb_p, cb_p = _trunk(x_prompt, None, w, tm=1024, tn_in=1536, tm_out=512, lru_tt=256)
    caches = (cache_attn_k[0], cache_attn_v[0], state_lru_h[0], state_lru_conv[0], state_conv[0])
    y_s, k_s, v_s, h_s, lb_s, cb_s = _trunk(x_sample, caches, w, tm=512, tn_in=1536, tm_out=512, lru_tt=t_s)
    return (y_p, y_s, k_p, v_p, h_p, lb_p, cb_p, k_s, v_s, h_s, lb_s, cb_s)
```

```python
import functools

import jax
import jax.numpy as jnp
from jax import lax
from jax.experimental import pallas as pl
from jax.experimental.pallas import tpu as pltpu

D_MODEL = 2048
CHUNK = 64
LEFT_CHUNKS = 8
BAND_PAST = CHUNK * LEFT_CHUNKS
N_HEADS = 8
HEAD_DIM = 128
D_A = N_HEADS * HEAD_DIM
REL_CLIP = 128
D_B = 1024
N_BLOCKS_B = 8
BLOCK_B = D_B // N_BLOCKS_B
CONV_B = 4
LRU_C = 8.0
D_C = D_MODEL
CONV_C = 31
IN_AB = 4 * D_A + 2 * D_B
IN_CV = 3 * D_C
EPS = 1e-6

F32 = jnp.float32
BF16 = jnp.bfloat16

SUBLANES = 8
LANES = 128
Q_TILE = 2 * CHUNK
K_WIN = BAND_PAST + Q_TILE
ATT_BLOCK = 512
SAMPLE_STREAMS_PER_STEP = 2
CONV_HALO = 32
CONV_ROWS = 128
LRU_SCAN_ROWS = 32
LRU_HALO = 8
VMEM_LIMIT = 56 * 1024 * 1024


def _cparams(sem):
    return pltpu.CompilerParams(dimension_semantics=sem, vmem_limit_bytes=VMEM_LIMIT)


def _sigmoid(x):
    return 0.5 * jnp.tanh(0.5 * x) + 0.5


def _silu(x):
    half = 0.5 * x
    return half + half * jnp.tanh(half)


def _norm_to_bf16(x, g):
    ms = jnp.mean(x * x, axis=-1, keepdims=True)
    return (x * lax.rsqrt(ms + EPS) * g).astype(BF16)


def _norm_matmul_kernel(x_ref, g_ref, w_ref, o_ref, xn_ref):
    @pl.when(pl.program_id(1) == 0)
    def _():
        xn_ref[...] = _norm_to_bf16(x_ref[...], g_ref[...])

    o_ref[...] = jnp.dot(xn_ref[...], w_ref[...], preferred_element_type=F32)


def _norm_matmul(x, g, w, *, tm, tn):
    m, d = x.shape
    n = w.shape[1]
    return pl.pallas_call(
        _norm_matmul_kernel,
        grid=(m // tm, n // tn),
        in_specs=[
            pl.BlockSpec((tm, d), lambda i, j: (i, 0)),
            pl.BlockSpec((1, d), lambda i, j: (0, 0)),
            pl.BlockSpec((d, tn), lambda i, j: (0, j)),
        ],
        out_specs=pl.BlockSpec((tm, tn), lambda i, j: (i, j)),
        out_shape=jax.ShapeDtypeStruct((m, n), F32),
        scratch_shapes=[pltpu.VMEM((tm, d), BF16)],
        compiler_params=_cparams(("parallel", "arbitrary")),
        name="norm_in_proj",
    )(x, g, w)


def _dwconv_rows(zext_ref, base, rows, lane0, w_ref, b_ref):
    lanes = slice(lane0, lane0 + LANES)
    acc = jnp.broadcast_to(b_ref[...], (rows, LANES))
    for phase in range(SUBLANES):
        win_rows = rows if phase == 0 else rows + SUBLANES
        part = None
        for k in range(CONV_C):
            off = base + CONV_HALO - (CONV_C - 1) + k
            if off % SUBLANES != phase:
                continue
            tap = w_ref[k]
            win = zext_ref[off - phase:off - phase + win_rows, lanes]
            term = win.reshape(win_rows // SUBLANES, SUBLANES, LANES) * tap[None]
            part = term if part is None else part + term
        if part is None:
            continue
        part = part.reshape(win_rows, LANES)
        if phase:
            part = pltpu.roll(part, win_rows - phase, 0)[:rows]
        acc = acc + part
    return acc


def _norm_glu_conv_kernel(x_ref, g_ref, wv_ref, wg_ref, wt_ref, st_ref, cw_ref, cb_ref,
                          cv_ref, sg_ref, tail_ref, xn_ref, zext_ref, carry_ref, *, seg, tiles_per_batch):
    i = pl.program_id(0)
    j = pl.program_id(1)
    n_seg = x_ref.shape[0] // seg
    ext = CONV_HALO + seg

    @pl.when(j == 0)
    def _():
        xn_ref[...] = _norm_to_bf16(x_ref[...], g_ref[...])

    xn = xn_ref[...]
    gate = jnp.dot(xn, wt_ref[...], preferred_element_type=F32)
    sg_ref[...] = _silu(gate).astype(BF16)
    val = jnp.dot(xn, wv_ref[...], preferred_element_type=F32)
    glu = jnp.dot(xn, wg_ref[...], preferred_element_type=F32)
    z = val * _sigmoid(glu)

    carried = tiles_per_batch > 1
    if carried:
        @pl.when(i % tiles_per_batch == 0)
        def _():
            carry_ref[j] = st_ref[0]

    rows = min(seg, CONV_ROWS)
    for s in range(n_seg):
        base = s * ext
        zext_ref[base:base + CONV_HALO, :] = carry_ref[j] if carried else st_ref[s]
        zext_ref[base + CONV_HALO:base + ext, :] = z[s * seg:(s + 1) * seg]
        for c in range(cw_ref.shape[0]):
            for r0 in range(0, seg, rows):
                out_rows = slice(s * seg + r0, s * seg + r0 + rows)
                cv_ref[out_rows, c * LANES:(c + 1) * LANES] = _dwconv_rows(
                    zext_ref, base + r0, rows, c * LANES, cw_ref.at[c], cb_ref.at[c])
        tail = zext_ref[base + seg:base + ext, :]
        tail_ref[s] = tail
        if carried:
            carry_ref[j] = tail


def _norm_glu_conv(x, g, w, state, dw_w, dw_b, *, t, tm, tn):
    m, d = x.shape
    nb = D_C // tn
    seg = min(t, tm)
    assert tm % seg == 0 and t % seg == 0 and seg % min(seg, CONV_ROWS) == 0 and seg % SUBLANES == 0
    tiles_per_batch = t // seg
    dw_w = dw_w.reshape(CONV_C, D_C // LANES, 1, LANES).transpose(1, 0, 2, 3)
    dw_w = jnp.broadcast_to(dw_w, (D_C // LANES, CONV_C, SUBLANES, LANES))
    dw_b = dw_b.reshape(D_C // LANES, 1, LANES)
    bpt = tm // seg if tiles_per_batch == 1 else 1
    batch_blk = (lambda i: i // tiles_per_batch) if tiles_per_batch > 1 else (lambda i: i)
    state_spec = pl.BlockSpec((bpt, CONV_HALO, tn), lambda i, j: (batch_blk(i), 0, j))
    body = functools.partial(_norm_glu_conv_kernel, seg=seg, tiles_per_batch=tiles_per_batch)
    return pl.pallas_call(
        body,
        grid=(m // tm, nb),
        in_specs=[
            pl.BlockSpec((tm, d), lambda i, j: (i, 0)),
            pl.BlockSpec((1, d), lambda i, j: (0, 0)),
            pl.BlockSpec((d, tn), lambda i, j: (0, j)),
            pl.BlockSpec((d, tn), lambda i, j: (0, j + nb)),
            pl.BlockSpec((d, tn), lambda i, j: (0, j + 2 * nb)),
            state_spec,
            pl.BlockSpec((tn // LANES, CONV_C, SUBLANES, LANES), lambda i, j: (j, 0, 0, 0)),
            pl.BlockSpec((tn // LANES, 1, LANES), lambda i, j: (j, 0, 0)),
        ],
        out_specs=[
            pl.BlockSpec((tm, tn), lambda i, j: (i, j)),
            pl.BlockSpec((tm, tn), lambda i, j: (i, j)),
            pl.BlockSpec((tm // seg, CONV_HALO, tn), lambda i, j: (i, 0, j)),
        ],
        out_shape=[
            jax.ShapeDtypeStruct((m, D_C), F32),
            jax.ShapeDtypeStruct((m, D_C), BF16),
            jax.ShapeDtypeStruct((m // seg, CONV_HALO, D_C), F32),
        ],
        scratch_shapes=[
            pltpu.VMEM((tm, d), BF16),
            pltpu.VMEM(((tm // seg) * (CONV_HALO + seg), tn), F32),
            pltpu.VMEM((nb, CONV_HALO, tn), F32),
        ],
        compiler_params=_cparams(("arbitrary", "arbitrary")),
        name="norm_in_proj_glu_conv",
    )(x, g, w, w, w, state, dw_w, dw_b)


def _attend_heads(q_ref, kcat_ref, vcat_ref, bias_ref, o_ref, q_rows, k_rows, kpos0):
    heads = [slice(h * HEAD_DIM, (h + 1) * HEAD_DIM) for h in range(N_HEADS)]
    scores = []
    for cols in heads:
        q = q_ref[q_rows, cols].astype(BF16)
        scores.append(lax.dot_general(q, kcat_ref[k_rows, cols], (((1,), (1,)), ((), ())),
                                      preferred_element_type=F32))
    if kpos0 is not None:
        col = lax.broadcasted_iota(jnp.int32, (1, scores[0].shape[1]), 1)
        before_start = jnp.where(col + kpos0 >= 0, 0.0, -jnp.inf)
    probs = []
    for h, s in enumerate(scores):
        s = s * (HEAD_DIM ** -0.5) + bias_ref[h]
        if kpos0 is not None:
            s = s + before_start
        m = jnp.max(s, axis=-1, keepdims=True)
        p = jnp.exp(s - m)
        probs.append((p.astype(BF16), jnp.sum(p, axis=-1, keepdims=True)))
    for cols, (p, l) in zip(heads, probs):
        o_ref[q_rows, cols] = jnp.dot(p, vcat_ref[k_rows, cols], preferred_element_type=F32) / l


def _heads_to_rows(x_ref):
    heads = [x_ref[:, h * HEAD_DIM:(h + 1) * HEAD_DIM] for h in range(N_HEADS)]
    return jnp.swapaxes(jnp.stack(heads, axis=0), 0, 1)


def _expand_rel_bias(fvec_ref, bias_ref, n_valid, chunk_masks):
    tq = bias_ref.shape[1]
    i = lax.broadcasted_iota(jnp.int32, (tq, K_WIN), 0)
    j = lax.broadcasted_iota(jnp.int32, (tq, K_WIN), 1)
    if chunk_masks:
        lo = jnp.where(i < CHUNK, 0, CHUNK)
        hi = jnp.where(i < CHUNK, min(BAND_PAST + CHUNK, n_valid), n_valid)
        hidden = (j < lo) | (j >= hi)
    else:
        hidden = j >= n_valid
    for h in range(N_HEADS):
        rows = jnp.broadcast_to(fvec_ref[h:h + 1, :], (tq, Q_TILE + K_WIN))
        skew = pltpu.roll(rows, 0, 1, stride=1, stride_axis=0)
        bias_ref[h] = jnp.where(hidden, -jnp.inf, skew[:, Q_TILE:])


def _attn_prompt_kernel(q_ref, kp_ref, kc_ref, vp_ref, vc_ref, fvec_ref, o_ref, nk_ref, nv_ref,
                        kcat_ref, vcat_ref, bias_ref):
    b = pl.program_id(0)
    t = pl.program_id(1)

    @pl.when((b == 0) & (t == 0))
    def _():
        _expand_rel_bias(fvec_ref, bias_ref, K_WIN, True)

    @pl.when(t == pl.num_programs(1) - 1)
    def _():
        nk_ref[...] = _heads_to_rows(kc_ref)
        nv_ref[...] = _heads_to_rows(vc_ref)

    kcat_ref[0:ATT_BLOCK, :] = kp_ref[...].astype(BF16)
    kcat_ref[ATT_BLOCK:, :] = kc_ref[...].astype(BF16)
    vcat_ref[0:ATT_BLOCK, :] = vp_ref[...].astype(BF16)
    vcat_ref[ATT_BLOCK:, :] = vc_ref[...].astype(BF16)
    for r0 in range(0, ATT_BLOCK, Q_TILE):
        kpos0 = t * ATT_BLOCK + r0 - BAND_PAST
        _attend_heads(q_ref, kcat_ref, vcat_ref, bias_ref, o_ref,
                      slice(r0, r0 + Q_TILE), slice(r0, r0 + K_WIN), kpos0)


def _attn_prompt(u, fvec):
    b, s, _ = u.shape
    blk = (None, ATT_BLOCK, D_A)
    prev = lambda col: (lambda bi, t: (bi, jnp.maximum(t - 1, 0), col))
    cur = lambda col: (lambda bi, t: (bi, t, col))
    state = pl.BlockSpec((None, ATT_BLOCK, N_HEADS, HEAD_DIM), lambda bi, t: (bi, 0, 0, 0))
    state_shape = jax.ShapeDtypeStruct((b, ATT_BLOCK, N_HEADS, HEAD_DIM), F32)
    return pl.pallas_call(
        _attn_prompt_kernel,
        grid=(b, s // ATT_BLOCK),
        in_specs=[
            pl.BlockSpec(blk, cur(0)),
            pl.BlockSpec(blk, prev(1)),
            pl.BlockSpec(blk, cur(1)),
            pl.BlockSpec(blk, prev(2)),
            pl.BlockSpec(blk, cur(2)),
            pl.BlockSpec(fvec.shape, lambda bi, t: (0, 0)),
        ],
        out_specs=[pl.BlockSpec(blk, cur(0)), state, state],
        out_shape=[jax.ShapeDtypeStruct((b, s, D_A), F32), state_shape, state_shape],
        scratch_shapes=[
            pltpu.VMEM((2 * ATT_BLOCK, D_A), BF16),
            pltpu.VMEM((2 * ATT_BLOCK, D_A), BF16),
            pltpu.VMEM((N_HEADS, Q_TILE, K_WIN), F32),
        ],
        compiler_params=_cparams(("arbitrary", "arbitrary")),
        name="attn_prompt",
    )(u, u, u, u, u, fvec)


def _attn_sample_kernel(q_ref, kn_ref, vn_ref, kc_ref, vc_ref, fvec_ref, o_ref, nk_ref, nv_ref,
                        kcat_ref, vcat_ref, bias_ref):
    n_streams, tq, _ = q_ref.shape
    lc = kc_ref.shape[1]

    @pl.when(pl.program_id(0) == 0)
    def _():
        _expand_rel_bias(fvec_ref, bias_ref, lc + tq, False)
        kcat_ref[lc + tq:, :] = jnp.zeros((K_WIN - lc - tq, D_A), BF16)
        vcat_ref[lc + tq:, :] = jnp.zeros((K_WIN - lc - tq, D_A), BF16)

    for s in range(n_streams):
        kcat_ref[lc:lc + tq, :] = kn_ref[s].astype(BF16)
        vcat_ref[lc:lc + tq, :] = vn_ref[s].astype(BF16)
        nk_ref[s] = _heads_to_rows(kn_ref.at[s])
        nv_ref[s] = _heads_to_rows(vn_ref.at[s])
        kch = jnp.swapaxes(kc_ref[s], 0, 1)
        vch = jnp.swapaxes(vc_ref[s], 0, 1)
        for h in range(N_HEADS):
            c0 = h * HEAD_DIM
            kcat_ref[0:lc, c0:c0 + HEAD_DIM] = kch[h].astype(BF16)
            vcat_ref[0:lc, c0:c0 + HEAD_DIM] = vch[h].astype(BF16)
        _attend_heads(q_ref.at[s], kcat_ref, vcat_ref, bias_ref, o_ref.at[s], slice(None), slice(None), None)


def _attn_sample(u, k_cache, v_cache, fvec):
    b, t, _ = u.shape
    lc = k_cache.shape[1]
    assert lc + t <= K_WIN and t <= Q_TILE
    ns = SAMPLE_STREAMS_PER_STEP if b % SAMPLE_STREAMS_PER_STEP == 0 else 1
    new = lambda col: pl.BlockSpec((ns, t, D_A), lambda bi: (bi, 0, col))
    cache = pl.BlockSpec((ns, lc, N_HEADS, HEAD_DIM), lambda bi: (bi, 0, 0, 0))
    state = pl.BlockSpec((ns, t, N_HEADS, HEAD_DIM), lambda bi: (bi, 0, 0, 0))
    state_shape = jax.ShapeDtypeStruct((b, t, N_HEADS, HEAD_DIM), F32)
    return pl.pallas_call(
        _attn_sample_kernel,
        grid=(b // ns,),
        in_specs=[new(0), new(1), new(2), cache, cache, pl.BlockSpec(fvec.shape, lambda bi: (0, 0))],
        out_specs=[pl.BlockSpec((ns, t, D_A), lambda bi: (bi, 0, 0)), state, state],
        out_shape=[jax.ShapeDtypeStruct((b, t, D_A), F32), state_shape, state_shape],
        scratch_shapes=[
            pltpu.VMEM((K_WIN, D_A), BF16),
            pltpu.VMEM((K_WIN, D_A), BF16),
            pltpu.VMEM((N_HEADS, t, K_WIN), F32),
        ],
        compiler_params=_cparams(("arbitrary",)),
        name="attn_sample",
    )(u, u, u, k_cache, v_cache, fvec)


def _rel_bias_vector(rel_table):
    width = Q_TILE + K_WIN
    n_const = BAND_PAST + Q_TILE - REL_CLIP + 1
    n_rev = width - n_const
    assert 0 < n_rev <= 2 * REL_CLIP
    const = jnp.broadcast_to(rel_table[:, 2 * REL_CLIP:], (N_HEADS, n_const))
    rev = lax.rev(rel_table[:, 2 * REL_CLIP - n_rev:2 * REL_CLIP], (1,))
    return jnp.concatenate([const, rev], axis=1).astype(F32)


def _log_sigmoid(x):
    return -(jnp.maximum(-x, 0.0) + jnp.log1p(jnp.exp(-jnp.abs(x))))


def _shift_rows(x, d, fill, block):
    n, c = x.shape
    if d % SUBLANES == 0:
        x3 = x.reshape(n // block, block, c)
        pad = jnp.full((n // block, d, c), fill, x.dtype)
        return jnp.concatenate([pad, x3[:, :block - d]], axis=1).reshape(n, c)
    rolled = pltpu.roll(x, d, 0)
    row = lax.broadcasted_iota(jnp.int32, x.shape, 0)
    return jnp.where(row % block < d, fill, rolled)


def _rglru_kernel(xb_ref, lb_ref, h0_ref, cw_ref, cb_ref, wa_ref, ba_ref, wx_ref, bx_ref, lam_ref,
                  h_ref, xext_ref, hc_ref):
    t = pl.program_id(1)
    tt = xb_ref.shape[0]

    @pl.when(t == 0)
    def _():
        xext_ref[0:LRU_HALO, :] = lb_ref[...]
        hc_ref[...] = h0_ref[...]

    xext_ref[LRU_HALO:, :] = xb_ref[...]
    xc = cb_ref[...] + cw_ref[CONV_B - 1:CONV_B, :] * xb_ref[...]
    win = xext_ref[...]
    for k in range(CONV_B - 1):
        off = LRU_HALO - (CONV_B - 1) + k
        xc = xc + pltpu.roll(cw_ref[k:k + 1, :] * win, tt + LRU_HALO - off, 0)[:tt]
    xext_ref[0:LRU_HALO, :] = xext_ref[tt:tt + LRU_HALO, :]

    xcb = xc.astype(BF16)
    ra, rx = [], []
    for n in range(N_BLOCKS_B):
        blk = xcb[:, n * BLOCK_B:(n + 1) * BLOCK_B]
        ra.append(jnp.dot(blk, wa_ref[n], preferred_element_type=F32))
        rx.append(jnp.dot(blk, wx_ref[n], preferred_element_type=F32))
    r = _sigmoid(jnp.concatenate(ra, axis=-1) + ba_ref[...])
    i = _sigmoid(jnp.concatenate(rx, axis=-1) + bx_ref[...])
    log_a = LRU_C * r * _log_sigmoid(lam_ref[...])
    a = jnp.exp(log_a)
    th = jnp.tanh(log_a)
    bt = jnp.sqrt(-2.0 * th / (1.0 - th)) * (i * xc)

    block = min(tt, LRU_SCAN_ROWS)
    d = 1
    while d < block:
        bt = bt + a * _shift_rows(bt, d, 0.0, block)
        a = a * _shift_rows(a, d, 1.0, block)
        d *= 2
    carry = hc_ref[...]
    for r0 in range(0, tt, block):
        h = a[r0:r0 + block] * carry + bt[r0:r0 + block]
        h_ref[r0:r0 + block, :] = h
        carry = h[block - 1:block, :]
    hc_ref[...] = carry


def _rglru(u, lb0, h0, cw, cb, w_a, b_a, w_x, b_x, lam, *, tt):
    b, t, _ = u.shape
    row = lambda a: a.reshape(1, D_B)
    full = lambda a: pl.BlockSpec(a.shape, lambda bi, ti: (0,) * a.ndim)
    args = (cw, row(cb), w_a, row(b_a), w_x, row(b_x), row(lam))
    return pl.pallas_call(
        _rglru_kernel,
        grid=(b, t // tt),
        in_specs=[
            pl.BlockSpec((None, tt, D_B), lambda bi, ti: (bi, ti, 4 * D_A // D_B)),
            pl.BlockSpec((None, LRU_HALO, D_B), lambda bi, ti: (bi, 0, 0)),
            pl.BlockSpec((None, 1, D_B), lambda bi, ti: (bi, 0, 0)),
        ] + [full(a) for a in args],
        out_specs=pl.BlockSpec((None, tt, D_B), lambda bi, ti: (bi, ti, 0)),
        out_shape=jax.ShapeDtypeStruct((b, t, D_B), F32),
        scratch_shapes=[pltpu.VMEM((LRU_HALO + tt, D_B), F32), pltpu.VMEM((1, D_B), F32)],
        compiler_params=_cparams(("parallel", "arbitrary")),
        name="rglru",
    )(u, lb0, h0, *args)


def _gated_out_kernel(oa_ref, ga_ref, h_ref, gb_ref, x_ref, w_ref, o_ref):
    ga = ga_ref[...]
    gb = gb_ref[...]
    ma = (_silu(ga) * oa_ref[...]).astype(BF16)
    mb = (_silu(gb) * h_ref[...]).astype(BF16)
    acc = jnp.dot(ma, w_ref[0:D_A, :], preferred_element_type=F32)
    acc = acc + jnp.dot(mb, w_ref[D_A:, :], preferred_element_type=F32)
    o_ref[...] = x_ref[...] + acc


def _gated_out(o_a, u, h, x, w, *, tm):
    m = x.shape[0]
    return pl.pallas_call(
        _gated_out_kernel,
        grid=(m // tm,),
        in_specs=[
            pl.BlockSpec((tm, D_A), lambda i: (i, 0)),
            pl.BlockSpec((tm, D_A), lambda i: (i, 3)),
            pl.BlockSpec((tm, D_B), lambda i: (i, 0)),
            pl.BlockSpec((tm, D_B), lambda i: (i, (4 * D_A + D_B) // D_B)),
            pl.BlockSpec((tm, D_MODEL), lambda i: (i, 0)),
            pl.BlockSpec(w.shape, lambda i: (0, 0), pipeline_mode=pl.Buffered(1)),
        ],
        out_specs=pl.BlockSpec((tm, D_MODEL), lambda i: (i, 0)),
        out_shape=jax.ShapeDtypeStruct((m, D_MODEL), F32),
        compiler_params=_cparams(("parallel",)),
        name="gated_out_proj",
    )(o_a, u, h, u, x, w)


def _ln_out_norm_kernel(cv_ref, sg_ref, lg_ref, lb_ref, w_ref, x_ref, g_ref, o_ref):
    acc = cv_ref[...]
    mu = jnp.mean(acc, axis=-1, keepdims=True)
    cen = acc - mu
    var = jnp.mean(cen * cen, axis=-1, keepdims=True)
    yn = cen * lax.rsqrt(var + EPS) * lg_ref[...] + lb_ref[...]
    y = (_silu(yn) * sg_ref[...].astype(F32)).astype(BF16)
    x = x_ref[...] + jnp.dot(y, w_ref[...], preferred_element_type=F32)
    ms = jnp.mean(x * x, axis=-1, keepdims=True)
    o_ref[...] = x * lax.rsqrt(ms + EPS) * g_ref[...]


def _ln_out_norm(cv, sg, ln_g, ln_b, w, x, g, *, tm):
    m = x.shape[0]
    row = pl.BlockSpec((1, D_C), lambda i: (0, 0))
    return pl.pallas_call(
        _ln_out_norm_kernel,
        grid=(m // tm,),
        in_specs=[
            pl.BlockSpec((tm, D_C), lambda i: (i, 0)),
            pl.BlockSpec((tm, D_C), lambda i: (i, 0)),
            row,
            row,
            pl.BlockSpec(w.shape, lambda i: (0, 0), pipeline_mode=pl.Buffered(1)),
            pl.BlockSpec((tm, D_MODEL), lambda i: (i, 0)),
            pl.BlockSpec((1, D_MODEL), lambda i: (0, 0)),
        ],
        out_specs=pl.BlockSpec((tm, D_MODEL), lambda i: (i, 0)),
        out_shape=jax.ShapeDtypeStruct((m, D_MODEL), F32),
        compiler_params=_cparams(("parallel",)),
        name="ln_out_proj_final_norm",
    )(cv, sg, ln_g.reshape(1, D_C), ln_b.reshape(1, D_C), w, x, g)


def _front_pad_rows(a, rows):
    return jnp.pad(a, ((0, 0), (rows - a.shape[1], 0), (0, 0)))


def _trunk(x, caches, w, *, tm, tn_in, tm_out, lru_tt):
    b, t, _ = x.shape
    m = b * t
    x2d = x.reshape(m, D_MODEL)

    u = _norm_matmul(x2d, w["norm_ab"], w["w_in_ab"], tm=tm, tn=tn_in)
    u3 = u.reshape(b, t, IN_AB)
    if caches is None:
        assert t % ATT_BLOCK == 0 and ATT_BLOCK == BAND_PAST
        o_a, new_k, new_v = _attn_prompt(u3, w["rel_bias_vec"])
        h0 = jnp.zeros((b, 1, D_B), F32)
        lb0 = jnp.zeros((b, LRU_HALO, D_B), F32)
        cb0 = jnp.zeros((b, CONV_HALO, D_C), F32)
    else:
        kc, vc, hc, lbc, cbc = caches
        o_a, new_k, new_v = _attn_sample(u3, kc, vc, w["rel_bias_vec"])
        h0 = hc.reshape(b, 1, D_B)
        lb0 = _front_pad_rows(lbc, LRU_HALO)
        cb0 = _front_pad_rows(cbc, CONV_HALO)
    h = _rglru(u3, lb0, h0, w["lru_conv_w"], w["lru_conv_b"], w["lru_w_a"], w["lru_b_a"],
               w["lru_w_x"], w["lru_b_x"], w["lru_lambda"], tt=lru_tt)
    x1 = _gated_out(o_a.reshape(m, D_A), u, h.reshape(m, D_B), x2d, w["w_out_ab"], tm=tm_out)

    new_h = h[:, t - 1]
    new_lb = u3[:, t - (CONV_B - 1):, 4 * D_A:4 * D_A + D_B]

    cv, sg, z_tail = _norm_glu_conv(x1, w["norm_cv"], w["w_in_cv"], cb0, w["dw_w"], w["dw_b"], t=t, tm=tm, tn=512)
    out = _ln_out_norm(cv, sg, w["ln_g"], w["ln_b"], w["w_out_cv"], x1, w["final_norm"], tm=tm_out)
    segs_per_batch = z_tail.shape[0] // b
    new_cb = z_tail[segs_per_batch - 1::segs_per_batch, CONV_HALO - (CONV_C - 1):]

    return (out.reshape(b, t, D_MODEL), new_k[None], new_v[None], new_h[None], new_lb[None], new_cb[None])


def kernel(x_prompt, x_sample, cache_attn_k, cache_attn_v, state_lru_h, state_lru_conv, state_conv, norm_ab, w_in_ab, w_out_ab, rel_bias, lru_conv_w, lru_conv_b, lru_w_a, lru_b_a, lru_w_x, lru_b_x, lru_lambda, norm_cv, w_in_cv, w_out_cv, dw_w, dw_b, ln_g, ln_b, final_norm):
    assert norm_ab.shape[0] == 1 and norm_cv.shape[0] == 1, "one even and one odd layer"
    t_s = x_sample.shape[1]
    w = {
        "norm_ab": norm_ab[0].reshape(1, D_MODEL),
        "w_in_ab": w_in_ab[0].astype(BF16),
        "w_out_ab": w_out_ab[0].astype(BF16),
        "rel_bias_vec": _rel_bias_vector(rel_bias[0]),
        "lru_conv_w": lru_conv_w[0],
        "lru_conv_b": lru_conv_b[0],
        "lru_w_a": lru_w_a[0].astype(BF16),
        "lru_b_a": lru_b_a[0],
        "lru_w_x": lru_w_x[0].astype(BF16),
        "lru_b_x": lru_b_x[0],
        "lru_lambda": lru_lambda[0],
        "norm_cv": norm_cv[0].reshape(1, D_MODEL),
        "w_in_cv": w_in_cv[0].astype(BF16),
        "w_out_cv": w_out_cv[0].astype(BF16),
        "dw_w": dw_w[0],
        "dw_b": dw_b[0],
        "ln_g": ln_g[0],
        "ln_b": ln_b[0],
        "final_norm": final_norm.reshape(1, D_MODEL),
    }
    y_p, k_p, v_p, h_p, lb_p, cb_p = _trunk(x_prompt, None, w, tm=1024, tn_in=1536, tm_out=512, lru_tt=512)
    caches = (cache_attn_k[0], cache_attn_v[0], state_lru_h[0], state_lru_conv[0], state_conv[0])
    y_s, k_s, v_s, h_s, lb_s, cb_s = _trunk(x_sample, caches, w, tm=512, tn_in=1536, tm_out=512, lru_tt=t_s)
    return (y_p, y_s, k_p, v_p, h_p, lb_p, cb_p, k_s, v_s, h_s, lb_s, cb_s)
```

```python
import functools

import jax
import jax.numpy as jnp
from jax import lax
from jax.experimental import pallas as pl
from jax.experimental.pallas import tpu as pltpu

D_MODEL = 2048
CHUNK = 64
LEFT_CHUNKS = 8
BAND_PAST = CHUNK * LEFT_CHUNKS
N_HEADS = 8
HEAD_DIM = 128
D_A = N_HEADS * HEAD_DIM
REL_CLIP = 128
D_B = 1024
N_BLOCKS_B = 8
BLOCK_B = D_B // N_BLOCKS_B
CONV_B = 4
LRU_C = 8.0
D_C = D_MODEL
CONV_C = 31
IN_AB = 4 * D_A + 2 * D_B
IN_CV = 3 * D_C
EPS = 1e-6

F32 = jnp.float32
BF16 = jnp.bfloat16

SUBLANES = 8
LANES = 128
Q_TILE = 2 * CHUNK
K_WIN = BAND_PAST + Q_TILE
ATT_BLOCK = 512
SAMPLE_STREAMS_PER_STEP = 2
CONV_HALO = 32
CONV_ROWS = 128
LRU_SCAN_ROWS = 32
LRU_HALO = 8
VMEM_LIMIT = 56 * 1024 * 1024


def _cparams(sem):
    return pltpu.CompilerParams(dimension_semantics=sem, vmem_limit_bytes=VMEM_LIMIT)


def _sigmoid(x):
    return 0.5 * jnp.tanh(0.5 * x) + 0.5


def _silu(x):
    half = 0.5 * x
    return half + half * jnp.tanh(half)


def _norm_to_bf16(x, g):
    ms = jnp.mean(x * x, axis=-1, keepdims=True)
    return (x * lax.rsqrt(ms + EPS) * g).astype(BF16)


U_Q, U_K, U_V, U_GA, U_XB, U_GB = range(6)
U32_K, U32_V, U32_XB = U_K // 2, U_V // 2, U_XB // 2


def _norm_matmul_kernel(x_ref, g_ref, w_ref, o16_ref, o32_ref, xn_ref):
    j = pl.program_id(1)

    @pl.when(j == 0)
    def _():
        xn_ref[...] = _norm_to_bf16(x_ref[...], g_ref[...])

    acc = jnp.dot(xn_ref[...], w_ref[...], preferred_element_type=F32)
    o16_ref[...] = acc.astype(BF16)

    @pl.when((j == U_K) | (j == U_V) | (j == U_XB))
    def _():
        o32_ref[...] = acc


def _norm_matmul(x, g, w, *, tm):
    m, d = x.shape
    assert D_A == D_B and w.shape[1] == IN_AB
    return pl.pallas_call(
        _norm_matmul_kernel,
        grid=(m // tm, IN_AB // D_A),
        in_specs=[
            pl.BlockSpec((tm, d), lambda i, j: (i, 0)),
            pl.BlockSpec((1, d), lambda i, j: (0, 0)),
            pl.BlockSpec((d, D_A), lambda i, j: (0, j)),
        ],
        out_specs=[pl.BlockSpec((tm, D_A), lambda i, j: (i, j)), pl.BlockSpec((tm, D_A), lambda i, j: (i, j // 2))],
        out_shape=[jax.ShapeDtypeStruct((m, IN_AB), BF16), jax.ShapeDtypeStruct((m, 3 * D_A), F32)],
        scratch_shapes=[pltpu.VMEM((tm, d), BF16)],
        compiler_params=_cparams(("parallel", "arbitrary")),
        name="norm_in_proj",
    )(x, g, w)


def _dwconv_rows(zext_ref, base, rows, lane0, w_ref, b_ref):
    lanes = slice(lane0, lane0 + LANES)
    acc = jnp.broadcast_to(b_ref[...], (rows, LANES))
    for phase in range(SUBLANES):
        win_rows = rows if phase == 0 else rows + SUBLANES
        part = None
        for k in range(CONV_C):
            off = base + CONV_HALO - (CONV_C - 1) + k
            if off % SUBLANES != phase:
                continue
            tap = w_ref[k]
            win = zext_ref[off - phase:off - phase + win_rows, lanes]
            term = win.reshape(win_rows // SUBLANES, SUBLANES, LANES) * tap[None]
            part = term if part is None else part + term
        if part is None:
            continue
        part = part.reshape(win_rows, LANES)
        if phase:
            part = pltpu.roll(part, win_rows - phase, 0)[:rows]
        acc = acc + part
    return acc


def _norm_glu_conv_kernel(x_ref, g_ref, wv_ref, wg_ref, wt_ref, st_ref, cw_ref, cb_ref,
                          cv_ref, sg_ref, tail_ref, xn_ref, zext_ref, carry_ref, *, seg, tiles_per_batch):
    i = pl.program_id(0)
    j = pl.program_id(1)
    n_seg = x_ref.shape[0] // seg
    ext = CONV_HALO + seg

    @pl.when(j == 0)
    def _():
        xn_ref[...] = _norm_to_bf16(x_ref[...], g_ref[...])

    xn = xn_ref[...]
    gate = jnp.dot(xn, wt_ref[...], preferred_element_type=F32)
    sg_ref[...] = _silu(gate).astype(BF16)
    val = jnp.dot(xn, wv_ref[...], preferred_element_type=F32)
    glu = jnp.dot(xn, wg_ref[...], preferred_element_type=F32)
    z = val * _sigmoid(glu)

    carried = tiles_per_batch > 1
    if carried:
        @pl.when(i % tiles_per_batch == 0)
        def _():
            carry_ref[j] = st_ref[0]

    rows = min(seg, CONV_ROWS)
    for s in range(n_seg):
        base = s * ext
        zext_ref[base:base + CONV_HALO, :] = carry_ref[j] if carried else st_ref[s]
        zext_ref[base + CONV_HALO:base + ext, :] = z[s * seg:(s + 1) * seg]
        for c in range(cw_ref.shape[0]):
            for r0 in range(0, seg, rows):
                out_rows = slice(s * seg + r0, s * seg + r0 + rows)
                cv_ref[out_rows, c * LANES:(c + 1) * LANES] = _dwconv_rows(
                    zext_ref, base + r0, rows, c * LANES, cw_ref.at[c], cb_ref.at[c])
        tail = zext_ref[base + seg:base + ext, :]
        tail_ref[s] = tail
        if carried:
            carry_ref[j] = tail


def _norm_glu_conv(x, g, w, state, dw_w, dw_b, *, t, tm, tn):
    m, d = x.shape
    nb = D_C // tn
    seg = min(t, tm)
    assert tm % seg == 0 and t % seg == 0 and seg % min(seg, CONV_ROWS) == 0 and seg % SUBLANES == 0
    tiles_per_batch = t // seg
    dw_w = dw_w.reshape(CONV_C, D_C // LANES, 1, LANES).transpose(1, 0, 2, 3)
    dw_w = jnp.broadcast_to(dw_w, (D_C // LANES, CONV_C, SUBLANES, LANES))
    dw_b = dw_b.reshape(D_C // LANES, 1, LANES)
    bpt = tm // seg if tiles_per_batch == 1 else 1
    batch_blk = (lambda i: i // tiles_per_batch) if tiles_per_batch > 1 else (lambda i: i)
    state_spec = pl.BlockSpec((bpt, CONV_HALO, tn), lambda i, j: (batch_blk(i), 0, j))
    body = functools.partial(_norm_glu_conv_kernel, seg=seg, tiles_per_batch=tiles_per_batch)
    return pl.pallas_call(
        body,
        grid=(m // tm, nb),
        in_specs=[
            pl.BlockSpec((tm, d), lambda i, j: (i, 0)),
            pl.BlockSpec((1, d), lambda i, j: (0, 0)),
            pl.BlockSpec((d, tn), lambda i, j: (0, j)),
            pl.BlockSpec((d, tn), lambda i, j: (0, j + nb)),
            pl.BlockSpec((d, tn), lambda i, j: (0, j + 2 * nb)),
            state_spec,
            pl.BlockSpec((tn // LANES, CONV_C, SUBLANES, LANES), lambda i, j: (j, 0, 0, 0)),
            pl.BlockSpec((tn // LANES, 1, LANES), lambda i, j: (j, 0, 0)),
        ],
        out_specs=[
            pl.BlockSpec((tm, tn), lambda i, j: (i, j)),
            pl.BlockSpec((tm, tn), lambda i, j: (i, j)),
            pl.BlockSpec((tm // seg, CONV_HALO, tn), lambda i, j: (i, 0, j)),
        ],
        out_shape=[
            jax.ShapeDtypeStruct((m, D_C), F32),
            jax.ShapeDtypeStruct((m, D_C), BF16),
            jax.ShapeDtypeStruct((m // seg, CONV_HALO, D_C), F32),
        ],
        scratch_shapes=[
            pltpu.VMEM((tm, d), BF16),
            pltpu.VMEM(((tm // seg) * (CONV_HALO + seg), tn), F32),
            pltpu.VMEM((nb, CONV_HALO, tn), F32),
        ],
        compiler_params=_cparams(("arbitrary", "arbitrary")),
        name="norm_in_proj_glu_conv",
    )(x, g, w, w, w, state, dw_w, dw_b)


def _attend_heads(q_ref, kcat_ref, vcat_ref, bias_ref, o_ref, q_rows, k_rows, kpos0):
    heads = [slice(h * HEAD_DIM, (h + 1) * HEAD_DIM) for h in range(N_HEADS)]
    scores = []
    for cols in heads:
        q = q_ref[q_rows, cols].astype(BF16)
        scores.append(lax.dot_general(q, kcat_ref[k_rows, cols], (((1,), (1,)), ((), ())),
                                      preferred_element_type=F32))
    if kpos0 is not None:
        col = lax.broadcasted_iota(jnp.int32, (1, scores[0].shape[1]), 1)
        before_start = jnp.where(col + kpos0 >= 0, 0.0, -jnp.inf)
    probs = []
    for h, s in enumerate(scores):
        s = s * (HEAD_DIM ** -0.5) + bias_ref[h]
        if kpos0 is not None:
            s = s + before_start
        m = jnp.max(s, axis=-1, keepdims=True)
        p = jnp.exp(s - m)
        probs.append((p.astype(BF16), jnp.sum(p, axis=-1, keepdims=True)))
    for cols, (p, l) in zip(heads, probs):
        o = jnp.dot(p, vcat_ref[k_rows, cols], preferred_element_type=F32) / l
        o_ref[q_rows, cols] = o.astype(o_ref.dtype)


def _heads_to_rows(x_ref):
    heads = [x_ref[:, h * HEAD_DIM:(h + 1) * HEAD_DIM] for h in range(N_HEADS)]
    return jnp.swapaxes(jnp.stack(heads, axis=0), 0, 1)


def _expand_rel_bias(fvec_ref, bias_ref, n_valid, chunk_masks):
    tq = bias_ref.shape[1]
    i = lax.broadcasted_iota(jnp.int32, (tq, K_WIN), 0)
    j = lax.broadcasted_iota(jnp.int32, (tq, K_WIN), 1)
    if chunk_masks:
        lo = jnp.where(i < CHUNK, 0, CHUNK)
        hi = jnp.where(i < CHUNK, min(BAND_PAST + CHUNK, n_valid), n_valid)
        hidden = (j < lo) | (j >= hi)
    else:
        hidden = j >= n_valid
    for h in range(N_HEADS):
        rows = jnp.broadcast_to(fvec_ref[h:h + 1, :], (tq, Q_TILE + K_WIN))
        skew = pltpu.roll(rows, 0, 1, stride=1, stride_axis=0)
        bias_ref[h] = jnp.where(hidden, -jnp.inf, skew[:, Q_TILE:])


def _attn_prompt_kernel(q_ref, kp_ref, kc_ref, vp_ref, vc_ref, kf_ref, vf_ref, fvec_ref, o_ref, nk_ref, nv_ref,
                        kcat_ref, vcat_ref, bias_ref):
    b = pl.program_id(0)
    t = pl.program_id(1)

    @pl.when((b == 0) & (t == 0))
    def _():
        _expand_rel_bias(fvec_ref, bias_ref, K_WIN, True)

    @pl.when(t == pl.num_programs(1) - 1)
    def _():
        nk_ref[...] = _heads_to_rows(kf_ref)
        nv_ref[...] = _heads_to_rows(vf_ref)

    kcat_ref[0:ATT_BLOCK, :] = kp_ref[...]
    kcat_ref[ATT_BLOCK:, :] = kc_ref[...]
    vcat_ref[0:ATT_BLOCK, :] = vp_ref[...]
    vcat_ref[ATT_BLOCK:, :] = vc_ref[...]
    for r0 in range(0, ATT_BLOCK, Q_TILE):
        kpos0 = t * ATT_BLOCK + r0 - BAND_PAST
        _attend_heads(q_ref, kcat_ref, vcat_ref, bias_ref, o_ref,
                      slice(r0, r0 + Q_TILE), slice(r0, r0 + K_WIN), kpos0)


def _attn_prompt(u16, u32, fvec):
    b, s, _ = u16.shape
    blk = (None, ATT_BLOCK, D_A)
    last = s // ATT_BLOCK - 1
    prev = lambda col: (lambda bi, t: (bi, jnp.maximum(t - 1, 0), col))
    cur = lambda col: (lambda bi, t: (bi, t, col))
    newest = lambda col: (lambda bi, t: (bi, last, col))
    state = pl.BlockSpec((None, ATT_BLOCK, N_HEADS, HEAD_DIM), lambda bi, t: (bi, 0, 0, 0))
    state_shape = jax.ShapeDtypeStruct((b, ATT_BLOCK, N_HEADS, HEAD_DIM), F32)
    return pl.pallas_call(
        _attn_prompt_kernel,
        grid=(b, s // ATT_BLOCK),
        in_specs=[
            pl.BlockSpec(blk, cur(U_Q)),
            pl.BlockSpec(blk, prev(U_K)),
            pl.BlockSpec(blk, cur(U_K)),
            pl.BlockSpec(blk, prev(U_V)),
            pl.BlockSpec(blk, cur(U_V)),
            pl.BlockSpec(blk, newest(U32_K)),
            pl.BlockSpec(blk, newest(U32_V)),
            pl.BlockSpec(fvec.shape, lambda bi, t: (0, 0)),
        ],
        out_specs=[pl.BlockSpec(blk, cur(0)), state, state],
        out_shape=[jax.ShapeDtypeStruct((b, s, D_A), BF16), state_shape, state_shape],
        scratch_shapes=[
            pltpu.VMEM((2 * ATT_BLOCK, D_A), BF16),
            pltpu.VMEM((2 * ATT_BLOCK, D_A), BF16),
            pltpu.VMEM((N_HEADS, Q_TILE, K_WIN), F32),
        ],
        compiler_params=_cparams(("arbitrary", "arbitrary")),
        name="attn_prompt",
    )(u16, u16, u16, u16, u16, u32, u32, fvec)


def _attn_sample_kernel(q_ref, kn_ref, vn_ref, kc_ref, vc_ref, fvec_ref, o_ref, nk_ref, nv_ref,
                        kcat_ref, vcat_ref, bias_ref):
    n_streams, tq, _ = q_ref.shape
    lc = kc_ref.shape[1]

    @pl.when(pl.program_id(0) == 0)
    def _():
        _expand_rel_bias(fvec_ref, bias_ref, lc + tq, False)
        kcat_ref[lc + tq:, :] = jnp.zeros((K_WIN - lc - tq, D_A), BF16)
        vcat_ref[lc + tq:, :] = jnp.zeros((K_WIN - lc - tq, D_A), BF16)

    for s in range(n_streams):
        kcat_ref[lc:lc + tq, :] = kn_ref[s].astype(BF16)
        vcat_ref[lc:lc + tq, :] = vn_ref[s].astype(BF16)
        nk_ref[s] = _heads_to_rows(kn_ref.at[s])
        nv_ref[s] = _heads_to_rows(vn_ref.at[s])
        kch = jnp.swapaxes(kc_ref[s], 0, 1)
        vch = jnp.swapaxes(vc_ref[s], 0, 1)
        for h in range(N_HEADS):
            c0 = h * HEAD_DIM
            kcat_ref[0:lc, c0:c0 + HEAD_DIM] = kch[h].astype(BF16)
            vcat_ref[0:lc, c0:c0 + HEAD_DIM] = vch[h].astype(BF16)
        _attend_heads(q_ref.at[s], kcat_ref, vcat_ref, bias_ref, o_ref.at[s], slice(None), slice(None), None)


def _attn_sample(u16, u32, k_cache, v_cache, fvec):
    b, t, _ = u16.shape
    lc = k_cache.shape[1]
    assert lc + t <= K_WIN and t <= Q_TILE
    ns = SAMPLE_STREAMS_PER_STEP if b % SAMPLE_STREAMS_PER_STEP == 0 else 1
    new = lambda col: pl.BlockSpec((ns, t, D_A), lambda bi: (bi, 0, col))
    cache = pl.BlockSpec((ns, lc, N_HEADS, HEAD_DIM), lambda bi: (bi, 0, 0, 0))
    state = pl.BlockSpec((ns, t, N_HEADS, HEAD_DIM), lambda bi: (bi, 0, 0, 0))
    state_shape = jax.ShapeDtypeStruct((b, t, N_HEADS, HEAD_DIM), F32)
    return pl.pallas_call(
        _attn_sample_kernel,
        grid=(b // ns,),
        in_specs=[new(U_Q), new(U32_K), new(U32_V), cache, cache, pl.BlockSpec(fvec.shape, lambda bi: (0, 0))],
        out_specs=[pl.BlockSpec((ns, t, D_A), lambda bi: (bi, 0, 0)), state, state],
        out_shape=[jax.ShapeDtypeStruct((b, t, D_A), BF16), state_shape, state_shape],
        scratch_shapes=[
            pltpu.VMEM((K_WIN, D_A), BF16),
            pltpu.VMEM((K_WIN, D_A), BF16),
            pltpu.VMEM((N_HEADS, t, K_WIN), F32),
        ],
        compiler_params=_cparams(("arbitrary",)),
        name="attn_sample",
    )(u16, u32, u32, k_cache, v_cache, fvec)


def _rel_bias_vector(rel_table):
    width = Q_TILE + K_WIN
    n_const = BAND_PAST + Q_TILE - REL_CLIP + 1
    n_rev = width - n_const
    assert 0 < n_rev <= 2 * REL_CLIP
    const = jnp.broadcast_to(rel_table[:, 2 * REL_CLIP:], (N_HEADS, n_const))
    rev = lax.rev(rel_table[:, 2 * REL_CLIP - n_rev:2 * REL_CLIP], (1,))
    return jnp.concatenate([const, rev], axis=1).astype(F32)


def _log_sigmoid(x):
    return -(jnp.maximum(-x, 0.0) + jnp.log1p(jnp.exp(-jnp.abs(x))))


def _shift_rows(x, d, fill, block):
    n, c = x.shape
    if d % SUBLANES == 0:
        x3 = x.reshape(n // block, block, c)
        pad = jnp.full((n // block, d, c), fill, x.dtype)
        return jnp.concatenate([pad, x3[:, :block - d]], axis=1).reshape(n, c)
    rolled = pltpu.roll(x, d, 0)
    row = lax.broadcasted_iota(jnp.int32, x.shape, 0)
    return jnp.where(row % block < d, fill, rolled)


def _rglru_kernel(xb_ref, lb_ref, h0_ref, cw_ref, cb_ref, wa_ref, ba_ref, wx_ref, bx_ref, lam_ref,
                  h_ref, hl_ref, xext_ref, hc_ref):
    t = pl.program_id(1)
    tt = xb_ref.shape[0]

    @pl.when(t == 0)
    def _():
        xext_ref[0:LRU_HALO, :] = lb_ref[...]
        hc_ref[...] = h0_ref[...]

    xext_ref[LRU_HALO:, :] = xb_ref[...]
    xc = cb_ref[...] + cw_ref[CONV_B - 1:CONV_B, :] * xb_ref[...]
    win = xext_ref[...]
    for k in range(CONV_B - 1):
        off = LRU_HALO - (CONV_B - 1) + k
        xc = xc + pltpu.roll(cw_ref[k:k + 1, :] * win, tt + LRU_HALO - off, 0)[:tt]
    xext_ref[0:LRU_HALO, :] = xext_ref[tt:tt + LRU_HALO, :]

    xcb = xc.astype(BF16)
    ra, rx = [], []
    for n in range(N_BLOCKS_B):
        blk = xcb[:, n * BLOCK_B:(n + 1) * BLOCK_B]
        ra.append(jnp.dot(blk, wa_ref[n], preferred_element_type=F32))
        rx.append(jnp.dot(blk, wx_ref[n], preferred_element_type=F32))
    r = _sigmoid(jnp.concatenate(ra, axis=-1) + ba_ref[...])
    i = _sigmoid(jnp.concatenate(rx, axis=-1) + bx_ref[...])
    log_a = LRU_C * r * _log_sigmoid(lam_ref[...])
    a = jnp.exp(log_a)
    th = jnp.tanh(log_a)
    bt = jnp.sqrt(-2.0 * th / (1.0 - th)) * (i * xc)

    block = min(tt, LRU_SCAN_ROWS)
    d = 1
    while d < block:
        bt = bt + a * _shift_rows(bt, d, 0.0, block)
        a = a * _shift_rows(a, d, 1.0, block)
        d *= 2
    carry = hc_ref[...]
    for r0 in range(0, tt, block):
        h = a[r0:r0 + block] * carry + bt[r0:r0 + block]
        h_ref[r0:r0 + block, :] = h.astype(h_ref.dtype)
        carry = h[block - 1:block, :]
    hc_ref[...] = carry
    hl_ref[...] = carry


def _rglru(u32, lb0, h0, cw, cb, w_a, b_a, w_x, b_x, lam, *, tt):
    b, t, _ = u32.shape
    row = lambda a: a.reshape(1, D_B)
    full = lambda a: pl.BlockSpec(a.shape, lambda bi, ti: (0,) * a.ndim)
    args = (cw, row(cb), w_a, row(b_a), w_x, row(b_x), row(lam))
    return pl.pallas_call(
        _rglru_kernel,
        grid=(b, t // tt),
        in_specs=[
            pl.BlockSpec((None, tt, D_B), lambda bi, ti: (bi, ti, U32_XB)),
            pl.BlockSpec((None, LRU_HALO, D_B), lambda bi, ti: (bi, 0, 0)),
            pl.BlockSpec((None, 1, D_B), lambda bi, ti: (bi, 0, 0)),
        ] + [full(a) for a in args],
        out_specs=[
            pl.BlockSpec((None, tt, D_B), lambda bi, ti: (bi, ti, 0)),
            pl.BlockSpec((None, 1, D_B), lambda bi, ti: (bi, 0, 0)),
        ],
        out_shape=[jax.ShapeDtypeStruct((b, t, D_B), BF16), jax.ShapeDtypeStruct((b, 1, D_B), F32)],
        scratch_shapes=[pltpu.VMEM((LRU_HALO + tt, D_B), F32), pltpu.VMEM((1, D_B), F32)],
        compiler_params=_cparams(("parallel", "arbitrary")),
        name="rglru",
    )(u32, lb0, h0, *args)


def _gated_out_kernel(oa_ref, ga_ref, h_ref, gb_ref, x_ref, w_ref, o_ref):
    ma = (_silu(ga_ref[...].astype(F32)) * oa_ref[...].astype(F32)).astype(BF16)
    mb = (_silu(gb_ref[...].astype(F32)) * h_ref[...].astype(F32)).astype(BF16)
    acc = jnp.dot(ma, w_ref[0:D_A, :], preferred_element_type=F32)
    acc = acc + jnp.dot(mb, w_ref[D_A:, :], preferred_element_type=F32)
    o_ref[...] = x_ref[...] + acc


def _gated_out(o_a, u16, h, x, w, *, tm):
    m = x.shape[0]
    return pl.pallas_call(
        _gated_out_kernel,
        grid=(m // tm,),
        in_specs=[
            pl.BlockSpec((tm, D_A), lambda i: (i, 0)),
            pl.BlockSpec((tm, D_A), lambda i: (i, U_GA)),
            pl.BlockSpec((tm, D_B), lambda i: (i, 0)),
            pl.BlockSpec((tm, D_B), lambda i: (i, U_GB)),
            pl.BlockSpec((tm, D_MODEL), lambda i: (i, 0)),
            pl.BlockSpec(w.shape, lambda i: (0, 0), pipeline_mode=pl.Buffered(1)),
        ],
        out_specs=pl.BlockSpec((tm, D_MODEL), lambda i: (i, 0)),
        out_shape=jax.ShapeDtypeStruct((m, D_MODEL), F32),
        compiler_params=_cparams(("parallel",)),
        name="gated_out_proj",
    )(o_a, u16, h, u16, x, w)


def _ln_out_norm_kernel(cv_ref, sg_ref, lg_ref, lb_ref, w_ref, x_ref, g_ref, o_ref):
    acc = cv_ref[...]
    mu = jnp.mean(acc, axis=-1, keepdims=True)
    cen = acc - mu
    var = jnp.mean(cen * cen, axis=-1, keepdims=True)
    yn = cen * lax.rsqrt(var + EPS) * lg_ref[...] + lb_ref[...]
    y = (_silu(yn) * sg_ref[...].astype(F32)).astype(BF16)
    x = x_ref[...] + jnp.dot(y, w_ref[...], preferred_element_type=F32)
    ms = jnp.mean(x * x, axis=-1, keepdims=True)
    o_ref[...] = x * lax.rsqrt(ms + EPS) * g_ref[...]


def _ln_out_norm(cv, sg, ln_g, ln_b, w, x, g, *, tm):
    m = x.shape[0]
    row = pl.BlockSpec((1, D_C), lambda i: (0, 0))
    return pl.pallas_call(
        _ln_out_norm_kernel,
        grid=(m // tm,),
        in_specs=[
            pl.BlockSpec((tm, D_C), lambda i: (i, 0)),
            pl.BlockSpec((tm, D_C), lambda i: (i, 0)),
            row,
            row,
            pl.BlockSpec(w.shape, lambda i: (0, 0), pipeline_mode=pl.Buffered(1)),
            pl.BlockSpec((tm, D_MODEL), lambda i: (i, 0)),
            pl.BlockSpec((1, D_MODEL), lambda i: (0, 0)),
        ],
        out_specs=pl.BlockSpec((tm, D_MODEL), lambda i: (i, 0)),
        out_shape=jax.ShapeDtypeStruct((m, D_MODEL), F32),
        compiler_params=_cparams(("parallel",)),
        name="ln_out_proj_final_norm",
    )(cv, sg, ln_g.reshape(1, D_C), ln_b.reshape(1, D_C), w, x, g)


def _front_pad_rows(a, rows):
    return jnp.pad(a, ((0, 0), (rows - a.shape[1], 0), (0, 0)))


def _trunk(x, caches, w, *, tm, tm_out, lru_tt):
    b, t, _ = x.shape
    m = b * t
    x2d = x.reshape(m, D_MODEL)

    u16, u32 = _norm_matmul(x2d, w["norm_ab"], w["w_in_ab"], tm=tm)
    u16_3 = u16.reshape(b, t, IN_AB)
    u32_3 = u32.reshape(b, t, 3 * D_A)
    if caches is None:
        assert t % ATT_BLOCK == 0 and ATT_BLOCK == BAND_PAST
        o_a, new_k, new_v = _attn_prompt(u16_3, u32_3, w["rel_bias_vec"])
        h0 = jnp.zeros((b, 1, D_B), F32)
        lb0 = jnp.zeros((b, LRU_HALO, D_B), F32)
        cb0 = jnp.zeros((b, CONV_HALO, D_C), F32)
    else:
        kc, vc, hc, lbc, cbc = caches
        o_a, new_k, new_v = _attn_sample(u16_3, u32_3, kc, vc, w["rel_bias_vec"])
        h0 = hc.reshape(b, 1, D_B)
        lb0 = _front_pad_rows(lbc, LRU_HALO)
        cb0 = _front_pad_rows(cbc, CONV_HALO)
    h, h_last = _rglru(u32_3, lb0, h0, w["lru_conv_w"], w["lru_conv_b"], w["lru_w_a"], w["lru_b_a"],
                       w["lru_w_x"], w["lru_b_x"], w["lru_lambda"], tt=lru_tt)
    x1 = _gated_out(o_a.reshape(m, D_A), u16, h.reshape(m, D_B), x2d, w["w_out_ab"], tm=tm_out)

    new_h = h_last[:, 0]
    new_lb = u32_3[:, t - (CONV_B - 1):, U32_XB * D_B:(U32_XB + 1) * D_B]

    cv, sg, z_tail = _norm_glu_conv(x1, w["norm_cv"], w["w_in_cv"], cb0, w["dw_w"], w["dw_b"], t=t, tm=tm, tn=512)
    out = _ln_out_norm(cv, sg, w["ln_g"], w["ln_b"], w["w_out_cv"], x1, w["final_norm"], tm=tm_out)
    segs_per_batch = z_tail.shape[0] // b
    new_cb = z_tail[segs_per_batch - 1::segs_per_batch, CONV_HALO - (CONV_C - 1):]

    return (out.reshape(b, t, D_MODEL), new_k[None], new_v[None], new_h[None], new_lb[None], new_cb[None])


def kernel(x_prompt, x_sample, cache_attn_k, cache_attn_v, state_lru_h, state_lru_conv, state_conv, norm_ab, w_in_ab, w_out_ab, rel_bias, lru_conv_w, lru_conv_b, lru_w_a, lru_b_a, lru_w_x, lru_b_x, lru_lambda, norm_cv, w_in_cv, w_out_cv, dw_w, dw_b, ln_g, ln_b, final_norm):
    assert norm_ab.shape[0] == 1 and norm_cv.shape[0] == 1, "one even and one odd layer"
    t_s = x_sample.shape[1]
    w = {
        "norm_ab": norm_ab[0].reshape(1, D_MODEL),
        "w_in_ab": w_in_ab[0].astype(BF16),
        "w_out_ab": w_out_ab[0].astype(BF16),
        "rel_bias_vec": _rel_bias_vector(rel_bias[0]),
        "lru_conv_w": lru_conv_w[0],
        "lru_conv_b": lru_conv_b[0],
        "lru_w_a": lru_w_a[0].astype(BF16),
        "lru_b_a": lru_b_a[0],
        "lru_w_x": lru_w_x[0].astype(BF16),
        "lru_b_x": lru_b_x[0],
        "lru_lambda": lru_lambda[0],
        "norm_cv": norm_cv[0].reshape(1, D_MODEL),
        "w_in_cv": w_in_cv[0].astype(BF16),
        "w_out_cv": w_out_cv[0].astype(BF16),
        "dw_w": dw_w[0],
        "dw_b": dw_b[0],
        "ln_g": ln_g[0],
        "ln_b": ln_b[0],
        "final_norm": final_norm.reshape(1, D_MODEL),
    }
    y_p, k_p, v_p, h_p, lb_p, cb_p = _trunk(x_prompt, None, w, tm=1024, tm_out=512, lru_tt=512)
    caches = (cache_attn_k[0], cache_attn_v[0], state_lru_h[0], state_lru_conv[0], state_conv[0])
    y_s, k_s, v_s, h_s, lb_s, cb_s = _trunk(x_sample, caches, w, tm=512, tm_out=512, lru_tt=t_s)
    return (y_p, y_s, k_p, v_p, h_p, lb_p, cb_p, k_s, v_s, h_s, lb_s, cb_s)
```

```python
import functools

import jax
import jax.numpy as jnp
from jax import lax
from jax.experimental import pallas as pl
from jax.experimental.pallas import tpu as pltpu

D_MODEL = 2048
CHUNK = 64
LEFT_CHUNKS = 8
BAND_PAST = CHUNK * LEFT_CHUNKS
N_HEADS = 8
HEAD_DIM = 128
D_A = N_HEADS * HEAD_DIM
REL_CLIP = 128
D_B = 1024
N_BLOCKS_B = 8
BLOCK_B = D_B // N_BLOCKS_B
CONV_B = 4
LRU_C = 8.0
D_C = D_MODEL
CONV_C = 31
IN_AB = 4 * D_A + 2 * D_B
IN_CV = 3 * D_C
EPS = 1e-6

F32 = jnp.float32
BF16 = jnp.bfloat16

SUBLANES = 8
LANES = 128
Q_TILE = 2 * CHUNK
K_WIN = BAND_PAST + Q_TILE
ATT_BLOCK = 512
SAMPLE_STREAMS_PER_STEP = 2
CONV_HALO = 32
CONV_ROWS = 128
LRU_SCAN_ROWS = 32
LRU_HALO = 8
VMEM_LIMIT = 56 * 1024 * 1024


def _cparams(sem):
    return pltpu.CompilerParams(dimension_semantics=sem, vmem_limit_bytes=VMEM_LIMIT)


def _sigmoid(x):
    return 0.5 * jnp.tanh(0.5 * x) + 0.5


def _silu(x):
    half = 0.5 * x
    return half + half * jnp.tanh(half)


def _norm_to_bf16(x, g):
    ms = jnp.mean(x * x, axis=-1, keepdims=True)
    return (x * lax.rsqrt(ms + EPS) * g).astype(BF16)


U_Q, U_K, U_V, U_GA, U_XB, U_GB = range(6)
U32_K, U32_V, U32_XB = U_K // 2, U_V // 2, U_XB // 2


def _norm_matmul_kernel(x_ref, g_ref, w_ref, o16_ref, o32_ref, xn_ref):
    j = pl.program_id(1)

    @pl.when(j == 0)
    def _():
        xn_ref[...] = _norm_to_bf16(x_ref[...], g_ref[...])

    acc = jnp.dot(xn_ref[...], w_ref[...], preferred_element_type=F32)
    o16_ref[...] = acc.astype(BF16)

    @pl.when((j == U_K) | (j == U_V) | (j == U_XB))
    def _():
        o32_ref[...] = acc


def _norm_matmul(x, g, w, *, tm):
    m, d = x.shape
    assert D_A == D_B and w.shape[1] == IN_AB
    return pl.pallas_call(
        _norm_matmul_kernel,
        grid=(m // tm, IN_AB // D_A),
        in_specs=[
            pl.BlockSpec((tm, d), lambda i, j: (i, 0)),
            pl.BlockSpec((1, d), lambda i, j: (0, 0)),
            pl.BlockSpec((d, D_A), lambda i, j: (0, j)),
        ],
        out_specs=[pl.BlockSpec((tm, D_A), lambda i, j: (i, j)), pl.BlockSpec((tm, D_A), lambda i, j: (i, j // 2))],
        out_shape=[jax.ShapeDtypeStruct((m, IN_AB), BF16), jax.ShapeDtypeStruct((m, 3 * D_A), F32)],
        scratch_shapes=[pltpu.VMEM((tm, d), BF16)],
        compiler_params=_cparams(("parallel", "arbitrary")),
        name="norm_in_proj",
    )(x, g, w)


def _dwconv_rows(zext_ref, base, rows, lane0, w_ref, b_ref):
    lanes = slice(lane0, lane0 + LANES)
    acc = jnp.broadcast_to(b_ref[...], (rows, LANES))
    for phase in range(SUBLANES):
        win_rows = rows if phase == 0 else rows + SUBLANES
        part = None
        for k in range(CONV_C):
            off = base + CONV_HALO - (CONV_C - 1) + k
            if off % SUBLANES != phase:
                continue
            tap = w_ref[k]
            win = zext_ref[off - phase:off - phase + win_rows, lanes]
            term = win.reshape(win_rows // SUBLANES, SUBLANES, LANES) * tap[None]
            part = term if part is None else part + term
        if part is None:
            continue
        part = part.reshape(win_rows, LANES)
        if phase:
            part = pltpu.roll(part, win_rows - phase, 0)[:rows]
        acc = acc + part
    return acc


def _norm_glu_conv_kernel(x_ref, g_ref, wv_ref, wg_ref, wt_ref, st_ref, cw_ref, cb_ref,
                          cv_ref, sg_ref, tail_ref, xn_ref, zext_ref, carry_ref, *, seg, tiles_per_batch):
    i = pl.program_id(0)
    j = pl.program_id(1)
    n_seg = x_ref.shape[0] // seg
    ext = CONV_HALO + seg

    @pl.when(j == 0)
    def _():
        xn_ref[...] = _norm_to_bf16(x_ref[...], g_ref[...])

    xn = xn_ref[...]
    gate = jnp.dot(xn, wt_ref[...], preferred_element_type=F32)
    sg_ref[...] = _silu(gate).astype(BF16)
    val = jnp.dot(xn, wv_ref[...], preferred_element_type=F32)
    glu = jnp.dot(xn, wg_ref[...], preferred_element_type=F32)
    z = val * _sigmoid(glu)

    carried = tiles_per_batch > 1
    if carried:
        @pl.when(i % tiles_per_batch == 0)
        def _():
            carry_ref[j] = st_ref[0]

    rows = min(seg, CONV_ROWS)
    for s in range(n_seg):
        base = s * ext
        zext_ref[base:base + CONV_HALO, :] = carry_ref[j] if carried else st_ref[s]
        zext_ref[base + CONV_HALO:base + ext, :] = z[s * seg:(s + 1) * seg]
        for c in range(cw_ref.shape[0]):
            for r0 in range(0, seg, rows):
                out_rows = slice(s * seg + r0, s * seg + r0 + rows)
                cv_ref[out_rows, c * LANES:(c + 1) * LANES] = _dwconv_rows(
                    zext_ref, base + r0, rows, c * LANES, cw_ref.at[c], cb_ref.at[c])
        tail = zext_ref[base + seg:base + ext, :]
        tail_ref[s] = tail
        if carried:
            carry_ref[j] = tail


def _norm_glu_conv(x, g, w, state, dw_w, dw_b, *, t, tm, tn):
    m, d = x.shape
    nb = D_C // tn
    seg = min(t, tm)
    assert tm % seg == 0 and t % seg == 0 and seg % min(seg, CONV_ROWS) == 0 and seg % SUBLANES == 0
    tiles_per_batch = t // seg
    dw_w = dw_w.reshape(CONV_C, D_C // LANES, 1, LANES).transpose(1, 0, 2, 3)
    dw_w = jnp.broadcast_to(dw_w, (D_C // LANES, CONV_C, SUBLANES, LANES))
    dw_b = dw_b.reshape(D_C // LANES, 1, LANES)
    bpt = tm // seg if tiles_per_batch == 1 else 1
    batch_blk = (lambda i: i // tiles_per_batch) if tiles_per_batch > 1 else (lambda i: i)
    state_spec = pl.BlockSpec((bpt, CONV_HALO, tn), lambda i, j: (batch_blk(i), 0, j))
    body = functools.partial(_norm_glu_conv_kernel, seg=seg, tiles_per_batch=tiles_per_batch)
    return pl.pallas_call(
        body,
        grid=(m // tm, nb),
        in_specs=[
            pl.BlockSpec((tm, d), lambda i, j: (i, 0)),
            pl.BlockSpec((1, d), lambda i, j: (0, 0)),
            pl.BlockSpec((d, tn), lambda i, j: (0, j)),
            pl.BlockSpec((d, tn), lambda i, j: (0, j + nb)),
            pl.BlockSpec((d, tn), lambda i, j: (0, j + 2 * nb)),
            state_spec,
            pl.BlockSpec((tn // LANES, CONV_C, SUBLANES, LANES), lambda i, j: (j, 0, 0, 0)),
            pl.BlockSpec((tn // LANES, 1, LANES), lambda i, j: (j, 0, 0)),
        ],
        out_specs=[
            pl.BlockSpec((tm, tn), lambda i, j: (i, j)),
            pl.BlockSpec((tm, tn), lambda i, j: (i, j)),
            pl.BlockSpec((tm // seg, CONV_HALO, tn), lambda i, j: (i, 0, j)),
        ],
        out_shape=[
            jax.ShapeDtypeStruct((m, D_C), F32),
            jax.ShapeDtypeStruct((m, D_C), BF16),
            jax.ShapeDtypeStruct((m // seg, CONV_HALO, D_C), F32),
        ],
        scratch_shapes=[
            pltpu.VMEM((tm, d), BF16),
            pltpu.VMEM(((tm // seg) * (CONV_HALO + seg), tn), F32),
            pltpu.VMEM((nb, CONV_HALO, tn), F32),
        ],
        compiler_params=_cparams(("arbitrary", "arbitrary")),
        name="norm_in_proj_glu_conv",
    )(x, g, w, w, w, state, dw_w, dw_b)


def _attend_heads(q_ref, kcat_ref, vcat_ref, bias_ref, o_ref, q_rows, k_rows, kpos0):
    heads = [slice(h * HEAD_DIM, (h + 1) * HEAD_DIM) for h in range(N_HEADS)]
    scores = []
    for cols in heads:
        q = q_ref[q_rows, cols].astype(BF16)
        scores.append(lax.dot_general(q, kcat_ref[k_rows, cols], (((1,), (1,)), ((), ())),
                                      preferred_element_type=F32))
    if kpos0 is not None:
        col = lax.broadcasted_iota(jnp.int32, (1, scores[0].shape[1]), 1)
        before_start = jnp.where(col + kpos0 >= 0, 0.0, -jnp.inf)
    probs = []
    for h, s in enumerate(scores):
        s = s * (HEAD_DIM ** -0.5) + bias_ref[h]
        if kpos0 is not None:
            s = s + before_start
        m = jnp.max(s, axis=-1, keepdims=True)
        p = jnp.exp(s - m)
        probs.append((p.astype(BF16), jnp.sum(p, axis=-1, keepdims=True)))
    for cols, (p, l) in zip(heads, probs):
        o = jnp.dot(p, vcat_ref[k_rows, cols], preferred_element_type=F32) / l
        o_ref[q_rows, cols] = o.astype(o_ref.dtype)


def _heads_to_rows(x_ref):
    heads = [x_ref[:, h * HEAD_DIM:(h + 1) * HEAD_DIM] for h in range(N_HEADS)]
    return jnp.swapaxes(jnp.stack(heads, axis=0), 0, 1)


def _expand_rel_bias(fvec_ref, bias_ref, n_valid, chunk_masks):
    tq = bias_ref.shape[1]
    i = lax.broadcasted_iota(jnp.int32, (tq, K_WIN), 0)
    j = lax.broadcasted_iota(jnp.int32, (tq, K_WIN), 1)
    if chunk_masks:
        lo = jnp.where(i < CHUNK, 0, CHUNK)
        hi = jnp.where(i < CHUNK, min(BAND_PAST + CHUNK, n_valid), n_valid)
        hidden = (j < lo) | (j >= hi)
    else:
        hidden = j >= n_valid
    for h in range(N_HEADS):
        rows = jnp.broadcast_to(fvec_ref[h:h + 1, :], (tq, Q_TILE + K_WIN))
        skew = pltpu.roll(rows, 0, 1, stride=1, stride_axis=0)
        bias_ref[h] = jnp.where(hidden, -jnp.inf, skew[:, Q_TILE:])


def _attn_prompt_kernel(q_ref, kp_ref, kc_ref, vp_ref, vc_ref, kf_ref, vf_ref, fvec_ref, o_ref, nk_ref, nv_ref,
                        kcat_ref, vcat_ref, bias_ref):
    b = pl.program_id(0)
    t = pl.program_id(1)

    @pl.when((b == 0) & (t == 0))
    def _():
        _expand_rel_bias(fvec_ref, bias_ref, K_WIN, True)

    @pl.when(t == pl.num_programs(1) - 1)
    def _():
        nk_ref[...] = _heads_to_rows(kf_ref)
        nv_ref[...] = _heads_to_rows(vf_ref)

    kcat_ref[0:ATT_BLOCK, :] = kp_ref[...]
    kcat_ref[ATT_BLOCK:, :] = kc_ref[...]
    vcat_ref[0:ATT_BLOCK, :] = vp_ref[...]
    vcat_ref[ATT_BLOCK:, :] = vc_ref[...]
    for r0 in range(0, ATT_BLOCK, Q_TILE):
        kpos0 = t * ATT_BLOCK + r0 - BAND_PAST
        _attend_heads(q_ref, kcat_ref, vcat_ref, bias_ref, o_ref,
                      slice(r0, r0 + Q_TILE), slice(r0, r0 + K_WIN), kpos0)


def _attn_prompt(u16, u32, fvec):
    b, s, _ = u16.shape
    blk = (None, ATT_BLOCK, D_A)
    last = s // ATT_BLOCK - 1
    prev = lambda col: (lambda bi, t: (bi, jnp.maximum(t - 1, 0), col))
    cur = lambda col: (lambda bi, t: (bi, t, col))
    newest = lambda col: (lambda bi, t: (bi, last, col))
    state = pl.BlockSpec((None, ATT_BLOCK, N_HEADS, HEAD_DIM), lambda bi, t: (bi, 0, 0, 0))
    state_shape = jax.ShapeDtypeStruct((b, ATT_BLOCK, N_HEADS, HEAD_DIM), F32)
    return pl.pallas_call(
        _attn_prompt_kernel,
        grid=(b, s // ATT_BLOCK),
        in_specs=[
            pl.BlockSpec(blk, cur(U_Q)),
            pl.BlockSpec(blk, prev(U_K)),
            pl.BlockSpec(blk, cur(U_K)),
            pl.BlockSpec(blk, prev(U_V)),
            pl.BlockSpec(blk, cur(U_V)),
            pl.BlockSpec(blk, newest(U32_K)),
            pl.BlockSpec(blk, newest(U32_V)),
            pl.BlockSpec(fvec.shape, lambda bi, t: (0, 0)),
        ],
        out_specs=[pl.BlockSpec(blk, cur(0)), state, state],
        out_shape=[jax.ShapeDtypeStruct((b, s, D_A), BF16), state_shape, state_shape],
        scratch_shapes=[
            pltpu.VMEM((2 * ATT_BLOCK, D_A), BF16),
            pltpu.VMEM((2 * ATT_BLOCK, D_A), BF16),
            pltpu.VMEM((N_HEADS, Q_TILE, K_WIN), F32),
        ],
        compiler_params=_cparams(("arbitrary", "arbitrary")),
        name="attn_prompt",
    )(u16, u16, u16, u16, u16, u32, u32, fvec)


def _attn_sample_kernel(q_ref, kn_ref, vn_ref, kc_ref, vc_ref, fvec_ref, o_ref, nk_ref, nv_ref,
                        kcat_ref, vcat_ref, bias_ref):
    n_streams, tq, _ = q_ref.shape
    lc = kc_ref.shape[1]

    @pl.when(pl.program_id(0) == 0)
    def _():
        _expand_rel_bias(fvec_ref, bias_ref, lc + tq, False)
        kcat_ref[lc + tq:, :] = jnp.zeros((K_WIN - lc - tq, D_A), BF16)
        vcat_ref[lc + tq:, :] = jnp.zeros((K_WIN - lc - tq, D_A), BF16)

    for s in range(n_streams):
        kcat_ref[lc:lc + tq, :] = kn_ref[s].astype(BF16)
        vcat_ref[lc:lc + tq, :] = vn_ref[s].astype(BF16)
        nk_ref[s] = _heads_to_rows(kn_ref.at[s])
        nv_ref[s] = _heads_to_rows(vn_ref.at[s])
        kch = jnp.swapaxes(kc_ref[s], 0, 1)
        vch = jnp.swapaxes(vc_ref[s], 0, 1)
        for h in range(N_HEADS):
            c0 = h * HEAD_DIM
            kcat_ref[0:lc, c0:c0 + HEAD_DIM] = kch[h].astype(BF16)
            vcat_ref[0:lc, c0:c0 + HEAD_DIM] = vch[h].astype(BF16)
        _attend_heads(q_ref.at[s], kcat_ref, vcat_ref, bias_ref, o_ref.at[s], slice(None), slice(None), None)


def _attn_sample(u16, u32, k_cache, v_cache, fvec):
    b, t, _ = u16.shape
    lc = k_cache.shape[1]
    assert lc + t <= K_WIN and t <= Q_TILE
    ns = SAMPLE_STREAMS_PER_STEP if b % SAMPLE_STREAMS_PER_STEP == 0 else 1
    new = lambda col: pl.BlockSpec((ns, t, D_A), lambda bi: (bi, 0, col))
    cache = pl.BlockSpec((ns, lc, N_HEADS, HEAD_DIM), lambda bi: (bi, 0, 0, 0))
    state = pl.BlockSpec((ns, t, N_HEADS, HEAD_DIM), lambda bi: (bi, 0, 0, 0))
    state_shape = jax.ShapeDtypeStruct((b, t, N_HEADS, HEAD_DIM), F32)
    return pl.pallas_call(
        _attn_sample_kernel,
        grid=(b // ns,),
        in_specs=[new(U_Q), new(U32_K), new(U32_V), cache, cache, pl.BlockSpec(fvec.shape, lambda bi: (0, 0))],
        out_specs=[pl.BlockSpec((ns, t, D_A), lambda bi: (bi, 0, 0)), state, state],
        out_shape=[jax.ShapeDtypeStruct((b, t, D_A), BF16), state_shape, state_shape],
        scratch_shapes=[
            pltpu.VMEM((K_WIN, D_A), BF16),
            pltpu.VMEM((K_WIN, D_A), BF16),
            pltpu.VMEM((N_HEADS, t, K_WIN), F32),
        ],
        compiler_params=_cparams(("arbitrary",)),
        name="attn_sample",
    )(u16, u32, u32, k_cache, v_cache, fvec)


def _rel_bias_vector(rel_table):
    width = Q_TILE + K_WIN
    n_const = BAND_PAST + Q_TILE - REL_CLIP + 1
    n_rev = width - n_const
    assert 0 < n_rev <= 2 * REL_CLIP
    const = jnp.broadcast_to(rel_table[:, 2 * REL_CLIP:], (N_HEADS, n_const))
    rev = lax.rev(rel_table[:, 2 * REL_CLIP - n_rev:2 * REL_CLIP], (1,))
    return jnp.concatenate([const, rev], axis=1).astype(F32)


def _log_sigmoid(x):
    return -(jnp.maximum(-x, 0.0) + jnp.log1p(jnp.exp(-jnp.abs(x))))


def _shift_rows(x, d, fill, block):
    n, c = x.shape
    if d % SUBLANES == 0:
        x3 = x.reshape(n // block, block, c)
        pad = jnp.full((n // block, d, c), fill, x.dtype)
        return jnp.concatenate([pad, x3[:, :block - d]], axis=1).reshape(n, c)
    rolled = pltpu.roll(x, d, 0)
    row = lax.broadcasted_iota(jnp.int32, x.shape, 0)
    return jnp.where(row % block < d, fill, rolled)


def _rglru_kernel(xb_ref, lb_ref, h0_ref, cw_ref, cb_ref, wa_ref, ba_ref, wx_ref, bx_ref, lam_ref,
                  h_ref, hl_ref, xext_ref, hc_ref):
    t = pl.program_id(1)
    tt = xb_ref.shape[0]

    @pl.when(t == 0)
    def _():
        xext_ref[0:LRU_HALO, :] = lb_ref[...]
        hc_ref[...] = h0_ref[...]

    xext_ref[LRU_HALO:, :] = xb_ref[...]
    xc = cb_ref[...] + cw_ref[CONV_B - 1:CONV_B, :] * xb_ref[...]
    win = xext_ref[...]
    for k in range(CONV_B - 1):
        off = LRU_HALO - (CONV_B - 1) + k
        xc = xc + pltpu.roll(cw_ref[k:k + 1, :] * win, tt + LRU_HALO - off, 0)[:tt]
    xext_ref[0:LRU_HALO, :] = xext_ref[tt:tt + LRU_HALO, :]

    xcb = xc.astype(BF16)
    ra, rx = [], []
    for n in range(N_BLOCKS_B):
        blk = xcb[:, n * BLOCK_B:(n + 1) * BLOCK_B]
        ra.append(jnp.dot(blk, wa_ref[n], preferred_element_type=F32))
        rx.append(jnp.dot(blk, wx_ref[n], preferred_element_type=F32))
    r = _sigmoid(jnp.concatenate(ra, axis=-1) + ba_ref[...])
    i = _sigmoid(jnp.concatenate(rx, axis=-1) + bx_ref[...])
    log_a = LRU_C * r * _log_sigmoid(lam_ref[...])
    a = jnp.exp(log_a)
    th = jnp.tanh(log_a)
    bt = jnp.sqrt(-2.0 * th / (1.0 - th)) * (i * xc)

    block = min(tt, LRU_SCAN_ROWS)
    d = 1
    while d < block:
        bt = bt + a * _shift_rows(bt, d, 0.0, block)
        a = a * _shift_rows(a, d, 1.0, block)
        d *= 2
    carry = hc_ref[...]
    for r0 in range(0, tt, block):
        h = a[r0:r0 + block] * carry + bt[r0:r0 + block]
        h_ref[r0:r0 + block, :] = h.astype(h_ref.dtype)
        carry = h[block - 1:block, :]
    hc_ref[...] = carry
    hl_ref[...] = carry


def _rglru(u32, lb0, h0, cw, cb, w_a, b_a, w_x, b_x, lam, *, tt):
    b, t, _ = u32.shape
    row = lambda a: a.reshape(1, D_B)
    full = lambda a: pl.BlockSpec(a.shape, lambda bi, ti: (0,) * a.ndim)
    args = (cw, row(cb), w_a, row(b_a), w_x, row(b_x), row(lam))
    return pl.pallas_call(
        _rglru_kernel,
        grid=(b, t // tt),
        in_specs=[
            pl.BlockSpec((None, tt, D_B), lambda bi, ti: (bi, ti, U32_XB)),
            pl.BlockSpec((None, LRU_HALO, D_B), lambda bi, ti: (bi, 0, 0)),
            pl.BlockSpec((None, 1, D_B), lambda bi, ti: (bi, 0, 0)),
        ] + [full(a) for a in args],
        out_specs=[
            pl.BlockSpec((None, tt, D_B), lambda bi, ti: (bi, ti, 0)),
            pl.BlockSpec((None, 1, D_B), lambda bi, ti: (bi, 0, 0)),
        ],
        out_shape=[jax.ShapeDtypeStruct((b, t, D_B), BF16), jax.ShapeDtypeStruct((b, 1, D_B), F32)],
        scratch_shapes=[pltpu.VMEM((LRU_HALO + tt, D_B), F32), pltpu.VMEM((1, D_B), F32)],
        compiler_params=_cparams(("parallel", "arbitrary")),
        name="rglru",
    )(u32, lb0, h0, *args)


def _gated_out_kernel(oa_ref, ga_ref, h_ref, gb_ref, x_ref, w_ref, o_ref):
    ma = (_silu(ga_ref[...].astype(F32)) * oa_ref[...].astype(F32)).astype(BF16)
    mb = (_silu(gb_ref[...].astype(F32)) * h_ref[...].astype(F32)).astype(BF16)
    acc = jnp.dot(ma, w_ref[0:D_A, :], preferred_element_type=F32)
    acc = acc + jnp.dot(mb, w_ref[D_A:, :], preferred_element_type=F32)
    o_ref[...] = x_ref[...] + acc


def _gated_out(o_a, u16, h, x, w, *, tm):
    m = x.shape[0]
    return pl.pallas_call(
        _gated_out_kernel,
        grid=(m // tm,),
        in_specs=[
            pl.BlockSpec((tm, D_A), lambda i: (i, 0)),
            pl.BlockSpec((tm, D_A), lambda i: (i, U_GA)),
            pl.BlockSpec((tm, D_B), lambda i: (i, 0)),
            pl.BlockSpec((tm, D_B), lambda i: (i, U_GB)),
            pl.BlockSpec((tm, D_MODEL), lambda i: (i, 0)),
            pl.BlockSpec(w.shape, lambda i: (0, 0), pipeline_mode=pl.Buffered(1)),
        ],
        out_specs=pl.BlockSpec((tm, D_MODEL), lambda i: (i, 0)),
        out_shape=jax.ShapeDtypeStruct((m, D_MODEL), F32),
        compiler_params=_cparams(("parallel",)),
        name="gated_out_proj",
    )(o_a, u16, h, u16, x, w)


def _ln_out_norm_kernel(cv_ref, sg_ref, lg_ref, lb_ref, w_ref, x_ref, g_ref, o_ref):
    acc = cv_ref[...]
    mu = jnp.mean(acc, axis=-1, keepdims=True)
    cen = acc - mu
    var = jnp.mean(cen * cen, axis=-1, keepdims=True)
    yn = cen * lax.rsqrt(var + EPS) * lg_ref[...] + lb_ref[...]
    y = (_silu(yn) * sg_ref[...].astype(F32)).astype(BF16)
    x = x_ref[...] + jnp.dot(y, w_ref[...], preferred_element_type=F32)
    ms = jnp.mean(x * x, axis=-1, keepdims=True)
    o_ref[...] = x * lax.rsqrt(ms + EPS) * g_ref[...]


def _ln_out_norm(cv, sg, ln_g, ln_b, w, x, g, *, tm):
    m = x.shape[0]
    row = pl.BlockSpec((1, D_C), lambda i: (0, 0))
    return pl.pallas_call(
        _ln_out_norm_kernel,
        grid=(m // tm,),
        in_specs=[
            pl.BlockSpec((tm, D_C), lambda i: (i, 0)),
            pl.BlockSpec((tm, D_C), lambda i: (i, 0)),
            row,
            row,
            pl.BlockSpec(w.shape, lambda i: (0, 0), pipeline_mode=pl.Buffered(1)),
            pl.BlockSpec((tm, D_MODEL), lambda i: (i, 0)),
            pl.BlockSpec((1, D_MODEL), lambda i: (0, 0)),
        ],
        out_specs=pl.BlockSpec((tm, D_MODEL), lambda i: (i, 0)),
        out_shape=jax.ShapeDtypeStruct((m, D_MODEL), F32),
        compiler_params=_cparams(("parallel",)),
        name="ln_out_proj_final_norm",
    )(cv, sg, ln_g.reshape(1, D_C), ln_b.reshape(1, D_C), w, x, g)


def _front_pad_rows(a, rows):
    return jnp.pad(a, ((0, 0), (rows - a.shape[1], 0), (0, 0)))


def _trunk(x, caches, w, *, tm, tm_out, lru_tt):
    b, t, _ = x.shape
    m = b * t
    x2d = x.reshape(m, D_MODEL)

    u16, u32 = _norm_matmul(x2d, w["norm_ab"], w["w_in_ab"], tm=tm)
    u16_3 = u16.reshape(b, t, IN_AB)
    u32_3 = u32.reshape(b, t, 3 * D_A)
    if caches is None:
        assert t % ATT_BLOCK == 0 and ATT_BLOCK == BAND_PAST
        o_a, new_k, new_v = _attn_prompt(u16_3, u32_3, w["rel_bias_vec"])
        h0 = jnp.zeros((b, 1, D_B), F32)
        lb0 = jnp.zeros((b, LRU_HALO, D_B), F32)
        cb0 = jnp.zeros((b, CONV_HALO, D_C), F32)
    else:
        kc, vc, hc, lbc, cbc = caches
        o_a, new_k, new_v = _attn_sample(u16_3, u32_3, kc, vc, w["rel_bias_vec"])
        h0 = hc.reshape(b, 1, D_B)
        lb0 = _front_pad_rows(lbc, LRU_HALO)
        cb0 = _front_pad_rows(cbc, CONV_HALO)
    h, h_last = _rglru(u32_3, lb0, h0, w["lru_conv_w"], w["lru_conv_b"], w["lru_w_a"], w["lru_b_a"],
                       w["lru_w_x"], w["lru_b_x"], w["lru_lambda"], tt=lru_tt)
    x1 = _gated_out(o_a.reshape(m, D_A), u16, h.reshape(m, D_B), x2d, w["w_out_ab"], tm=tm_out)

    new_h = h_last[:, 0]
    new_lb = u32_3[:, t - (CONV_B - 1):, U32_XB * D_B:(U32_XB + 1) * D_B]

    cv, sg, z_tail = _norm_glu_conv(x1, w["norm_cv"], w["w_in_cv"], cb0, w["dw_w"], w["dw_b"], t=t, tm=tm, tn=512)
    out = _ln_out_norm(cv, sg, w["ln_g"], w["ln_b"], w["w_out_cv"], x1, w["final_norm"], tm=tm_out)
    segs_per_batch = z_tail.shape[0] // b
    new_cb = z_tail[segs_per_batch - 1::segs_per_batch, CONV_HALO - (CONV_C - 1):]

    return (out.reshape(b, t, D_MODEL), new_k[None], new_v[None], new_h[None], new_lb[None], new_cb[None])


def kernel(x_prompt, x_sample, cache_attn_k, cache_attn_v, state_lru_h, state_lru_conv, state_conv, norm_ab, w_in_ab, w_out_ab, rel_bias, lru_conv_w, lru_conv_b, lru_w_a, lru_b_a, lru_w_x, lru_b_x, lru_lambda, norm_cv, w_in_cv, w_out_cv, dw_w, dw_b, ln_g, ln_b, final_norm):
    assert norm_ab.shape[0] == 1 and norm_cv.shape[0] == 1, "one even and one odd layer"
    t_s = x_sample.shape[1]
    w = {
        "norm_ab": norm_ab[0].reshape(1, D_MODEL),
        "w_in_ab": w_in_ab[0].astype(BF16),
        "w_out_ab": w_out_ab[0].astype(BF16),
        "rel_bias_vec": _rel_bias_vector(rel_bias[0]),
        "lru_conv_w": lru_conv_w[0],
        "lru_conv_b": lru_conv_b[0],
        "lru_w_a": lru_w_a[0].astype(BF16),
        "lru_b_a": lru_b_a[0],
        "lru_w_x": lru_w_x[0].astype(BF16),
        "lru_b_x": lru_b_x[0],
        "lru_lambda": lru_lambda[0],
        "norm_cv": norm_cv[0].reshape(1, D_MODEL),
        "w_in_cv": w_in_cv[0].astype(BF16),
        "w_out_cv": w_out_cv[0].astype(BF16),
        "dw_w": dw_w[0],
        "dw_b": dw_b[0],
        "ln_g": ln_g[0],
        "ln_b": ln_b[0],
        "final_norm": final_norm.reshape(1, D_MODEL),
    }
    y_p, k_p, v_p, h_p, lb_p, cb_p = _trunk(x_prompt, None, w, tm=1024, tm_out=512, lru_tt=512)
    caches = (cache_attn_k[0], cache_attn_v[0], state_lru_h[0], state_lru_conv[0], state_conv[0])
    y_s, k_s, v_s, h_s, lb_s, cb_s = _trunk(x_sample, caches, w, tm=512, tm_out=256, lru_tt=t_s)
    return (y_p, y_s, k_p, v_p, h_p, lb_p, cb_p, k_s, v_s, h_s, lb_s, cb_s)
```

```python
import functools

import jax
import jax.numpy as jnp
from jax import lax
from jax.experimental import pallas as pl
from jax.experimental.pallas import tpu as pltpu

D_MODEL = 2048
CHUNK = 64
LEFT_CHUNKS = 8
BAND_PAST = CHUNK * LEFT_CHUNKS
N_HEADS = 8
HEAD_DIM = 128
D_A = N_HEADS * HEAD_DIM
REL_CLIP = 128
D_B = 1024
N_BLOCKS_B = 8
BLOCK_B = D_B // N_BLOCKS_B
CONV_B = 4
LRU_C = 8.0
D_C = D_MODEL
CONV_C = 31
IN_AB = 4 * D_A + 2 * D_B
IN_CV = 3 * D_C
EPS = 1e-6

F32 = jnp.float32
BF16 = jnp.bfloat16

SUBLANES = 8
LANES = 128
Q_TILE = 2 * CHUNK
K_WIN = BAND_PAST + Q_TILE
ATT_BLOCK = 512
SAMPLE_STREAMS_PER_STEP = 2
CONV_HALO = 32
CONV_ROWS = 128
LRU_SCAN_ROWS = 32
LRU_HALO = 8
VMEM_LIMIT = 56 * 1024 * 1024


def _cparams(sem):
    return pltpu.CompilerParams(dimension_semantics=sem, vmem_limit_bytes=VMEM_LIMIT)


def _sigmoid(x):
    return 0.5 * jnp.tanh(0.5 * x) + 0.5


def _silu(x):
    half = 0.5 * x
    return half + half * jnp.tanh(half)


def _norm_to_bf16(x, g):
    ms = jnp.mean(x * x, axis=-1, keepdims=True)
    return (x * lax.rsqrt(ms + EPS) * g).astype(BF16)


U_Q, U_K, U_V, U_GA, U_XB, U_GB = range(6)


def _norm_matmul_kernel(x_ref, g_ref, w_ref, o16_ref, xb_ref, ks_ref, vs_ref, xn_ref, *, tiles_per_batch):
    i = pl.program_id(0)
    j = pl.program_id(1)
    tm = x_ref.shape[0]
    streams, keep, _ = ks_ref.shape

    @pl.when(j == 0)
    def _():
        xn_ref[...] = _norm_to_bf16(x_ref[...], g_ref[...])

    acc = jnp.dot(xn_ref[...], w_ref[...], preferred_element_type=F32)
    o16_ref[...] = acc.astype(BF16)

    @pl.when(j == U_XB)
    def _():
        xb_ref[...] = acc

    newest = acc[tm - streams * keep:].reshape(streams, keep, D_A)
    holds_state = (i % tiles_per_batch) == tiles_per_batch - 1

    @pl.when(holds_state & (j == U_K))
    def _():
        ks_ref[...] = newest

    @pl.when(holds_state & (j == U_V))
    def _():
        vs_ref[...] = newest


def _norm_matmul(x, g, w, *, t, tm):
    m, d = x.shape
    assert D_A == D_B and w.shape[1] == IN_AB
    keep = min(BAND_PAST, t)
    seg = min(t, tm)
    assert tm % seg == 0 and t % seg == 0 and keep <= seg and keep % SUBLANES == 0
    tiles_per_batch = t // seg
    streams = tm // seg
    assert streams == 1 or keep == seg
    state_spec = pl.BlockSpec((streams, keep, D_A), lambda i, j: (i // tiles_per_batch, 0, 0))
    state_shape = jax.ShapeDtypeStruct((m // t, keep, D_A), F32)
    return pl.pallas_call(
        functools.partial(_norm_matmul_kernel, tiles_per_batch=tiles_per_batch),
        grid=(m // tm, IN_AB // D_A),
        in_specs=[
            pl.BlockSpec((tm, d), lambda i, j: (i, 0)),
            pl.BlockSpec((1, d), lambda i, j: (0, 0)),
            pl.BlockSpec((d, D_A), lambda i, j: (0, j)),
        ],
        out_specs=[
            pl.BlockSpec((tm, D_A), lambda i, j: (i, j)),
            pl.BlockSpec((tm, D_B), lambda i, j: (i, 0)),
            state_spec,
            state_spec,
        ],
        out_shape=[jax.ShapeDtypeStruct((m, IN_AB), BF16), jax.ShapeDtypeStruct((m, D_B), F32), state_shape, state_shape],
        scratch_shapes=[pltpu.VMEM((tm, d), BF16)],
        compiler_params=_cparams(("arbitrary", "arbitrary")),
        name="norm_in_proj",
    )(x, g, w)


def _dwconv_rows(zext_ref, base, rows, lane0, w_ref, b_ref):
    lanes = slice(lane0, lane0 + LANES)
    acc = jnp.broadcast_to(b_ref[...], (rows, LANES))
    for phase in range(SUBLANES):
        win_rows = rows if phase == 0 else rows + SUBLANES
        part = None
        for k in range(CONV_C):
            off = base + CONV_HALO - (CONV_C - 1) + k
            if off % SUBLANES != phase:
                continue
            tap = w_ref[k]
            win = zext_ref[off - phase:off - phase + win_rows, lanes]
            term = win.reshape(win_rows // SUBLANES, SUBLANES, LANES) * tap[None]
            part = term if part is None else part + term
        if part is None:
            continue
        part = part.reshape(win_rows, LANES)
        if phase:
            part = pltpu.roll(part, win_rows - phase, 0)[:rows]
        acc = acc + part
    return acc


def _norm_glu_conv_kernel(x_ref, g_ref, wv_ref, wg_ref, wt_ref, st_ref, cw_ref, cb_ref,
                          cv_ref, sg_ref, tail_ref, xn_ref, zext_ref, carry_ref, *, seg, tiles_per_batch):
    i = pl.program_id(0)
    j = pl.program_id(1)
    n_seg = x_ref.shape[0] // seg
    ext = CONV_HALO + seg

    @pl.when(j == 0)
    def _():
        xn_ref[...] = _norm_to_bf16(x_ref[...], g_ref[...])

    xn = xn_ref[...]
    gate = jnp.dot(xn, wt_ref[...], preferred_element_type=F32)
    sg_ref[...] = _silu(gate).astype(BF16)
    val = jnp.dot(xn, wv_ref[...], preferred_element_type=F32)
    glu = jnp.dot(xn, wg_ref[...], preferred_element_type=F32)
    z = val * _sigmoid(glu)

    carried = tiles_per_batch > 1
    if carried:
        @pl.when(i % tiles_per_batch == 0)
        def _():
            carry_ref[j] = st_ref[0]

    rows = min(seg, CONV_ROWS)
    for s in range(n_seg):
        base = s * ext
        zext_ref[base:base + CONV_HALO, :] = carry_ref[j] if carried else st_ref[s]
        zext_ref[base + CONV_HALO:base + ext, :] = z[s * seg:(s + 1) * seg]
        for c in range(cw_ref.shape[0]):
            for r0 in range(0, seg, rows):
                out_rows = slice(s * seg + r0, s * seg + r0 + rows)
                cv_ref[out_rows, c * LANES:(c + 1) * LANES] = _dwconv_rows(
                    zext_ref, base + r0, rows, c * LANES, cw_ref.at[c], cb_ref.at[c])
        tail = zext_ref[base + seg:base + ext, :]
        tail_ref[s] = tail
        if carried:
            carry_ref[j] = tail


def _norm_glu_conv(x, g, w, state, dw_w, dw_b, *, t, tm, tn):
    m, d = x.shape
    nb = D_C // tn
    seg = min(t, tm)
    assert tm % seg == 0 and t % seg == 0 and seg % min(seg, CONV_ROWS) == 0 and seg % SUBLANES == 0
    tiles_per_batch = t // seg
    dw_w = dw_w.reshape(CONV_C, D_C // LANES, 1, LANES).transpose(1, 0, 2, 3)
    dw_w = jnp.broadcast_to(dw_w, (D_C // LANES, CONV_C, SUBLANES, LANES))
    dw_b = dw_b.reshape(D_C // LANES, 1, LANES)
    bpt = tm // seg if tiles_per_batch == 1 else 1
    batch_blk = (lambda i: i // tiles_per_batch) if tiles_per_batch > 1 else (lambda i: i)
    state_spec = pl.BlockSpec((bpt, CONV_HALO, tn), lambda i, j: (batch_blk(i), 0, j))
    body = functools.partial(_norm_glu_conv_kernel, seg=seg, tiles_per_batch=tiles_per_batch)
    return pl.pallas_call(
        body,
        grid=(m // tm, nb),
        in_specs=[
            pl.BlockSpec((tm, d), lambda i, j: (i, 0)),
            pl.BlockSpec((1, d), lambda i, j: (0, 0)),
            pl.BlockSpec((d, tn), lambda i, j: (0, j)),
            pl.BlockSpec((d, tn), lambda i, j: (0, j + nb)),
            pl.BlockSpec((d, tn), lambda i, j: (0, j + 2 * nb)),
            state_spec,
            pl.BlockSpec((tn // LANES, CONV_C, SUBLANES, LANES), lambda i, j: (j, 0, 0, 0)),
            pl.BlockSpec((tn // LANES, 1, LANES), lambda i, j: (j, 0, 0)),
        ],
        out_specs=[
            pl.BlockSpec((tm, tn), lambda i, j: (i, j)),
            pl.BlockSpec((tm, tn), lambda i, j: (i, j)),
            pl.BlockSpec((tm // seg, CONV_HALO, tn), lambda i, j: (i, 0, j)),
        ],
        out_shape=[
            jax.ShapeDtypeStruct((m, D_C), F32),
            jax.ShapeDtypeStruct((m, D_C), BF16),
            jax.ShapeDtypeStruct((m // seg, CONV_HALO, D_C), F32),
        ],
        scratch_shapes=[
            pltpu.VMEM((tm, d), BF16),
            pltpu.VMEM(((tm // seg) * (CONV_HALO + seg), tn), F32),
            pltpu.VMEM((nb, CONV_HALO, tn), F32),
        ],
        compiler_params=_cparams(("arbitrary", "arbitrary")),
        name="norm_in_proj_glu_conv",
    )(x, g, w, w, w, state, dw_w, dw_b)


def _attend_heads(q_ref, kcat_ref, vcat_ref, bias_ref, o_ref, q_rows, k_rows, kpos0):
    heads = [slice(h * HEAD_DIM, (h + 1) * HEAD_DIM) for h in range(N_HEADS)]
    scores = []
    for cols in heads:
        q = q_ref[q_rows, cols].astype(BF16)
        scores.append(lax.dot_general(q, kcat_ref[k_rows, cols], (((1,), (1,)), ((), ())),
                                      preferred_element_type=F32))
    if kpos0 is not None:
        col = lax.broadcasted_iota(jnp.int32, (1, scores[0].shape[1]), 1)
        before_start = jnp.where(col + kpos0 >= 0, 0.0, -jnp.inf)
    probs = []
    for h, s in enumerate(scores):
        s = s * (HEAD_DIM ** -0.5) + bias_ref[h]
        if kpos0 is not None:
            s = s + before_start
        m = jnp.max(s, axis=-1, keepdims=True)
        p = jnp.exp(s - m)
        probs.append((p.astype(BF16), jnp.sum(p, axis=-1, keepdims=True)))
    for cols, (p, l) in zip(heads, probs):
        o = jnp.dot(p, vcat_ref[k_rows, cols], preferred_element_type=F32) / l
        o_ref[q_rows, cols] = o.astype(o_ref.dtype)


def _heads_to_rows(x_ref):
    heads = [x_ref[:, h * HEAD_DIM:(h + 1) * HEAD_DIM] for h in range(N_HEADS)]
    return jnp.swapaxes(jnp.stack(heads, axis=0), 0, 1)


def _expand_rel_bias(fvec_ref, bias_ref, n_valid, chunk_masks):
    tq = bias_ref.shape[1]
    i = lax.broadcasted_iota(jnp.int32, (tq, K_WIN), 0)
    j = lax.broadcasted_iota(jnp.int32, (tq, K_WIN), 1)
    if chunk_masks:
        lo = jnp.where(i < CHUNK, 0, CHUNK)
        hi = jnp.where(i < CHUNK, min(BAND_PAST + CHUNK, n_valid), n_valid)
        hidden = (j < lo) | (j >= hi)
    else:
        hidden = j >= n_valid
    for h in range(N_HEADS):
        rows = jnp.broadcast_to(fvec_ref[h:h + 1, :], (tq, Q_TILE + K_WIN))
        skew = pltpu.roll(rows, 0, 1, stride=1, stride_axis=0)
        bias_ref[h] = jnp.where(hidden, -jnp.inf, skew[:, Q_TILE:])


def _attn_prompt_kernel(q_ref, kp_ref, kc_ref, vp_ref, vc_ref, kf_ref, vf_ref, fvec_ref, o_ref, nk_ref, nv_ref,
                        kcat_ref, vcat_ref, bias_ref):
    b = pl.program_id(0)
    t = pl.program_id(1)

    @pl.when((b == 0) & (t == 0))
    def _():
        _expand_rel_bias(fvec_ref, bias_ref, K_WIN, True)

    @pl.when(t == pl.num_programs(1) - 1)
    def _():
        nk_ref[...] = _heads_to_rows(kf_ref)
        nv_ref[...] = _heads_to_rows(vf_ref)

    kcat_ref[0:ATT_BLOCK, :] = kp_ref[...]
    kcat_ref[ATT_BLOCK:, :] = kc_ref[...]
    vcat_ref[0:ATT_BLOCK, :] = vp_ref[...]
    vcat_ref[ATT_BLOCK:, :] = vc_ref[...]
    for r0 in range(0, ATT_BLOCK, Q_TILE):
        kpos0 = t * ATT_BLOCK + r0 - BAND_PAST
        _attend_heads(q_ref, kcat_ref, vcat_ref, bias_ref, o_ref,
                      slice(r0, r0 + Q_TILE), slice(r0, r0 + K_WIN), kpos0)


def _attn_prompt(u16, k_new, v_new, fvec):
    b, s, _ = u16.shape
    blk = (None, ATT_BLOCK, D_A)
    assert k_new.shape == (b, ATT_BLOCK, D_A)
    prev = lambda col: (lambda bi, t: (bi, jnp.maximum(t - 1, 0), col))
    cur = lambda col: (lambda bi, t: (bi, t, col))
    newest = pl.BlockSpec(blk, lambda bi, t: (bi, 0, 0))
    state = pl.BlockSpec((None, ATT_BLOCK, N_HEADS, HEAD_DIM), lambda bi, t: (bi, 0, 0, 0))
    state_shape = jax.ShapeDtypeStruct((b, ATT_BLOCK, N_HEADS, HEAD_DIM), F32)
    return pl.pallas_call(
        _attn_prompt_kernel,
        grid=(b, s // ATT_BLOCK),
        in_specs=[
            pl.BlockSpec(blk, cur(U_Q)),
            pl.BlockSpec(blk, prev(U_K)),
            pl.BlockSpec(blk, cur(U_K)),
            pl.BlockSpec(blk, prev(U_V)),
            pl.BlockSpec(blk, cur(U_V)),
            newest,
            newest,
            pl.BlockSpec(fvec.shape, lambda bi, t: (0, 0)),
        ],
        out_specs=[pl.BlockSpec(blk, cur(0)), state, state],
        out_shape=[jax.ShapeDtypeStruct((b, s, D_A), BF16), state_shape, state_shape],
        scratch_shapes=[
            pltpu.VMEM((2 * ATT_BLOCK, D_A), BF16),
            pltpu.VMEM((2 * ATT_BLOCK, D_A), BF16),
            pltpu.VMEM((N_HEADS, Q_TILE, K_WIN), F32),
        ],
        compiler_params=_cparams(("arbitrary", "arbitrary")),
        name="attn_prompt",
    )(u16, u16, u16, u16, u16, k_new, v_new, fvec)


def _attn_sample_kernel(q_ref, kn_ref, vn_ref, kc_ref, vc_ref, fvec_ref, o_ref, nk_ref, nv_ref,
                        kcat_ref, vcat_ref, bias_ref):
    n_streams, tq, _ = q_ref.shape
    lc = kc_ref.shape[1]

    @pl.when(pl.program_id(0) == 0)
    def _():
        _expand_rel_bias(fvec_ref, bias_ref, lc + tq, False)
        kcat_ref[lc + tq:, :] = jnp.zeros((K_WIN - lc - tq, D_A), BF16)
        vcat_ref[lc + tq:, :] = jnp.zeros((K_WIN - lc - tq, D_A), BF16)

    for s in range(n_streams):
        kcat_ref[lc:lc + tq, :] = kn_ref[s].astype(BF16)
        vcat_ref[lc:lc + tq, :] = vn_ref[s].astype(BF16)
        nk_ref[s] = _heads_to_rows(kn_ref.at[s])
        nv_ref[s] = _heads_to_rows(vn_ref.at[s])
        kch = jnp.swapaxes(kc_ref[s], 0, 1)
        vch = jnp.swapaxes(vc_ref[s], 0, 1)
        for h in range(N_HEADS):
            c0 = h * HEAD_DIM
            kcat_ref[0:lc, c0:c0 + HEAD_DIM] = kch[h].astype(BF16)
            vcat_ref[0:lc, c0:c0 + HEAD_DIM] = vch[h].astype(BF16)
        _attend_heads(q_ref.at[s], kcat_ref, vcat_ref, bias_ref, o_ref.at[s], slice(None), slice(None), None)


def _attn_sample(u16, k_new, v_new, k_cache, v_cache, fvec):
    b, t, _ = u16.shape
    lc = k_cache.shape[1]
    assert lc + t <= K_WIN and t <= Q_TILE
    ns = SAMPLE_STREAMS_PER_STEP if b % SAMPLE_STREAMS_PER_STEP == 0 else 1
    new = lambda col: pl.BlockSpec((ns, t, D_A), lambda bi: (bi, 0, col))
    cache = pl.BlockSpec((ns, lc, N_HEADS, HEAD_DIM), lambda bi: (bi, 0, 0, 0))
    state = pl.BlockSpec((ns, t, N_HEADS, HEAD_DIM), lambda bi: (bi, 0, 0, 0))
    state_shape = jax.ShapeDtypeStruct((b, t, N_HEADS, HEAD_DIM), F32)
    return pl.pallas_call(
        _attn_sample_kernel,
        grid=(b // ns,),
        in_specs=[new(U_Q), new(0), new(0), cache, cache, pl.BlockSpec(fvec.shape, lambda bi: (0, 0))],
        out_specs=[pl.BlockSpec((ns, t, D_A), lambda bi: (bi, 0, 0)), state, state],
        out_shape=[jax.ShapeDtypeStruct((b, t, D_A), BF16), state_shape, state_shape],
        scratch_shapes=[
            pltpu.VMEM((K_WIN, D_A), BF16),
            pltpu.VMEM((K_WIN, D_A), BF16),
            pltpu.VMEM((N_HEADS, t, K_WIN), F32),
        ],
        compiler_params=_cparams(("arbitrary",)),
        name="attn_sample",
    )(u16, k_new, v_new, k_cache, v_cache, fvec)


def _rel_bias_vector(rel_table):
    width = Q_TILE + K_WIN
    n_const = BAND_PAST + Q_TILE - REL_CLIP + 1
    n_rev = width - n_const
    assert 0 < n_rev <= 2 * REL_CLIP
    const = jnp.broadcast_to(rel_table[:, 2 * REL_CLIP:], (N_HEADS, n_const))
    rev = lax.rev(rel_table[:, 2 * REL_CLIP - n_rev:2 * REL_CLIP], (1,))
    return jnp.concatenate([const, rev], axis=1).astype(F32)


def _log_sigmoid(x):
    return -(jnp.maximum(-x, 0.0) + jnp.log1p(jnp.exp(-jnp.abs(x))))


def _shift_rows(x, d, fill, block):
    n, c = x.shape
    if d % SUBLANES == 0:
        x3 = x.reshape(n // block, block, c)
        pad = jnp.full((n // block, d, c), fill, x.dtype)
        return jnp.concatenate([pad, x3[:, :block - d]], axis=1).reshape(n, c)
    rolled = pltpu.roll(x, d, 0)
    row = lax.broadcasted_iota(jnp.int32, x.shape, 0)
    return jnp.where(row % block < d, fill, rolled)


def _rglru_kernel(xb_ref, lb_ref, h0_ref, cw_ref, cb_ref, wa_ref, ba_ref, wx_ref, bx_ref, lam_ref,
                  h_ref, hl_ref, xext_ref, hc_ref):
    t = pl.program_id(1)
    tt = xb_ref.shape[0]

    @pl.when(t == 0)
    def _():
        xext_ref[0:LRU_HALO, :] = lb_ref[...]
        hc_ref[...] = h0_ref[...]

    xext_ref[LRU_HALO:, :] = xb_ref[...]
    xc = cb_ref[...] + cw_ref[CONV_B - 1:CONV_B, :] * xb_ref[...]
    win = xext_ref[...]
    for k in range(CONV_B - 1):
        off = LRU_HALO - (CONV_B - 1) + k
        xc = xc + pltpu.roll(cw_ref[k:k + 1, :] * win, tt + LRU_HALO - off, 0)[:tt]
    xext_ref[0:LRU_HALO, :] = xext_ref[tt:tt + LRU_HALO, :]

    xcb = xc.astype(BF16)
    ra, rx = [], []
    for n in range(N_BLOCKS_B):
        blk = xcb[:, n * BLOCK_B:(n + 1) * BLOCK_B]
        ra.append(jnp.dot(blk, wa_ref[n], preferred_element_type=F32))
        rx.append(jnp.dot(blk, wx_ref[n], preferred_element_type=F32))
    r = _sigmoid(jnp.concatenate(ra, axis=-1) + ba_ref[...])
    i = _sigmoid(jnp.concatenate(rx, axis=-1) + bx_ref[...])
    log_a = LRU_C * r * _log_sigmoid(lam_ref[...])
    a = jnp.exp(log_a)
    th = jnp.tanh(log_a)
    bt = jnp.sqrt(-2.0 * th / (1.0 - th)) * (i * xc)

    block = min(tt, LRU_SCAN_ROWS)
    d = 1
    while d < block:
        bt = bt + a * _shift_rows(bt, d, 0.0, block)
        a = a * _shift_rows(a, d, 1.0, block)
        d *= 2
    carry = hc_ref[...]
    for r0 in range(0, tt, block):
        h = a[r0:r0 + block] * carry + bt[r0:r0 + block]
        h_ref[r0:r0 + block, :] = h.astype(h_ref.dtype)
        carry = h[block - 1:block, :]
    hc_ref[...] = carry
    hl_ref[...] = carry


def _rglru(xb, lb0, h0, cw, cb, w_a, b_a, w_x, b_x, lam, *, tt):
    b, t, _ = xb.shape
    row = lambda a: a.reshape(1, D_B)
    full = lambda a: pl.BlockSpec(a.shape, lambda bi, ti: (0,) * a.ndim)
    args = (cw, row(cb), w_a, row(b_a), w_x, row(b_x), row(lam))
    return pl.pallas_call(
        _rglru_kernel,
        grid=(b, t // tt),
        in_specs=[
            pl.BlockSpec((None, tt, D_B), lambda bi, ti: (bi, ti, 0)),
            pl.BlockSpec((None, LRU_HALO, D_B), lambda bi, ti: (bi, 0, 0)),
            pl.BlockSpec((None, 1, D_B), lambda bi, ti: (bi, 0, 0)),
        ] + [full(a) for a in args],
        out_specs=[
            pl.BlockSpec((None, tt, D_B), lambda bi, ti: (bi, ti, 0)),
            pl.BlockSpec((None, 1, D_B), lambda bi, ti: (bi, 0, 0)),
        ],
        out_shape=[jax.ShapeDtypeStruct((b, t, D_B), BF16), jax.ShapeDtypeStruct((b, 1, D_B), F32)],
        scratch_shapes=[pltpu.VMEM((LRU_HALO + tt, D_B), F32), pltpu.VMEM((1, D_B), F32)],
        compiler_params=_cparams(("parallel", "arbitrary")),
        name="rglru",
    )(xb, lb0, h0, *args)


def _gated_out_kernel(oa_ref, ga_ref, h_ref, gb_ref, x_ref, w_ref, o_ref):
    ma = (_silu(ga_ref[...].astype(F32)) * oa_ref[...].astype(F32)).astype(BF16)
    mb = (_silu(gb_ref[...].astype(F32)) * h_ref[...].astype(F32)).astype(BF16)
    acc = jnp.dot(ma, w_ref[0:D_A, :], preferred_element_type=F32)
    acc = acc + jnp.dot(mb, w_ref[D_A:, :], preferred_element_type=F32)
    o_ref[...] = x_ref[...] + acc


def _gated_out(o_a, u16, h, x, w, *, tm):
    m = x.shape[0]
    return pl.pallas_call(
        _gated_out_kernel,
        grid=(m // tm,),
        in_specs=[
            pl.BlockSpec((tm, D_A), lambda i: (i, 0)),
            pl.BlockSpec((tm, D_A), lambda i: (i, U_GA)),
            pl.BlockSpec((tm, D_B), lambda i: (i, 0)),
            pl.BlockSpec((tm, D_B), lambda i: (i, U_GB)),
            pl.BlockSpec((tm, D_MODEL), lambda i: (i, 0)),
            pl.BlockSpec(w.shape, lambda i: (0, 0), pipeline_mode=pl.Buffered(1)),
        ],
        out_specs=pl.BlockSpec((tm, D_MODEL), lambda i: (i, 0)),
        out_shape=jax.ShapeDtypeStruct((m, D_MODEL), F32),
        compiler_params=_cparams(("parallel",)),
        name="gated_out_proj",
    )(o_a, u16, h, u16, x, w)


def _ln_out_norm_kernel(cv_ref, sg_ref, lg_ref, lb_ref, w_ref, x_ref, g_ref, o_ref):
    acc = cv_ref[...]
    mu = jnp.mean(acc, axis=-1, keepdims=True)
    cen = acc - mu
    var = jnp.mean(cen * cen, axis=-1, keepdims=True)
    yn = cen * lax.rsqrt(var + EPS) * lg_ref[...] + lb_ref[...]
    y = (_silu(yn) * sg_ref[...].astype(F32)).astype(BF16)
    x = x_ref[...] + jnp.dot(y, w_ref[...], preferred_element_type=F32)
    ms = jnp.mean(x * x, axis=-1, keepdims=True)
    o_ref[...] = x * lax.rsqrt(ms + EPS) * g_ref[...]


def _ln_out_norm(cv, sg, ln_g, ln_b, w, x, g, *, tm):
    m = x.shape[0]
    row = pl.BlockSpec((1, D_C), lambda i: (0, 0))
    return pl.pallas_call(
        _ln_out_norm_kernel,
        grid=(m // tm,),
        in_specs=[
            pl.BlockSpec((tm, D_C), lambda i: (i, 0)),
            pl.BlockSpec((tm, D_C), lambda i: (i, 0)),
            row,
            row,
            pl.BlockSpec(w.shape, lambda i: (0, 0), pipeline_mode=pl.Buffered(1)),
            pl.BlockSpec((tm, D_MODEL), lambda i: (i, 0)),
            pl.BlockSpec((1, D_MODEL), lambda i: (0, 0)),
        ],
        out_specs=pl.BlockSpec((tm, D_MODEL), lambda i: (i, 0)),
        out_shape=jax.ShapeDtypeStruct((m, D_MODEL), F32),
        compiler_params=_cparams(("parallel",)),
        name="ln_out_proj_final_norm",
    )(cv, sg, ln_g.reshape(1, D_C), ln_b.reshape(1, D_C), w, x, g)


def _front_pad_rows(a, rows):
    return jnp.pad(a, ((0, 0), (rows - a.shape[1], 0), (0, 0)))


def _trunk(x, caches, w, *, tm, tm_out, lru_tt):
    b, t, _ = x.shape
    m = b * t
    x2d = x.reshape(m, D_MODEL)

    u16, xb, k_new, v_new = _norm_matmul(x2d, w["norm_ab"], w["w_in_ab"], t=t, tm=tm)
    u16_3 = u16.reshape(b, t, IN_AB)
    xb = xb.reshape(b, t, D_B)
    if caches is None:
        assert t % ATT_BLOCK == 0 and ATT_BLOCK == BAND_PAST
        o_a, new_k, new_v = _attn_prompt(u16_3, k_new, v_new, w["rel_bias_vec"])
        h0 = jnp.zeros((b, 1, D_B), F32)
        lb0 = jnp.zeros((b, LRU_HALO, D_B), F32)
        cb0 = jnp.zeros((b, CONV_HALO, D_C), F32)
    else:
        kc, vc, hc, lbc, cbc = caches
        o_a, new_k, new_v = _attn_sample(u16_3, k_new, v_new, kc, vc, w["rel_bias_vec"])
        h0 = hc.reshape(b, 1, D_B)
        lb0 = _front_pad_rows(lbc, LRU_HALO)
        cb0 = _front_pad_rows(cbc, CONV_HALO)
    h, h_last = _rglru(xb, lb0, h0, w["lru_conv_w"], w["lru_conv_b"], w["lru_w_a"], w["lru_b_a"],
                       w["lru_w_x"], w["lru_b_x"], w["lru_lambda"], tt=lru_tt)
    x1 = _gated_out(o_a.reshape(m, D_A), u16, h.reshape(m, D_B), x2d, w["w_out_ab"], tm=tm_out)

    new_h = h_last[:, 0]
    new_lb = xb[:, t - (CONV_B - 1):]

    cv, sg, z_tail = _norm_glu_conv(x1, w["norm_cv"], w["w_in_cv"], cb0, w["dw_w"], w["dw_b"], t=t, tm=tm, tn=512)
    out = _ln_out_norm(cv, sg, w["ln_g"], w["ln_b"], w["w_out_cv"], x1, w["final_norm"], tm=tm_out)
    segs_per_batch = z_tail.shape[0] // b
    new_cb = z_tail[segs_per_batch - 1::segs_per_batch, CONV_HALO - (CONV_C - 1):]

    return (out.reshape(b, t, D_MODEL), new_k[None], new_v[None], new_h[None], new_lb[None], new_cb[None])


def kernel(x_prompt, x_sample, cache_attn_k, cache_attn_v, state_lru_h, state_lru_conv, state_conv, norm_ab, w_in_ab, w_out_ab, rel_bias, lru_conv_w, lru_conv_b, lru_w_a, lru_b_a, lru_w_x, lru_b_x, lru_lambda, norm_cv, w_in_cv, w_out_cv, dw_w, dw_b, ln_g, ln_b, final_norm):
    assert norm_ab.shape[0] == 1 and norm_cv.shape[0] == 1, "one even and one odd layer"
    t_s = x_sample.shape[1]
    w = {
        "norm_ab": norm_ab[0].reshape(1, D_MODEL),
        "w_in_ab": w_in_ab[0].astype(BF16),
        "w_out_ab": w_out_ab[0].astype(BF16),
        "rel_bias_vec": _rel_bias_vector(rel_bias[0]),
        "lru_conv_w": lru_conv_w[0],
        "lru_conv_b": lru_conv_b[0],
        "lru_w_a": lru_w_a[0].astype(BF16),
        "lru_b_a": lru_b_a[0],
        "lru_w_x": lru_w_x[0].astype(BF16),
        "lru_b_x": lru_b_x[0],
        "lru_lambda": lru_lambda[0],
        "norm_cv": norm_cv[0].reshape(1, D_MODEL),
        "w_in_cv": w_in_cv[0].astype(BF16),
        "w_out_cv": w_out_cv[0].astype(BF16),
        "dw_w": dw_w[0],
        "dw_b": dw_b[0],
        "ln_g": ln_g[0],
        "ln_b": ln_b[0],
        "final_norm": final_norm.reshape(1, D_MODEL),
    }
    y_p, k_p, v_p, h_p, lb_p, cb_p = _trunk(x_prompt, None, w, tm=1024, tm_out=512, lru_tt=512)
    caches = (cache_attn_k[0], cache_attn_v[0], state_lru_h[0], state_lru_conv[0], state_conv[0])
    y_s, k_s, v_s, h_s, lb_s, cb_s = _trunk(x_sample, caches, w, tm=512, tm_out=256, lru_tt=t_s)
    return (y_p, y_s, k_p, v_p, h_p, lb_p, cb_p, k_s, v_s, h_s, lb_s, cb_s)
```

```python
import functools

import jax
import jax.numpy as jnp
from jax import lax
from jax.experimental import pallas as pl
from jax.experimental.pallas import tpu as pltpu

D_MODEL = 2048
CHUNK = 64
LEFT_CHUNKS = 8
BAND_PAST = CHUNK * LEFT_CHUNKS
N_HEADS = 8
HEAD_DIM = 128
D_A = N_HEADS * HEAD_DIM
REL_CLIP = 128
D_B = 1024
N_BLOCKS_B = 8
BLOCK_B = D_B // N_BLOCKS_B
CONV_B = 4
LRU_C = 8.0
D_C = D_MODEL
CONV_C = 31
IN_AB = 4 * D_A + 2 * D_B
IN_CV = 3 * D_C
EPS = 1e-6

F32 = jnp.float32
BF16 = jnp.bfloat16

SUBLANES = 8
LANES = 128
Q_TILE = 2 * CHUNK
K_WIN = BAND_PAST + Q_TILE
ATT_BLOCK = 512
ATT_HEAD_GROUP = 4
SAMPLE_STREAMS_PER_STEP = 2
CONV_HALO = 32
CONV_ROWS = 128
LRU_SCAN_ROWS = 32
LRU_HALO = 8
VMEM_LIMIT = 56 * 1024 * 1024


def _cparams(sem):
    return pltpu.CompilerParams(dimension_semantics=sem, vmem_limit_bytes=VMEM_LIMIT)


def _sigmoid(x):
    return 0.5 * jnp.tanh(0.5 * x) + 0.5


def _silu(x):
    half = 0.5 * x
    return half + half * jnp.tanh(half)


def _norm_to_bf16(x, g):
    ms = jnp.mean(x * x, axis=-1, keepdims=True)
    return (x * lax.rsqrt(ms + EPS) * g).astype(BF16)


U_Q, U_K, U_V, U_GA, U_XB, U_GB = range(6)


def _norm_matmul_kernel(x_ref, g_ref, w_ref, o16_ref, xb_ref, ks_ref, vs_ref, xn_ref, *, tiles_per_batch):
    i = pl.program_id(0)
    j = pl.program_id(1)
    tm = x_ref.shape[0]
    streams, keep, _ = ks_ref.shape

    @pl.when(j == 0)
    def _():
        xn_ref[...] = _norm_to_bf16(x_ref[...], g_ref[...])

    acc = jnp.dot(xn_ref[...], w_ref[...], preferred_element_type=F32)
    o16_ref[...] = acc.astype(BF16)

    @pl.when(j == U_XB)
    def _():
        xb_ref[...] = acc

    newest = acc[tm - streams * keep:].reshape(streams, keep, D_A)
    holds_state = (i % tiles_per_batch) == tiles_per_batch - 1

    @pl.when(holds_state & (j == U_K))
    def _():
        ks_ref[...] = newest

    @pl.when(holds_state & (j == U_V))
    def _():
        vs_ref[...] = newest


def _norm_matmul(x, g, w, *, t, tm):
    m, d = x.shape
    assert D_A == D_B and w.shape[1] == IN_AB
    keep = min(BAND_PAST, t)
    seg = min(t, tm)
    assert tm % seg == 0 and t % seg == 0 and keep <= seg and keep % SUBLANES == 0
    tiles_per_batch = t // seg
    streams = tm // seg
    assert streams == 1 or keep == seg
    state_spec = pl.BlockSpec((streams, keep, D_A), lambda i, j: (i // tiles_per_batch, 0, 0))
    state_shape = jax.ShapeDtypeStruct((m // t, keep, D_A), F32)
    return pl.pallas_call(
        functools.partial(_norm_matmul_kernel, tiles_per_batch=tiles_per_batch),
        grid=(m // tm, IN_AB // D_A),
        in_specs=[
            pl.BlockSpec((tm, d), lambda i, j: (i, 0)),
            pl.BlockSpec((1, d), lambda i, j: (0, 0)),
            pl.BlockSpec((d, D_A), lambda i, j: (0, j)),
        ],
        out_specs=[
            pl.BlockSpec((tm, D_A), lambda i, j: (i, j)),
            pl.BlockSpec((tm, D_B), lambda i, j: (i, 0)),
            state_spec,
            state_spec,
        ],
        out_shape=[jax.ShapeDtypeStruct((m, IN_AB), BF16), jax.ShapeDtypeStruct((m, D_B), F32), state_shape, state_shape],
        scratch_shapes=[pltpu.VMEM((tm, d), BF16)],
        compiler_params=_cparams(("arbitrary", "arbitrary")),
        name="norm_in_proj",
    )(x, g, w)


def _dwconv_rows(zext_ref, base, rows, lane0, w_ref, b_ref):
    lanes = slice(lane0, lane0 + LANES)
    acc = jnp.broadcast_to(b_ref[...], (rows, LANES))
    for phase in range(SUBLANES):
        win_rows = rows if phase == 0 else rows + SUBLANES
        part = None
        for k in range(CONV_C):
            off = base + CONV_HALO - (CONV_C - 1) + k
            if off % SUBLANES != phase:
                continue
            tap = w_ref[k]
            win = zext_ref[off - phase:off - phase + win_rows, lanes]
            term = win.reshape(win_rows // SUBLANES, SUBLANES, LANES) * tap[None]
            part = term if part is None else part + term
        if part is None:
            continue
        part = part.reshape(win_rows, LANES)
        if phase:
            part = pltpu.roll(part, win_rows - phase, 0)[:rows]
        acc = acc + part
    return acc


def _norm_glu_conv_kernel(x_ref, g_ref, wv_ref, wg_ref, wt_ref, st_ref, cw_ref, cb_ref,
                          cv_ref, sg_ref, tail_ref, xn_ref, zext_ref, carry_ref, *, seg, tiles_per_batch):
    i = pl.program_id(0)
    j = pl.program_id(1)
    n_seg = x_ref.shape[0] // seg
    ext = CONV_HALO + seg

    @pl.when(j == 0)
    def _():
        xn_ref[...] = _norm_to_bf16(x_ref[...], g_ref[...])

    xn = xn_ref[...]
    gate = jnp.dot(xn, wt_ref[...], preferred_element_type=F32)
    sg_ref[...] = _silu(gate).astype(BF16)
    val = jnp.dot(xn, wv_ref[...], preferred_element_type=F32)
    glu = jnp.dot(xn, wg_ref[...], preferred_element_type=F32)
    z = val * _sigmoid(glu)

    carried = tiles_per_batch > 1
    if carried:
        @pl.when(i % tiles_per_batch == 0)
        def _():
            carry_ref[j] = st_ref[0]

    rows = min(seg, CONV_ROWS)
    for s in range(n_seg):
        base = s * ext
        zext_ref[base:base + CONV_HALO, :] = carry_ref[j] if carried else st_ref[s]
        zext_ref[base + CONV_HALO:base + ext, :] = z[s * seg:(s + 1) * seg]
        for c in range(cw_ref.shape[0]):
            for r0 in range(0, seg, rows):
                out_rows = slice(s * seg + r0, s * seg + r0 + rows)
                cv_ref[out_rows, c * LANES:(c + 1) * LANES] = _dwconv_rows(
                    zext_ref, base + r0, rows, c * LANES, cw_ref.at[c], cb_ref.at[c])
        tail = zext_ref[base + seg:base + ext, :]
        tail_ref[s] = tail
        if carried:
            carry_ref[j] = tail


def _norm_glu_conv(x, g, w, state, dw_w, dw_b, *, t, tm, tn):
    m, d = x.shape
    nb = D_C // tn
    seg = min(t, tm)
    assert tm % seg == 0 and t % seg == 0 and seg % min(seg, CONV_ROWS) == 0 and seg % SUBLANES == 0
    tiles_per_batch = t // seg
    dw_w = dw_w.reshape(CONV_C, D_C // LANES, 1, LANES).transpose(1, 0, 2, 3)
    dw_w = jnp.broadcast_to(dw_w, (D_C // LANES, CONV_C, SUBLANES, LANES))
    dw_b = dw_b.reshape(D_C // LANES, 1, LANES)
    bpt = tm // seg if tiles_per_batch == 1 else 1
    batch_blk = (lambda i: i // tiles_per_batch) if tiles_per_batch > 1 else (lambda i: i)
    state_spec = pl.BlockSpec((bpt, CONV_HALO, tn), lambda i, j: (batch_blk(i), 0, j))
    body = functools.partial(_norm_glu_conv_kernel, seg=seg, tiles_per_batch=tiles_per_batch)
    return pl.pallas_call(
        body,
        grid=(m // tm, nb),
        in_specs=[
            pl.BlockSpec((tm, d), lambda i, j: (i, 0)),
            pl.BlockSpec((1, d), lambda i, j: (0, 0)),
            pl.BlockSpec((d, tn), lambda i, j: (0, j)),
            pl.BlockSpec((d, tn), lambda i, j: (0, j + nb)),
            pl.BlockSpec((d, tn), lambda i, j: (0, j + 2 * nb)),
            state_spec,
            pl.BlockSpec((tn // LANES, CONV_C, SUBLANES, LANES), lambda i, j: (j, 0, 0, 0)),
            pl.BlockSpec((tn // LANES, 1, LANES), lambda i, j: (j, 0, 0)),
        ],
        out_specs=[
            pl.BlockSpec((tm, tn), lambda i, j: (i, j)),
            pl.BlockSpec((tm, tn), lambda i, j: (i, j)),
            pl.BlockSpec((tm // seg, CONV_HALO, tn), lambda i, j: (i, 0, j)),
        ],
        out_shape=[
            jax.ShapeDtypeStruct((m, D_C), F32),
            jax.ShapeDtypeStruct((m, D_C), BF16),
            jax.ShapeDtypeStruct((m // seg, CONV_HALO, D_C), F32),
        ],
        scratch_shapes=[
            pltpu.VMEM((tm, d), BF16),
            pltpu.VMEM(((tm // seg) * (CONV_HALO + seg), tn), F32),
            pltpu.VMEM((nb, CONV_HALO, tn), F32),
        ],
        compiler_params=_cparams(("arbitrary", "arbitrary")),
        name="norm_in_proj_glu_conv",
    )(x, g, w, w, w, state, dw_w, dw_b)


def _attend_heads(q_ref, kcat_ref, vcat_ref, bias_ref, o_ref, q_rows, k_rows, kpos0):
    if kpos0 is not None:
        col = lax.broadcasted_iota(jnp.int32, (1, K_WIN), 1)
        before_start = jnp.where(col + kpos0 >= 0, 0.0, -jnp.inf)
    for h0 in range(0, N_HEADS, ATT_HEAD_GROUP):
        group = range(h0, h0 + ATT_HEAD_GROUP)
        heads = [slice(h * HEAD_DIM, (h + 1) * HEAD_DIM) for h in group]
        scores = []
        for cols in heads:
            q = q_ref[q_rows, cols].astype(BF16)
            scores.append(lax.dot_general(q, kcat_ref[k_rows, cols], (((1,), (1,)), ((), ())),
                                          preferred_element_type=F32))
        probs = []
        for h, s in zip(group, scores):
            s = s * (HEAD_DIM ** -0.5) + bias_ref[h]
            if kpos0 is not None:
                s = s + before_start
            m = jnp.max(s, axis=-1, keepdims=True)
            p = jnp.exp(s - m)
            probs.append((p.astype(BF16), jnp.sum(p, axis=-1, keepdims=True)))
        for cols, (p, l) in zip(heads, probs):
            o = jnp.dot(p, vcat_ref[k_rows, cols], preferred_element_type=F32) / l
            o_ref[q_rows, cols] = o.astype(o_ref.dtype)


def _heads_to_rows(x_ref):
    heads = [x_ref[:, h * HEAD_DIM:(h + 1) * HEAD_DIM] for h in range(N_HEADS)]
    return jnp.swapaxes(jnp.stack(heads, axis=0), 0, 1)


def _expand_rel_bias(fvec_ref, bias_ref, n_valid, chunk_masks):
    tq = bias_ref.shape[1]
    i = lax.broadcasted_iota(jnp.int32, (tq, K_WIN), 0)
    j = lax.broadcasted_iota(jnp.int32, (tq, K_WIN), 1)
    if chunk_masks:
        lo = jnp.where(i < CHUNK, 0, CHUNK)
        hi = jnp.where(i < CHUNK, min(BAND_PAST + CHUNK, n_valid), n_valid)
        hidden = (j < lo) | (j >= hi)
    else:
        hidden = j >= n_valid
    for h in range(N_HEADS):
        rows = jnp.broadcast_to(fvec_ref[h:h + 1, :], (tq, Q_TILE + K_WIN))
        skew = pltpu.roll(rows, 0, 1, stride=1, stride_axis=0)
        bias_ref[h] = jnp.where(hidden, -jnp.inf, skew[:, Q_TILE:])


def _attn_prompt_kernel(q_ref, kp_ref, kc_ref, vp_ref, vc_ref, kf_ref, vf_ref, fvec_ref, o_ref, nk_ref, nv_ref,
                        kcat_ref, vcat_ref, bias_ref):
    b = pl.program_id(0)
    t = pl.program_id(1)

    @pl.when((b == 0) & (t == 0))
    def _():
        _expand_rel_bias(fvec_ref, bias_ref, K_WIN, True)

    @pl.when(t == pl.num_programs(1) - 1)
    def _():
        nk_ref[...] = _heads_to_rows(kf_ref)
        nv_ref[...] = _heads_to_rows(vf_ref)

    kcat_ref[0:ATT_BLOCK, :] = kp_ref[...]
    kcat_ref[ATT_BLOCK:, :] = kc_ref[...]
    vcat_ref[0:ATT_BLOCK, :] = vp_ref[...]
    vcat_ref[ATT_BLOCK:, :] = vc_ref[...]
    for r0 in range(0, ATT_BLOCK, Q_TILE):
        kpos0 = t * ATT_BLOCK + r0 - BAND_PAST
        _attend_heads(q_ref, kcat_ref, vcat_ref, bias_ref, o_ref,
                      slice(r0, r0 + Q_TILE), slice(r0, r0 + K_WIN), kpos0)


def _attn_prompt(u16, k_new, v_new, fvec):
    b, s, _ = u16.shape
    blk = (None, ATT_BLOCK, D_A)
    assert k_new.shape == (b, ATT_BLOCK, D_A)
    prev = lambda col: (lambda bi, t: (bi, jnp.maximum(t - 1, 0), col))
    cur = lambda col: (lambda bi, t: (bi, t, col))
    newest = pl.BlockSpec(blk, lambda bi, t: (bi, 0, 0))
    state = pl.BlockSpec((None, ATT_BLOCK, N_HEADS, HEAD_DIM), lambda bi, t: (bi, 0, 0, 0))
    state_shape = jax.ShapeDtypeStruct((b, ATT_BLOCK, N_HEADS, HEAD_DIM), F32)
    return pl.pallas_call(
        _attn_prompt_kernel,
        grid=(b, s // ATT_BLOCK),
        in_specs=[
            pl.BlockSpec(blk, cur(U_Q)),
            pl.BlockSpec(blk, prev(U_K)),
            pl.BlockSpec(blk, cur(U_K)),
            pl.BlockSpec(blk, prev(U_V)),
            pl.BlockSpec(blk, cur(U_V)),
            newest,
            newest,
            pl.BlockSpec(fvec.shape, lambda bi, t: (0, 0)),
        ],
        out_specs=[pl.BlockSpec(blk, cur(0)), state, state],
        out_shape=[jax.ShapeDtypeStruct((b, s, D_A), BF16), state_shape, state_shape],
        scratch_shapes=[
            pltpu.VMEM((2 * ATT_BLOCK, D_A), BF16),
            pltpu.VMEM((2 * ATT_BLOCK, D_A), BF16),
            pltpu.VMEM((N_HEADS, Q_TILE, K_WIN), F32),
        ],
        compiler_params=_cparams(("arbitrary", "arbitrary")),
        name="attn_prompt",
    )(u16, u16, u16, u16, u16, k_new, v_new, fvec)


def _attn_sample_kernel(q_ref, kn_ref, vn_ref, kc_ref, vc_ref, fvec_ref, o_ref, nk_ref, nv_ref,
                        kcat_ref, vcat_ref, bias_ref):
    n_streams, tq, _ = q_ref.shape
    lc = kc_ref.shape[1]

    @pl.when(pl.program_id(0) == 0)
    def _():
        _expand_rel_bias(fvec_ref, bias_ref, lc + tq, False)
        kcat_ref[lc + tq:, :] = jnp.zeros((K_WIN - lc - tq, D_A), BF16)
        vcat_ref[lc + tq:, :] = jnp.zeros((K_WIN - lc - tq, D_A), BF16)

    for s in range(n_streams):
        kcat_ref[lc:lc + tq, :] = kn_ref[s].astype(BF16)
        vcat_ref[lc:lc + tq, :] = vn_ref[s].astype(BF16)
        nk_ref[s] = _heads_to_rows(kn_ref.at[s])
        nv_ref[s] = _heads_to_rows(vn_ref.at[s])
        kch = jnp.swapaxes(kc_ref[s], 0, 1)
        vch = jnp.swapaxes(vc_ref[s], 0, 1)
        for h in range(N_HEADS):
            c0 = h * HEAD_DIM
            kcat_ref[0:lc, c0:c0 + HEAD_DIM] = kch[h].astype(BF16)
            vcat_ref[0:lc, c0:c0 + HEAD_DIM] = vch[h].astype(BF16)
        _attend_heads(q_ref.at[s], kcat_ref, vcat_ref, bias_ref, o_ref.at[s], slice(None), slice(None), None)


def _attn_sample(u16, k_new, v_new, k_cache, v_cache, fvec):
    b, t, _ = u16.shape
    lc = k_cache.shape[1]
    assert lc + t <= K_WIN and t <= Q_TILE
    ns = SAMPLE_STREAMS_PER_STEP if b % SAMPLE_STREAMS_PER_STEP == 0 else 1
    new = lambda col: pl.BlockSpec((ns, t, D_A), lambda bi: (bi, 0, col))
    cache = pl.BlockSpec((ns, lc, N_HEADS, HEAD_DIM), lambda bi: (bi, 0, 0, 0))
    state = pl.BlockSpec((ns, t, N_HEADS, HEAD_DIM), lambda bi: (bi, 0, 0, 0))
    state_shape = jax.ShapeDtypeStruct((b, t, N_HEADS, HEAD_DIM), F32)
    return pl.pallas_call(
        _attn_sample_kernel,
        grid=(b // ns,),
        in_specs=[new(U_Q), new(0), new(0), cache, cache, pl.BlockSpec(fvec.shape, lambda bi: (0, 0))],
        out_specs=[pl.BlockSpec((ns, t, D_A), lambda bi: (bi, 0, 0)), state, state],
        out_shape=[jax.ShapeDtypeStruct((b, t, D_A), BF16), state_shape, state_shape],
        scratch_shapes=[
            pltpu.VMEM((K_WIN, D_A), BF16),
            pltpu.VMEM((K_WIN, D_A), BF16),
            pltpu.VMEM((N_HEADS, t, K_WIN), F32),
        ],
        compiler_params=_cparams(("arbitrary",)),
        name="attn_sample",
    )(u16, k_new, v_new, k_cache, v_cache, fvec)


def _rel_bias_vector(rel_table):
    width = Q_TILE + K_WIN
    n_const = BAND_PAST + Q_TILE - REL_CLIP + 1
    n_rev = width - n_const
    assert 0 < n_rev <= 2 * REL_CLIP
    const = jnp.broadcast_to(rel_table[:, 2 * REL_CLIP:], (N_HEADS, n_const))
    rev = lax.rev(rel_table[:, 2 * REL_CLIP - n_rev:2 * REL_CLIP], (1,))
    return jnp.concatenate([const, rev], axis=1).astype(F32)


def _log_sigmoid(x):
    return -(jnp.maximum(-x, 0.0) + jnp.log1p(jnp.exp(-jnp.abs(x))))


def _shift_rows(x, d, fill, block):
    n, c = x.shape
    if d % SUBLANES == 0:
        x3 = x.reshape(n // block, block, c)
        pad = jnp.full((n // block, d, c), fill, x.dtype)
        return jnp.concatenate([pad, x3[:, :block - d]], axis=1).reshape(n, c)
    rolled = pltpu.roll(x, d, 0)
    row = lax.broadcasted_iota(jnp.int32, x.shape, 0)
    return jnp.where(row % block < d, fill, rolled)


def _rglru_kernel(xb_ref, lb_ref, h0_ref, cw_ref, cb_ref, wa_ref, ba_ref, wx_ref, bx_ref, lam_ref,
                  h_ref, hl_ref, xext_ref, hc_ref):
    t = pl.program_id(1)
    tt = xb_ref.shape[0]

    @pl.when(t == 0)
    def _():
        xext_ref[0:LRU_HALO, :] = lb_ref[...]
        hc_ref[...] = h0_ref[...]

    xext_ref[LRU_HALO:, :] = xb_ref[...]
    xc = cb_ref[...] + cw_ref[CONV_B - 1:CONV_B, :] * xb_ref[...]
    win = xext_ref[...]
    for k in range(CONV_B - 1):
        off = LRU_HALO - (CONV_B - 1) + k
        xc = xc + pltpu.roll(cw_ref[k:k + 1, :] * win, tt + LRU_HALO - off, 0)[:tt]
    xext_ref[0:LRU_HALO, :] = xext_ref[tt:tt + LRU_HALO, :]

    xcb = xc.astype(BF16)
    ra, rx = [], []
    for n in range(N_BLOCKS_B):
        blk = xcb[:, n * BLOCK_B:(n + 1) * BLOCK_B]
        ra.append(jnp.dot(blk, wa_ref[n], preferred_element_type=F32))
        rx.append(jnp.dot(blk, wx_ref[n], preferred_element_type=F32))
    r = _sigmoid(jnp.concatenate(ra, axis=-1) + ba_ref[...])
    i = _sigmoid(jnp.concatenate(rx, axis=-1) + bx_ref[...])
    log_a = LRU_C * r * _log_sigmoid(lam_ref[...])
    a = jnp.exp(log_a)
    th = jnp.tanh(log_a)
    bt = jnp.sqrt(-2.0 * th / (1.0 - th)) * (i * xc)

    block = min(tt, LRU_SCAN_ROWS)
    d = 1
    while d < block:
        bt = bt + a * _shift_rows(bt, d, 0.0, block)
        a = a * _shift_rows(a, d, 1.0, block)
        d *= 2
    carry = hc_ref[...]
    for r0 in range(0, tt, block):
        h = a[r0:r0 + block] * carry + bt[r0:r0 + block]
        h_ref[r0:r0 + block, :] = h.astype(h_ref.dtype)
        carry = h[block - 1:block, :]
    hc_ref[...] = carry
    hl_ref[...] = carry


def _rglru(xb, lb0, h0, cw, cb, w_a, b_a, w_x, b_x, lam, *, tt):
    b, t, _ = xb.shape
    row = lambda a: a.reshape(1, D_B)
    full = lambda a: pl.BlockSpec(a.shape, lambda bi, ti: (0,) * a.ndim)
    args = (cw, row(cb), w_a, row(b_a), w_x, row(b_x), row(lam))
    return pl.pallas_call(
        _rglru_kernel,
        grid=(b, t // tt),
        in_specs=[
            pl.BlockSpec((None, tt, D_B), lambda bi, ti: (bi, ti, 0)),
            pl.BlockSpec((None, LRU_HALO, D_B), lambda bi, ti: (bi, 0, 0)),
            pl.BlockSpec((None, 1, D_B), lambda bi, ti: (bi, 0, 0)),
        ] + [full(a) for a in args],
        out_specs=[
            pl.BlockSpec((None, tt, D_B), lambda bi, ti: (bi, ti, 0)),
            pl.BlockSpec((None, 1, D_B), lambda bi, ti: (bi, 0, 0)),
        ],
        out_shape=[jax.ShapeDtypeStruct((b, t, D_B), BF16), jax.ShapeDtypeStruct((b, 1, D_B), F32)],
        scratch_shapes=[pltpu.VMEM((LRU_HALO + tt, D_B), F32), pltpu.VMEM((1, D_B), F32)],
        compiler_params=_cparams(("parallel", "arbitrary")),
        name="rglru",
    )(xb, lb0, h0, *args)


def _gated_out_kernel(oa_ref, ga_ref, h_ref, gb_ref, x_ref, w_ref, o_ref):
    ma = (_silu(ga_ref[...].astype(F32)) * oa_ref[...].astype(F32)).astype(BF16)
    mb = (_silu(gb_ref[...].astype(F32)) * h_ref[...].astype(F32)).astype(BF16)
    acc = jnp.dot(ma, w_ref[0:D_A, :], preferred_element_type=F32)
    acc = acc + jnp.dot(mb, w_ref[D_A:, :], preferred_element_type=F32)
    o_ref[...] = x_ref[...] + acc


def _gated_out(o_a, u16, h, x, w, *, tm):
    m = x.shape[0]
    return pl.pallas_call(
        _gated_out_kernel,
        grid=(m // tm,),
        in_specs=[
            pl.BlockSpec((tm, D_A), lambda i: (i, 0)),
            pl.BlockSpec((tm, D_A), lambda i: (i, U_GA)),
            pl.BlockSpec((tm, D_B), lambda i: (i, 0)),
            pl.BlockSpec((tm, D_B), lambda i: (i, U_GB)),
            pl.BlockSpec((tm, D_MODEL), lambda i: (i, 0)),
            pl.BlockSpec(w.shape, lambda i: (0, 0), pipeline_mode=pl.Buffered(1)),
        ],
        out_specs=pl.BlockSpec((tm, D_MODEL), lambda i: (i, 0)),
        out_shape=jax.ShapeDtypeStruct((m, D_MODEL), F32),
        compiler_params=_cparams(("parallel",)),
        name="gated_out_proj",
    )(o_a, u16, h, u16, x, w)


def _ln_out_norm_kernel(cv_ref, sg_ref, lg_ref, lb_ref, w_ref, x_ref, g_ref, o_ref):
    acc = cv_ref[...]
    mu = jnp.mean(acc, axis=-1, keepdims=True)
    cen = acc - mu
    var = jnp.mean(cen * cen, axis=-1, keepdims=True)
    yn = cen * lax.rsqrt(var + EPS) * lg_ref[...] + lb_ref[...]
    y = (_silu(yn) * sg_ref[...].astype(F32)).astype(BF16)
    x = x_ref[...] + jnp.dot(y, w_ref[...], preferred_element_type=F32)
    ms = jnp.mean(x * x, axis=-1, keepdims=True)
    o_ref[...] = x * lax.rsqrt(ms + EPS) * g_ref[...]


def _ln_out_norm(cv, sg, ln_g, ln_b, w, x, g, *, tm):
    m = x.shape[0]
    row = pl.BlockSpec((1, D_C), lambda i: (0, 0))
    return pl.pallas_call(
        _ln_out_norm_kernel,
        grid=(m // tm,),
        in_specs=[
            pl.BlockSpec((tm, D_C), lambda i: (i, 0)),
            pl.BlockSpec((tm, D_C), lambda i: (i, 0)),
            row,
            row,
            pl.BlockSpec(w.shape, lambda i: (0, 0), pipeline_mode=pl.Buffered(1)),
            pl.BlockSpec((tm, D_MODEL), lambda i: (i, 0)),
            pl.BlockSpec((1, D_MODEL), lambda i: (0, 0)),
        ],
        out_specs=pl.BlockSpec((tm, D_MODEL), lambda i: (i, 0)),
        out_shape=jax.ShapeDtypeStruct((m, D_MODEL), F32),
        compiler_params=_cparams(("parallel",)),
        name="ln_out_proj_final_norm",
    )(cv, sg, ln_g.reshape(1, D_C), ln_b.reshape(1, D_C), w, x, g)


def _front_pad_rows(a, rows):
    return jnp.pad(a, ((0, 0), (rows - a.shape[1], 0), (0, 0)))


def _trunk(x, caches, w, *, tm, tm_out, lru_tt):
    b, t, _ = x.shape
    m = b * t
    x2d = x.reshape(m, D_MODEL)

    u16, xb, k_new, v_new = _norm_matmul(x2d, w["norm_ab"], w["w_in_ab"], t=t, tm=tm)
    u16_3 = u16.reshape(b, t, IN_AB)
    xb = xb.reshape(b, t, D_B)
    if caches is None:
        assert t % ATT_BLOCK == 0 and ATT_BLOCK == BAND_PAST
        o_a, new_k, new_v = _attn_prompt(u16_3, k_new, v_new, w["rel_bias_vec"])
        h0 = jnp.zeros((b, 1, D_B), F32)
        lb0 = jnp.zeros((b, LRU_HALO, D_B), F32)
        cb0 = jnp.zeros((b, CONV_HALO, D_C), F32)
    else:
        kc, vc, hc, lbc, cbc = caches
        o_a, new_k, new_v = _attn_sample(u16_3, k_new, v_new, kc, vc, w["rel_bias_vec"])
        h0 = hc.reshape(b, 1, D_B)
        lb0 = _front_pad_rows(lbc, LRU_HALO)
        cb0 = _front_pad_rows(cbc, CONV_HALO)
    h, h_last = _rglru(xb, lb0, h0, w["lru_conv_w"], w["lru_conv_b"], w["lru_w_a"], w["lru_b_a"],
                       w["lru_w_x"], w["lru_b_x"], w["lru_lambda"], tt=lru_tt)
    x1 = _gated_out(o_a.reshape(m, D_A), u16, h.reshape(m, D_B), x2d, w["w_out_ab"], tm=tm_out)

    new_h = h_last[:, 0]
    new_lb = xb[:, t - (CONV_B - 1):]

    cv, sg, z_tail = _norm_glu_conv(x1, w["norm_cv"], w["w_in_cv"], cb0, w["dw_w"], w["dw_b"], t=t, tm=tm, tn=512)
    out = _ln_out_norm(cv, sg, w["ln_g"], w["ln_b"], w["w_out_cv"], x1, w["final_norm"], tm=tm_out)
    segs_per_batch = z_tail.shape[0] // b
    new_cb = z_tail[segs_per_batch - 1::segs_per_batch, CONV_HALO - (CONV_C - 1):]

    return (out.reshape(b, t, D_MODEL), new_k[None], new_v[None], new_h[None], new_lb[None], new_cb[None])


def kernel(x_prompt, x_sample, cache_attn_k, cache_attn_v, state_lru_h, state_lru_conv, state_conv, norm_ab, w_in_ab, w_out_ab, rel_bias, lru_conv_w, lru_conv_b, lru_w_a, lru_b_a, lru_w_x, lru_b_x, lru_lambda, norm_cv, w_in_cv, w_out_cv, dw_w, dw_b, ln_g, ln_b, final_norm):
    assert norm_ab.shape[0] == 1 and norm_cv.shape[0] == 1, "one even and one odd layer"
    t_s = x_sample.shape[1]
    w = {
        "norm_ab": norm_ab[0].reshape(1, D_MODEL),
        "w_in_ab": w_in_ab[0].astype(BF16),
        "w_out_ab": w_out_ab[0].astype(BF16),
        "rel_bias_vec": _rel_bias_vector(rel_bias[0]),
        "lru_conv_w": lru_conv_w[0],
        "lru_conv_b": lru_conv_b[0],
        "lru_w_a": lru_w_a[0].astype(BF16),
        "lru_b_a": lru_b_a[0],
        "lru_w_x": lru_w_x[0].astype(BF16),
        "lru_b_x": lru_b_x[0],
        "lru_lambda": lru_lambda[0],
        "norm_cv": norm_cv[0].reshape(1, D_MODEL),
        "w_in_cv": w_in_cv[0].astype(BF16),
        "w_out_cv": w_out_cv[0].astype(BF16),
        "dw_w": dw_w[0],
        "dw_b": dw_b[0],
        "ln_g": ln_g[0],
        "ln_b": ln_b[0],
        "final_norm": final_norm.reshape(1, D_MODEL),
    }
    y_p, k_p, v_p, h_p, lb_p, cb_p = _trunk(x_prompt, None, w, tm=1024, tm_out=512, lru_tt=1024)
    caches = (cache_attn_k[0], cache_attn_v[0], state_lru_h[0], state_lru_conv[0], state_conv[0])
    y_s, k_s, v_s, h_s, lb_s, cb_s = _trunk(x_sample, caches, w, tm=512, tm_out=256, lru_tt=t_s)
    return (y_p, y_s, k_p, v_p, h_p, lb_p, cb_p, k_s, v_s, h_s, lb_s, cb_s)
```

```python
import functools

import jax
import jax.numpy as jnp
from jax import lax
from jax.experimental import pallas as pl
from jax.experimental.pallas import tpu as pltpu

D_MODEL = 2048
CHUNK = 64
LEFT_CHUNKS = 8
BAND_PAST = CHUNK * LEFT_CHUNKS
N_HEADS = 8
HEAD_DIM = 128
D_A = N_HEADS * HEAD_DIM
REL_CLIP = 128
D_B = 1024
N_BLOCKS_B = 8
BLOCK_B = D_B // N_BLOCKS_B
CONV_B = 4
LRU_C = 8.0
D_C = D_MODEL
CONV_C = 31
IN_AB = 4 * D_A + 2 * D_B
IN_CV = 3 * D_C
EPS = 1e-6

F32 = jnp.float32
BF16 = jnp.bfloat16

SUBLANES = 8
LANES = 128
Q_TILE = 2 * CHUNK
K_WIN = BAND_PAST + Q_TILE
ATT_BLOCK = 512
SAMPLE_STREAMS_PER_STEP = 2
CONV_HALO = 32
CONV_ROWS = 128
CV_TN = 512
LRU_SCAN_ROWS = 32
LRU_HALO = 8
VMEM_LIMIT = 56 * 1024 * 1024


def _cparams(sem):
    return pltpu.CompilerParams(dimension_semantics=sem, vmem_limit_bytes=VMEM_LIMIT)


def _sigmoid(x):
    return 0.5 * jnp.tanh(0.5 * x) + 0.5


def _silu(x):
    half = 0.5 * x
    return half + half * jnp.tanh(half)


def _norm_to_bf16(x, g):
    ms = jnp.mean(x * x, axis=-1, keepdims=True)
    return (x * lax.rsqrt(ms + EPS) * g).astype(BF16)


U_Q, U_K, U_V, U_GA, U_XB, U_GB = range(6)
U32_K, U32_V, U32_XB = U_K // 2, U_V // 2, U_XB // 2


def _norm_matmul_kernel(x_ref, g_ref, w_ref, o16_ref, o32_ref, xn_ref):
    j = pl.program_id(1)

    @pl.when(j == 0)
    def _():
        xn_ref[...] = _norm_to_bf16(x_ref[...], g_ref[...])

    acc = jnp.dot(xn_ref[...], w_ref[...], preferred_element_type=F32)
    o16_ref[...] = acc.astype(BF16)

    @pl.when((j == U_K) | (j == U_V) | (j == U_XB))
    def _():
        o32_ref[...] = acc


def _column_blocks(w, tn):
    d, n = w.shape
    return w.reshape(d, n // tn, tn).transpose(1, 0, 2)


def _norm_matmul(x, g, w, *, tm):
    m, d = x.shape
    assert D_A == D_B and w.shape == (IN_AB // D_A, d, D_A)
    return pl.pallas_call(
        _norm_matmul_kernel,
        grid=(m // tm, IN_AB // D_A),
        in_specs=[
            pl.BlockSpec((tm, d), lambda i, j: (i, 0)),
            pl.BlockSpec((1, d), lambda i, j: (0, 0)),
            pl.BlockSpec((None, d, D_A), lambda i, j: (j, 0, 0)),
        ],
        out_specs=[pl.BlockSpec((None, tm, D_A), lambda i, j: (j, i, 0)), pl.BlockSpec((tm, D_A), lambda i, j: (i, j // 2))],
        out_shape=[jax.ShapeDtypeStruct((IN_AB // D_A, m, D_A), BF16), jax.ShapeDtypeStruct((m, 3 * D_A), F32)],
        scratch_shapes=[pltpu.VMEM((tm, d), BF16)],
        compiler_params=_cparams(("parallel", "arbitrary")),
        name="norm_in_proj",
    )(x, g, w)


def _dwconv_rows(zext_ref, base, rows, lane0, w_ref, b_ref):
    lanes = slice(lane0, lane0 + LANES)
    acc = jnp.broadcast_to(b_ref[...], (rows, LANES))
    for phase in range(SUBLANES):
        win_rows = rows if phase == 0 else rows + SUBLANES
        part = None
        for k in range(CONV_C):
            off = base + CONV_HALO - (CONV_C - 1) + k
            if off % SUBLANES != phase:
                continue
            tap = w_ref[k]
            win = zext_ref[off - phase:off - phase + win_rows, lanes]
            term = win.reshape(win_rows // SUBLANES, SUBLANES, LANES) * tap[None]
            part = term if part is None else part + term
        if part is None:
            continue
        part = part.reshape(win_rows, LANES)
        if phase:
            part = pltpu.roll(part, win_rows - phase, 0)[:rows]
        acc = acc + part
    return acc


def _norm_glu_conv_kernel(x_ref, g_ref, wv_ref, wg_ref, wt_ref, st_ref, cw_ref, cb_ref,
                          cv_ref, sg_ref, tail_ref, xn_ref, zext_ref, carry_ref, *, seg, tiles_per_batch):
    i = pl.program_id(0)
    j = pl.program_id(1)
    n_seg = x_ref.shape[0] // seg
    ext = CONV_HALO + seg

    @pl.when(j == 0)
    def _():
        xn_ref[...] = _norm_to_bf16(x_ref[...], g_ref[...])

    xn = xn_ref[...]
    gate = jnp.dot(xn, wt_ref[...], preferred_element_type=F32)
    sg_ref[...] = _silu(gate).astype(BF16)
    val = jnp.dot(xn, wv_ref[...], preferred_element_type=F32)
    glu = jnp.dot(xn, wg_ref[...], preferred_element_type=F32)
    z = val * _sigmoid(glu)

    carried = tiles_per_batch > 1
    if carried:
        @pl.when(i % tiles_per_batch == 0)
        def _():
            carry_ref[j] = st_ref[0]

    rows = min(seg, CONV_ROWS)
    for s in range(n_seg):
        base = s * ext
        zext_ref[base:base + CONV_HALO, :] = carry_ref[j] if carried else st_ref[s]
        zext_ref[base + CONV_HALO:base + ext, :] = z[s * seg:(s + 1) * seg]
        for c in range(cw_ref.shape[0]):
            for r0 in range(0, seg, rows):
                out_rows = slice(s * seg + r0, s * seg + r0 + rows)
                cv_ref[out_rows, c * LANES:(c + 1) * LANES] = _dwconv_rows(
                    zext_ref, base + r0, rows, c * LANES, cw_ref.at[c], cb_ref.at[c])
        tail = zext_ref[base + seg:base + ext, :]
        tail_ref[s] = tail
        if carried:
            carry_ref[j] = tail


def _norm_glu_conv(x, g, w, state, dw_w, dw_b, *, t, tm, tn):
    m, d = x.shape
    nb = D_C // tn
    assert w.shape == (3 * nb, d, tn)
    seg = min(t, tm)
    assert tm % seg == 0 and t % seg == 0 and seg % min(seg, CONV_ROWS) == 0 and seg % SUBLANES == 0
    tiles_per_batch = t // seg
    dw_w = dw_w.reshape(CONV_C, D_C // LANES, 1, LANES).transpose(1, 0, 2, 3)
    dw_w = jnp.broadcast_to(dw_w, (D_C // LANES, CONV_C, SUBLANES, LANES))
    dw_b = dw_b.reshape(D_C // LANES, 1, LANES)
    bpt = tm // seg if tiles_per_batch == 1 else 1
    batch_blk = (lambda i: i // tiles_per_batch) if tiles_per_batch > 1 else (lambda i: i)
    state_spec = pl.BlockSpec((bpt, CONV_HALO, tn), lambda i, j: (batch_blk(i), 0, j))
    body = functools.partial(_norm_glu_conv_kernel, seg=seg, tiles_per_batch=tiles_per_batch)
    return pl.pallas_call(
        body,
        grid=(m // tm, nb),
        in_specs=[
            pl.BlockSpec((tm, d), lambda i, j: (i, 0)),
            pl.BlockSpec((1, d), lambda i, j: (0, 0)),
            pl.BlockSpec((None, d, tn), lambda i, j: (j, 0, 0)),
            pl.BlockSpec((None, d, tn), lambda i, j: (j + nb, 0, 0)),
            pl.BlockSpec((None, d, tn), lambda i, j: (j + 2 * nb, 0, 0)),
            state_spec,
            pl.BlockSpec((tn // LANES, CONV_C, SUBLANES, LANES), lambda i, j: (j, 0, 0, 0)),
            pl.BlockSpec((tn // LANES, 1, LANES), lambda i, j: (j, 0, 0)),
        ],
        out_specs=[
            pl.BlockSpec((tm, tn), lambda i, j: (i, j)),
            pl.BlockSpec((tm, tn), lambda i, j: (i, j)),
            pl.BlockSpec((tm // seg, CONV_HALO, tn), lambda i, j: (i, 0, j)),
        ],
        out_shape=[
            jax.ShapeDtypeStruct((m, D_C), F32),
            jax.ShapeDtypeStruct((m, D_C), BF16),
            jax.ShapeDtypeStruct((m // seg, CONV_HALO, D_C), F32),
        ],
        scratch_shapes=[
            pltpu.VMEM((tm, d), BF16),
            pltpu.VMEM(((tm // seg) * (CONV_HALO + seg), tn), F32),
            pltpu.VMEM((nb, CONV_HALO, tn), F32),
        ],
        compiler_params=_cparams(("arbitrary", "arbitrary")),
        name="norm_in_proj_glu_conv",
    )(x, g, w, w, w, state, dw_w, dw_b)


def _attend_heads(q_ref, kcat_ref, vcat_ref, bias_ref, o_ref, q_rows, k_rows, kpos0):
    heads = [slice(h * HEAD_DIM, (h + 1) * HEAD_DIM) for h in range(N_HEADS)]
    scores = []
    for cols in heads:
        q = q_ref[q_rows, cols].astype(BF16)
        scores.append(lax.dot_general(q, kcat_ref[k_rows, cols], (((1,), (1,)), ((), ())),
                                      preferred_element_type=F32))
    if kpos0 is not None:
        col = lax.broadcasted_iota(jnp.int32, (1, scores[0].shape[1]), 1)
        before_start = jnp.where(col + kpos0 >= 0, 0.0, -jnp.inf)
    probs = []
    for h, s in enumerate(scores):
        s = s * (HEAD_DIM ** -0.5) + bias_ref[h]
        if kpos0 is not None:
            s = s + before_start
        m = jnp.max(s, axis=-1, keepdims=True)
        p = jnp.exp(s - m)
        probs.append((p.astype(BF16), jnp.sum(p, axis=-1, keepdims=True)))
    for cols, (p, l) in zip(heads, probs):
        o = jnp.dot(p, vcat_ref[k_rows, cols], preferred_element_type=F32) / l
        o_ref[q_rows, cols] = o.astype(o_ref.dtype)


def _heads_to_rows(x_ref):
    heads = [x_ref[:, h * HEAD_DIM:(h + 1) * HEAD_DIM] for h in range(N_HEADS)]
    return jnp.swapaxes(jnp.stack(heads, axis=0), 0, 1)


def _expand_rel_bias(fvec_ref, bias_ref, n_valid, chunk_masks):
    tq = bias_ref.shape[1]
    i = lax.broadcasted_iota(jnp.int32, (tq, K_WIN), 0)
    j = lax.broadcasted_iota(jnp.int32, (tq, K_WIN), 1)
    if chunk_masks:
        lo = jnp.where(i < CHUNK, 0, CHUNK)
        hi = jnp.where(i < CHUNK, min(BAND_PAST + CHUNK, n_valid), n_valid)
        hidden = (j < lo) | (j >= hi)
    else:
        hidden = j >= n_valid
    for h in range(N_HEADS):
        rows = jnp.broadcast_to(fvec_ref[h:h + 1, :], (tq, Q_TILE + K_WIN))
        skew = pltpu.roll(rows, 0, 1, stride=1, stride_axis=0)
        bias_ref[h] = jnp.where(hidden, -jnp.inf, skew[:, Q_TILE:])


def _attn_prompt_kernel(q_ref, kp_ref, kc_ref, vp_ref, vc_ref, kf_ref, vf_ref, fvec_ref, o_ref, nk_ref, nv_ref,
                        kcat_ref, vcat_ref, bias_ref):
    b = pl.program_id(0)
    t = pl.program_id(1)

    @pl.when((b == 0) & (t == 0))
    def _():
        _expand_rel_bias(fvec_ref, bias_ref, K_WIN, True)

    @pl.when(t == pl.num_programs(1) - 1)
    def _():
        nk_ref[...] = _heads_to_rows(kf_ref)
        nv_ref[...] = _heads_to_rows(vf_ref)

    kcat_ref[0:ATT_BLOCK, :] = kp_ref[...]
    kcat_ref[ATT_BLOCK:, :] = kc_ref[...]
    vcat_ref[0:ATT_BLOCK, :] = vp_ref[...]
    vcat_ref[ATT_BLOCK:, :] = vc_ref[...]
    for r0 in range(0, ATT_BLOCK, Q_TILE):
        kpos0 = t * ATT_BLOCK + r0 - BAND_PAST
        _attend_heads(q_ref, kcat_ref, vcat_ref, bias_ref, o_ref,
                      slice(r0, r0 + Q_TILE), slice(r0, r0 + K_WIN), kpos0)


def _attn_prompt(u16, u32, fvec):
    _, b, s, _ = u16.shape
    blk = (None, ATT_BLOCK, D_A)
    blk16 = (None, None, ATT_BLOCK, D_A)
    last = s // ATT_BLOCK - 1
    prev = lambda col: (lambda bi, t: (col, bi, jnp.maximum(t - 1, 0), 0))
    cur = lambda col: (lambda bi, t: (col, bi, t, 0))
    newest = lambda col: (lambda bi, t: (bi, last, col))
    state = pl.BlockSpec((None, ATT_BLOCK, N_HEADS, HEAD_DIM), lambda bi, t: (bi, 0, 0, 0))
    state_shape = jax.ShapeDtypeStruct((b, ATT_BLOCK, N_HEADS, HEAD_DIM), F32)
    return pl.pallas_call(
        _attn_prompt_kernel,
        grid=(b, s // ATT_BLOCK),
        in_specs=[
            pl.BlockSpec(blk16, cur(U_Q)),
            pl.BlockSpec(blk16, prev(U_K)),
            pl.BlockSpec(blk16, cur(U_K)),
            pl.BlockSpec(blk16, prev(U_V)),
            pl.BlockSpec(blk16, cur(U_V)),
            pl.BlockSpec(blk, newest(U32_K)),
            pl.BlockSpec(blk, newest(U32_V)),
            pl.BlockSpec(fvec.shape, lambda bi, t: (0, 0)),
        ],
        out_specs=[pl.BlockSpec(blk, lambda bi, t: (bi, t, 0)), state, state],
        out_shape=[jax.ShapeDtypeStruct((b, s, D_A), BF16), state_shape, state_shape],
        scratch_shapes=[
            pltpu.VMEM((2 * ATT_BLOCK, D_A), BF16),
            pltpu.VMEM((2 * ATT_BLOCK, D_A), BF16),
            pltpu.VMEM((N_HEADS, Q_TILE, K_WIN), F32),
        ],
        compiler_params=_cparams(("arbitrary", "arbitrary")),
        name="attn_prompt",
    )(u16, u16, u16, u16, u16, u32, u32, fvec)


def _attn_sample_kernel(q_ref, kn_ref, vn_ref, kc_ref, vc_ref, fvec_ref, o_ref, nk_ref, nv_ref,
                        kcat_ref, vcat_ref, bias_ref):
    n_streams, tq, _ = q_ref.shape
    lc = kc_ref.shape[1]

    @pl.when(pl.program_id(0) == 0)
    def _():
        _expand_rel_bias(fvec_ref, bias_ref, lc + tq, False)
        kcat_ref[lc + tq:, :] = jnp.zeros((K_WIN - lc - tq, D_A), BF16)
        vcat_ref[lc + tq:, :] = jnp.zeros((K_WIN - lc - tq, D_A), BF16)

    for s in range(n_streams):
        kcat_ref[lc:lc + tq, :] = kn_ref[s].astype(BF16)
        vcat_ref[lc:lc + tq, :] = vn_ref[s].astype(BF16)
        nk_ref[s] = _heads_to_rows(kn_ref.at[s])
        nv_ref[s] = _heads_to_rows(vn_ref.at[s])
        kch = jnp.swapaxes(kc_ref[s], 0, 1)
        vch = jnp.swapaxes(vc_ref[s], 0, 1)
        for h in range(N_HEADS):
            c0 = h * HEAD_DIM
            kcat_ref[0:lc, c0:c0 + HEAD_DIM] = kch[h].astype(BF16)
            vcat_ref[0:lc, c0:c0 + HEAD_DIM] = vch[h].astype(BF16)
        _attend_heads(q_ref.at[s], kcat_ref, vcat_ref, bias_ref, o_ref.at[s], slice(None), slice(None), None)


def _attn_sample(u16, u32, k_cache, v_cache, fvec):
    _, b, t, _ = u16.shape
    lc = k_cache.shape[1]
    assert lc + t <= K_WIN and t <= Q_TILE
    ns = SAMPLE_STREAMS_PER_STEP if b % SAMPLE_STREAMS_PER_STEP == 0 else 1
    new = lambda col: pl.BlockSpec((ns, t, D_A), lambda bi: (bi, 0, col))
    cache = pl.BlockSpec((ns, lc, N_HEADS, HEAD_DIM), lambda bi: (bi, 0, 0, 0))
    state = pl.BlockSpec((ns, t, N_HEADS, HEAD_DIM), lambda bi: (bi, 0, 0, 0))
    state_shape = jax.ShapeDtypeStruct((b, t, N_HEADS, HEAD_DIM), F32)
    return pl.pallas_call(
        _attn_sample_kernel,
        grid=(b // ns,),
        in_specs=[pl.BlockSpec((None, ns, t, D_A), lambda bi: (U_Q, bi, 0, 0)), new(U32_K), new(U32_V), cache, cache,
                  pl.BlockSpec(fvec.shape, lambda bi: (0, 0))],
        out_specs=[pl.BlockSpec((ns, t, D_A), lambda bi: (bi, 0, 0)), state, state],
        out_shape=[jax.ShapeDtypeStruct((b, t, D_A), BF16), state_shape, state_shape],
        scratch_shapes=[
            pltpu.VMEM((K_WIN, D_A), BF16),
            pltpu.VMEM((K_WIN, D_A), BF16),
            pltpu.VMEM((N_HEADS, t, K_WIN), F32),
        ],
        compiler_params=_cparams(("arbitrary",)),
        name="attn_sample",
    )(u16, u32, u32, k_cache, v_cache, fvec)


def _rel_bias_vector(rel_table):
    width = Q_TILE + K_WIN
    n_const = BAND_PAST + Q_TILE - REL_CLIP + 1
    n_rev = width - n_const
    assert 0 < n_rev <= 2 * REL_CLIP
    const = jnp.broadcast_to(rel_table[:, 2 * REL_CLIP:], (N_HEADS, n_const))
    rev = lax.rev(rel_table[:, 2 * REL_CLIP - n_rev:2 * REL_CLIP], (1,))
    return jnp.concatenate([const, rev], axis=1).astype(F32)


def _log_sigmoid(x):
    return -(jnp.maximum(-x, 0.0) + jnp.log1p(jnp.exp(-jnp.abs(x))))


def _shift_rows(x, d, fill, block):
    n, c = x.shape
    if d % SUBLANES == 0:
        x3 = x.reshape(n // block, block, c)
        pad = jnp.full((n // block, d, c), fill, x.dtype)
        return jnp.concatenate([pad, x3[:, :block - d]], axis=1).reshape(n, c)
    rolled = pltpu.roll(x, d, 0)
    row = lax.broadcasted_iota(jnp.int32, x.shape, 0)
    return jnp.where(row % block < d, fill, rolled)


def _rglru_kernel(xb_ref, lb_ref, h0_ref, cw_ref, cb_ref, wa_ref, ba_ref, wx_ref, bx_ref, lam_ref,
                  h_ref, hl_ref, xext_ref, hc_ref):
    t = pl.program_id(1)
    tt = xb_ref.shape[0]

    @pl.when(t == 0)
    def _():
        xext_ref[0:LRU_HALO, :] = lb_ref[...]
        hc_ref[...] = h0_ref[...]

    xext_ref[LRU_HALO:, :] = xb_ref[...]
    xc = cb_ref[...] + cw_ref[CONV_B - 1:CONV_B, :] * xb_ref[...]
    win = xext_ref[...]
    for k in range(CONV_B - 1):
        off = LRU_HALO - (CONV_B - 1) + k
        xc = xc + pltpu.roll(cw_ref[k:k + 1, :] * win, tt + LRU_HALO - off, 0)[:tt]
    xext_ref[0:LRU_HALO, :] = xext_ref[tt:tt + LRU_HALO, :]

    xcb = xc.astype(BF16)
    ra, rx = [], []
    for n in range(N_BLOCKS_B):
        blk = xcb[:, n * BLOCK_B:(n + 1) * BLOCK_B]
        ra.append(jnp.dot(blk, wa_ref[n], preferred_element_type=F32))
        rx.append(jnp.dot(blk, wx_ref[n], preferred_element_type=F32))
    r = _sigmoid(jnp.concatenate(ra, axis=-1) + ba_ref[...])
    i = _sigmoid(jnp.concatenate(rx, axis=-1) + bx_ref[...])
    log_a = LRU_C * r * _log_sigmoid(lam_ref[...])
    a = jnp.exp(log_a)
    th = jnp.tanh(log_a)
    bt = jnp.sqrt(-2.0 * th / (1.0 - th)) * (i * xc)

    block = min(tt, LRU_SCAN_ROWS)
    d = 1
    while d < block:
        bt = bt + a * _shift_rows(bt, d, 0.0, block)
        a = a * _shift_rows(a, d, 1.0, block)
        d *= 2
    carry = hc_ref[...]
    for r0 in range(0, tt, block):
        h = a[r0:r0 + block] * carry + bt[r0:r0 + block]
        h_ref[r0:r0 + block, :] = h.astype(h_ref.dtype)
        carry = h[block - 1:block, :]
    hc_ref[...] = carry
    hl_ref[...] = carry


def _rglru(u32, lb0, h0, cw, cb, w_a, b_a, w_x, b_x, lam, *, tt):
    b, t, _ = u32.shape
    row = lambda a: a.reshape(1, D_B)
    full = lambda a: pl.BlockSpec(a.shape, lambda bi, ti: (0,) * a.ndim)
    args = (cw, row(cb), w_a, row(b_a), w_x, row(b_x), row(lam))
    return pl.pallas_call(
        _rglru_kernel,
        grid=(b, t // tt),
        in_specs=[
            pl.BlockSpec((None, tt, D_B), lambda bi, ti: (bi, ti, U32_XB)),
            pl.BlockSpec((None, LRU_HALO, D_B), lambda bi, ti: (bi, 0, 0)),
            pl.BlockSpec((None, 1, D_B), lambda bi, ti: (bi, 0, 0)),
        ] + [full(a) for a in args],
        out_specs=[
            pl.BlockSpec((None, tt, D_B), lambda bi, ti: (bi, ti, 0)),
            pl.BlockSpec((None, 1, D_B), lambda bi, ti: (bi, 0, 0)),
        ],
        out_shape=[jax.ShapeDtypeStruct((b, t, D_B), BF16), jax.ShapeDtypeStruct((b, 1, D_B), F32)],
        scratch_shapes=[pltpu.VMEM((LRU_HALO + tt, D_B), F32), pltpu.VMEM((1, D_B), F32)],
        compiler_params=_cparams(("parallel", "arbitrary")),
        name="rglru",
    )(u32, lb0, h0, *args)


def _gated_out_kernel(oa_ref, ga_ref, h_ref, gb_ref, x_ref, w_ref, o_ref):
    ma = (_silu(ga_ref[...].astype(F32)) * oa_ref[...].astype(F32)).astype(BF16)
    mb = (_silu(gb_ref[...].astype(F32)) * h_ref[...].astype(F32)).astype(BF16)
    acc = jnp.dot(ma, w_ref[0:D_A, :], preferred_element_type=F32)
    acc = acc + jnp.dot(mb, w_ref[D_A:, :], preferred_element_type=F32)
    o_ref[...] = x_ref[...] + acc


def _gated_out(o_a, u16, h, x, w, *, tm):
    m = x.shape[0]
    return pl.pallas_call(
        _gated_out_kernel,
        grid=(m // tm,),
        in_specs=[
            pl.BlockSpec((tm, D_A), lambda i: (i, 0)),
            pl.BlockSpec((None, tm, D_A), lambda i: (U_GA, i, 0)),
            pl.BlockSpec((tm, D_B), lambda i: (i, 0)),
            pl.BlockSpec((None, tm, D_B), lambda i: (U_GB, i, 0)),
            pl.BlockSpec((tm, D_MODEL), lambda i: (i, 0)),
            pl.BlockSpec(w.shape, lambda i: (0, 0), pipeline_mode=pl.Buffered(1)),
        ],
        out_specs=pl.BlockSpec((tm, D_MODEL), lambda i: (i, 0)),
        out_shape=jax.ShapeDtypeStruct((m, D_MODEL), F32),
        compiler_params=_cparams(("parallel",)),
        name="gated_out_proj",
    )(o_a, u16, h, u16, x, w)


def _ln_out_norm_kernel(cv_ref, sg_ref, lg_ref, lb_ref, w_ref, x_ref, g_ref, o_ref):
    acc = cv_ref[...]
    mu = jnp.mean(acc, axis=-1, keepdims=True)
    cen = acc - mu
    var = jnp.mean(cen * cen, axis=-1, keepdims=True)
    yn = cen * lax.rsqrt(var + EPS) * lg_ref[...] + lb_ref[...]
    y = (_silu(yn) * sg_ref[...].astype(F32)).astype(BF16)
    x = x_ref[...] + jnp.dot(y, w_ref[...], preferred_element_type=F32)
    ms = jnp.mean(x * x, axis=-1, keepdims=True)
    o_ref[...] = x * lax.rsqrt(ms + EPS) * g_ref[...]


def _ln_out_norm(cv, sg, ln_g, ln_b, w, x, g, *, tm):
    m = x.shape[0]
    row = pl.BlockSpec((1, D_C), lambda i: (0, 0))
    return pl.pallas_call(
        _ln_out_norm_kernel,
        grid=(m // tm,),
        in_specs=[
            pl.BlockSpec((tm, D_C), lambda i: (i, 0)),
            pl.BlockSpec((tm, D_C), lambda i: (i, 0)),
            row,
            row,
            pl.BlockSpec(w.shape, lambda i: (0, 0), pipeline_mode=pl.Buffered(1)),
            pl.BlockSpec((tm, D_MODEL), lambda i: (i, 0)),
            pl.BlockSpec((1, D_MODEL), lambda i: (0, 0)),
        ],
        out_specs=pl.BlockSpec((tm, D_MODEL), lambda i: (i, 0)),
        out_shape=jax.ShapeDtypeStruct((m, D_MODEL), F32),
        compiler_params=_cparams(("parallel",)),
        name="ln_out_proj_final_norm",
    )(cv, sg, ln_g.reshape(1, D_C), ln_b.reshape(1, D_C), w, x, g)


def _front_pad_rows(a, rows):
    return jnp.pad(a, ((0, 0), (rows - a.shape[1], 0), (0, 0)))


def _trunk(x, caches, w, *, tm, tm_out, lru_tt):
    b, t, _ = x.shape
    m = b * t
    x2d = x.reshape(m, D_MODEL)

    u16, u32 = _norm_matmul(x2d, w["norm_ab"], w["w_in_ab"], tm=tm)
    u16_3 = u16.reshape(IN_AB // D_A, b, t, D_A)
    u32_3 = u32.reshape(b, t, 3 * D_A)
    if caches is None:
        assert t % ATT_BLOCK == 0 and ATT_BLOCK == BAND_PAST
        o_a, new_k, new_v = _attn_prompt(u16_3, u32_3, w["rel_bias_vec"])
        h0 = jnp.zeros((b, 1, D_B), F32)
        lb0 = jnp.zeros((b, LRU_HALO, D_B), F32)
        cb0 = jnp.zeros((b, CONV_HALO, D_C), F32)
    else:
        kc, vc, hc, lbc, cbc = caches
        o_a, new_k, new_v = _attn_sample(u16_3, u32_3, kc, vc, w["rel_bias_vec"])
        h0 = hc.reshape(b, 1, D_B)
        lb0 = _front_pad_rows(lbc, LRU_HALO)
        cb0 = _front_pad_rows(cbc, CONV_HALO)
    h, h_last = _rglru(u32_3, lb0, h0, w["lru_conv_w"], w["lru_conv_b"], w["lru_w_a"], w["lru_b_a"],
                       w["lru_w_x"], w["lru_b_x"], w["lru_lambda"], tt=lru_tt)
    x1 = _gated_out(o_a.reshape(m, D_A), u16, h.reshape(m, D_B), x2d, w["w_out_ab"], tm=tm_out)

    new_h = h_last[:, 0]
    new_lb = u32_3[:, t - (CONV_B - 1):, U32_XB * D_B:(U32_XB + 1) * D_B]

    cv, sg, z_tail = _norm_glu_conv(x1, w["norm_cv"], w["w_in_cv"], cb0, w["dw_w"], w["dw_b"], t=t, tm=tm, tn=CV_TN)
    out = _ln_out_norm(cv, sg, w["ln_g"], w["ln_b"], w["w_out_cv"], x1, w["final_norm"], tm=tm_out)
    segs_per_batch = z_tail.shape[0] // b
    new_cb = z_tail[segs_per_batch - 1::segs_per_batch, CONV_HALO - (CONV_C - 1):]

    return (out.reshape(b, t, D_MODEL), new_k[None], new_v[None], new_h[None], new_lb[None], new_cb[None])


def kernel(x_prompt, x_sample, cache_attn_k, cache_attn_v, state_lru_h, state_lru_conv, state_conv, norm_ab, w_in_ab, w_out_ab, rel_bias, lru_conv_w, lru_conv_b, lru_w_a, lru_b_a, lru_w_x, lru_b_x, lru_lambda, norm_cv, w_in_cv, w_out_cv, dw_w, dw_b, ln_g, ln_b, final_norm):
    assert norm_ab.shape[0] == 1 and norm_cv.shape[0] == 1, "one even and one odd layer"
    t_s = x_sample.shape[1]
    w = {
        "norm_ab": norm_ab[0].reshape(1, D_MODEL),
        "w_in_ab": _column_blocks(w_in_ab[0].astype(BF16), D_A),
        "w_out_ab": w_out_ab[0].astype(BF16),
        "rel_bias_vec": _rel_bias_vector(rel_bias[0]),
        "lru_conv_w": lru_conv_w[0],
        "lru_conv_b": lru_conv_b[0],
        "lru_w_a": lru_w_a[0].astype(BF16),
        "lru_b_a": lru_b_a[0],
        "lru_w_x": lru_w_x[0].astype(BF16),
        "lru_b_x": lru_b_x[0],
        "lru_lambda": lru_lambda[0],
        "norm_cv": norm_cv[0].reshape(1, D_MODEL),
        "w_in_cv": _column_blocks(w_in_cv[0].astype(BF16), CV_TN),
        "w_out_cv": w_out_cv[0].astype(BF16),
        "dw_w": dw_w[0],
        "dw_b": dw_b[0],
        "ln_g": ln_g[0],
        "ln_b": ln_b[0],
        "final_norm": final_norm.reshape(1, D_MODEL),
    }
    y_p, k_p, v_p, h_p, lb_p, cb_p = _trunk(x_prompt, None, w, tm=1024, tm_out=512, lru_tt=512)
    caches = (cache_attn_k[0], cache_attn_v[0], state_lru_h[0], state_lru_conv[0], state_conv[0])
    y_s, k_s, v_s, h_s, lb_s, cb_s = _trunk(x_sample, caches, w, tm=512, tm_out=256, lru_tt=t_s)
    return (y_p, y_s, k_p, v_p, h_p, lb_p, cb_p, k_s, v_s, h_s, lb_s, cb_s)
```

```python
import functools

import jax
import jax.numpy as jnp
from jax import lax
from jax.experimental import pallas as pl
from jax.experimental.pallas import tpu as pltpu

D_MODEL = 2048
CHUNK = 64
LEFT_CHUNKS = 8
BAND_PAST = CHUNK * LEFT_CHUNKS
N_HEADS = 8
HEAD_DIM = 128
D_A = N_HEADS * HEAD_DIM
REL_CLIP = 128
D_B = 1024
N_BLOCKS_B = 8
BLOCK_B = D_B // N_BLOCKS_B
CONV_B = 4
LRU_C = 8.0
D_C = D_MODEL
CONV_C = 31
IN_AB = 4 * D_A + 2 * D_B
IN_CV = 3 * D_C
EPS = 1e-6

F32 = jnp.float32
BF16 = jnp.bfloat16

SUBLANES = 8
LANES = 128
Q_TILE = 2 * CHUNK
K_WIN = BAND_PAST + Q_TILE
ATT_BLOCK = 512
W_SLOTS = 3
SAMPLE_STREAMS_PER_STEP = 2
CONV_HALO = 32
CONV_ROWS = 128
LRU_SCAN_ROWS = 32
LRU_HALO = 8
VMEM_LIMIT = 56 * 1024 * 1024


def _cparams(sem):
    return pltpu.CompilerParams(dimension_semantics=sem, vmem_limit_bytes=VMEM_LIMIT)


def _sigmoid(x):
    return 0.5 * jnp.tanh(0.5 * x) + 0.5


def _silu(x):
    half = 0.5 * x
    return half + half * jnp.tanh(half)


def _norm_to_bf16(x, g):
    ms = jnp.mean(x * x, axis=-1, keepdims=True)
    return (x * lax.rsqrt(ms + EPS) * g).astype(BF16)


U_Q, U_K, U_V, U_GA, U_XB, U_GB = range(6)
U32_K, U32_V, U32_XB = U_K // 2, U_V // 2, U_XB // 2


def _norm_matmul_kernel(x_ref, g_ref, w_hbm, o16_ref, o32_ref, xn_ref, wbuf_ref, wsem_ref):
    i = pl.program_id(0)
    j = pl.program_id(1)
    nj = pl.num_programs(1)
    step = i * nj + j
    n_steps = pl.num_programs(0) * nj

    def w_copy(s):
        col = pl.multiple_of(lax.rem(s, nj) * D_A, D_A)
        slot = lax.rem(s, W_SLOTS)
        return pltpu.make_async_copy(w_hbm.at[:, pl.ds(col, D_A)], wbuf_ref.at[slot], wsem_ref.at[slot])

    @pl.when(step == 0)
    def _():
        for s in range(W_SLOTS - 1):
            w_copy(s).start()

    @pl.when(step + W_SLOTS - 1 < n_steps)
    def _():
        w_copy(step + W_SLOTS - 1).start()

    @pl.when(j == 0)
    def _():
        xn_ref[...] = _norm_to_bf16(x_ref[...], g_ref[...])

    w_copy(step).wait()
    acc = jnp.dot(xn_ref[...], wbuf_ref[lax.rem(step, W_SLOTS)], preferred_element_type=F32)
    o16_ref[...] = acc.astype(BF16)

    @pl.when((j == U_K) | (j == U_V) | (j == U_XB))
    def _():
        o32_ref[...] = acc


def _norm_matmul(x, g, w, *, tm):
    m, d = x.shape
    assert D_A == D_B and w.shape[1] == IN_AB and (m // tm) * (IN_AB // D_A) >= W_SLOTS - 1
    return pl.pallas_call(
        _norm_matmul_kernel,
        grid=(m // tm, IN_AB // D_A),
        in_specs=[
            pl.BlockSpec((tm, d), lambda i, j: (i, 0)),
            pl.BlockSpec((1, d), lambda i, j: (0, 0)),
            pl.BlockSpec(memory_space=pl.ANY),
        ],
        out_specs=[pl.BlockSpec((tm, D_A), lambda i, j: (i, j)), pl.BlockSpec((tm, D_A), lambda i, j: (i, j // 2))],
        out_shape=[jax.ShapeDtypeStruct((m, IN_AB), BF16), jax.ShapeDtypeStruct((m, 3 * D_A), F32)],
        scratch_shapes=[
            pltpu.VMEM((tm, d), BF16),
            pltpu.VMEM((W_SLOTS, d, D_A), BF16),
            pltpu.SemaphoreType.DMA((W_SLOTS,)),
        ],
        compiler_params=_cparams(("arbitrary", "arbitrary")),
        name="norm_in_proj",
    )(x, g, w)


def _dwconv_rows(zext_ref, base, rows, lane0, w_ref, b_ref):
    lanes = slice(lane0, lane0 + LANES)
    acc = jnp.broadcast_to(b_ref[...], (rows, LANES))
    for phase in range(SUBLANES):
        win_rows = rows if phase == 0 else rows + SUBLANES
        part = None
        for k in range(CONV_C):
            off = base + CONV_HALO - (CONV_C - 1) + k
            if off % SUBLANES != phase:
                continue
            tap = w_ref[k]
            win = zext_ref[off - phase:off - phase + win_rows, lanes]
            term = win.reshape(win_rows // SUBLANES, SUBLANES, LANES) * tap[None]
            part = term if part is None else part + term
        if part is None:
            continue
        part = part.reshape(win_rows, LANES)
        if phase:
            part = pltpu.roll(part, win_rows - phase, 0)[:rows]
        acc = acc + part
    return acc


def _norm_glu_conv_kernel(x_ref, g_ref, wv_ref, wg_ref, wt_ref, st_ref, cw_ref, cb_ref,
                          cv_ref, sg_ref, tail_ref, xn_ref, zext_ref, carry_ref, *, seg, tiles_per_batch):
    i = pl.program_id(0)
    j = pl.program_id(1)
    n_seg = x_ref.shape[0] // seg
    ext = CONV_HALO + seg

    @pl.when(j == 0)
    def _():
        xn_ref[...] = _norm_to_bf16(x_ref[...], g_ref[...])

    xn = xn_ref[...]
    gate = jnp.dot(xn, wt_ref[...], preferred_element_type=F32)
    sg_ref[...] = _silu(gate).astype(BF16)
    val = jnp.dot(xn, wv_ref[...], preferred_element_type=F32)
    glu = jnp.dot(xn, wg_ref[...], preferred_element_type=F32)
    z = val * _sigmoid(glu)

    carried = tiles_per_batch > 1
    if carried:
        @pl.when(i % tiles_per_batch == 0)
        def _():
            carry_ref[j] = st_ref[0]

    rows = min(seg, CONV_ROWS)
    for s in range(n_seg):
        base = s * ext
        zext_ref[base:base + CONV_HALO, :] = carry_ref[j] if carried else st_ref[s]
        zext_ref[base + CONV_HALO:base + ext, :] = z[s * seg:(s + 1) * seg]
        for c in range(cw_ref.shape[0]):
            for r0 in range(0, seg, rows):
                out_rows = slice(s * seg + r0, s * seg + r0 + rows)
                cv_ref[out_rows, c * LANES:(c + 1) * LANES] = _dwconv_rows(
                    zext_ref, base + r0, rows, c * LANES, cw_ref.at[c], cb_ref.at[c])
        tail = zext_ref[base + seg:base + ext, :]
        tail_ref[s] = tail
        if carried:
            carry_ref[j] = tail


def _norm_glu_conv(x, g, w, state, dw_w, dw_b, *, t, tm, tn):
    m, d = x.shape
    nb = D_C // tn
    seg = min(t, tm)
    assert tm % seg == 0 and t % seg == 0 and seg % min(seg, CONV_ROWS) == 0 and seg % SUBLANES == 0
    tiles_per_batch = t // seg
    dw_w = dw_w.reshape(CONV_C, D_C // LANES, 1, LANES).transpose(1, 0, 2, 3)
    dw_w = jnp.broadcast_to(dw_w, (D_C // LANES, CONV_C, SUBLANES, LANES))
    dw_b = dw_b.reshape(D_C // LANES, 1, LANES)
    bpt = tm // seg if tiles_per_batch == 1 else 1
    batch_blk = (lambda i: i // tiles_per_batch) if tiles_per_batch > 1 else (lambda i: i)
    state_spec = pl.BlockSpec((bpt, CONV_HALO, tn), lambda i, j: (batch_blk(i), 0, j))
    body = functools.partial(_norm_glu_conv_kernel, seg=seg, tiles_per_batch=tiles_per_batch)
    return pl.pallas_call(
        body,
        grid=(m // tm, nb),
        in_specs=[
            pl.BlockSpec((tm, d), lambda i, j: (i, 0)),
            pl.BlockSpec((1, d), lambda i, j: (0, 0)),
            pl.BlockSpec((d, tn), lambda i, j: (0, j)),
            pl.BlockSpec((d, tn), lambda i, j: (0, j + nb)),
            pl.BlockSpec((d, tn), lambda i, j: (0, j + 2 * nb)),
            state_spec,
            pl.BlockSpec((tn // LANES, CONV_C, SUBLANES, LANES), lambda i, j: (j, 0, 0, 0)),
            pl.BlockSpec((tn // LANES, 1, LANES), lambda i, j: (j, 0, 0)),
        ],
        out_specs=[
            pl.BlockSpec((tm, tn), lambda i, j: (i, j)),
            pl.BlockSpec((tm, tn), lambda i, j: (i, j)),
            pl.BlockSpec((tm // seg, CONV_HALO, tn), lambda i, j: (i, 0, j)),
        ],
        out_shape=[
            jax.ShapeDtypeStruct((m, D_C), F32),
            jax.ShapeDtypeStruct((m, D_C), BF16),
            jax.ShapeDtypeStruct((m // seg, CONV_HALO, D_C), F32),
        ],
        scratch_shapes=[
            pltpu.VMEM((tm, d), BF16),
            pltpu.VMEM(((tm // seg) * (CONV_HALO + seg), tn), F32),
            pltpu.VMEM((nb, CONV_HALO, tn), F32),
        ],
        compiler_params=_cparams(("arbitrary", "arbitrary")),
        name="norm_in_proj_glu_conv",
    )(x, g, w, w, w, state, dw_w, dw_b)


def _attend_heads(q_ref, kcat_ref, vcat_ref, bias_ref, o_ref, q_rows, k_rows, kpos0):
    heads = [slice(h * HEAD_DIM, (h + 1) * HEAD_DIM) for h in range(N_HEADS)]
    scores = []
    for cols in heads:
        q = q_ref[q_rows, cols].astype(BF16)
        scores.append(lax.dot_general(q, kcat_ref[k_rows, cols], (((1,), (1,)), ((), ())),
                                      preferred_element_type=F32))
    if kpos0 is not None:
        col = lax.broadcasted_iota(jnp.int32, (1, scores[0].shape[1]), 1)
        before_start = jnp.where(col + kpos0 >= 0, 0.0, -jnp.inf)
    probs = []
    for h, s in enumerate(scores):
        s = s * (HEAD_DIM ** -0.5) + bias_ref[h]
        if kpos0 is not None:
            s = s + before_start
        m = jnp.max(s, axis=-1, keepdims=True)
        p = jnp.exp(s - m)
        probs.append((p.astype(BF16), jnp.sum(p, axis=-1, keepdims=True)))
    for cols, (p, l) in zip(heads, probs):
        o = jnp.dot(p, vcat_ref[k_rows, cols], preferred_element_type=F32) / l
        o_ref[q_rows, cols] = o.astype(o_ref.dtype)


def _heads_to_rows(x_ref):
    heads = [x_ref[:, h * HEAD_DIM:(h + 1) * HEAD_DIM] for h in range(N_HEADS)]
    return jnp.swapaxes(jnp.stack(heads, axis=0), 0, 1)


def _expand_rel_bias(fvec_ref, bias_ref, n_valid, chunk_masks):
    tq = bias_ref.shape[1]
    i = lax.broadcasted_iota(jnp.int32, (tq, K_WIN), 0)
    j = lax.broadcasted_iota(jnp.int32, (tq, K_WIN), 1)
    if chunk_masks:
        lo = jnp.where(i < CHUNK, 0, CHUNK)
        hi = jnp.where(i < CHUNK, min(BAND_PAST + CHUNK, n_valid), n_valid)
        hidden = (j < lo) | (j >= hi)
    else:
        hidden = j >= n_valid
    for h in range(N_HEADS):
        rows = jnp.broadcast_to(fvec_ref[h:h + 1, :], (tq, Q_TILE + K_WIN))
        skew = pltpu.roll(rows, 0, 1, stride=1, stride_axis=0)
        bias_ref[h] = jnp.where(hidden, -jnp.inf, skew[:, Q_TILE:])


def _attn_prompt_kernel(q_ref, kp_ref, kc_ref, vp_ref, vc_ref, kf_ref, vf_ref, fvec_ref, o_ref, nk_ref, nv_ref,
                        kcat_ref, vcat_ref, bias_ref):
    b = pl.program_id(0)
    t = pl.program_id(1)

    @pl.when((b == 0) & (t == 0))
    def _():
        _expand_rel_bias(fvec_ref, bias_ref, K_WIN, True)

    @pl.when(t == pl.num_programs(1) - 1)
    def _():
        nk_ref[...] = _heads_to_rows(kf_ref)
        nv_ref[...] = _heads_to_rows(vf_ref)

    kcat_ref[0:ATT_BLOCK, :] = kp_ref[...]
    kcat_ref[ATT_BLOCK:, :] = kc_ref[...]
    vcat_ref[0:ATT_BLOCK, :] = vp_ref[...]
    vcat_ref[ATT_BLOCK:, :] = vc_ref[...]
    for r0 in range(0, ATT_BLOCK, Q_TILE):
        kpos0 = t * ATT_BLOCK + r0 - BAND_PAST
        _attend_heads(q_ref, kcat_ref, vcat_ref, bias_ref, o_ref,
                      slice(r0, r0 + Q_TILE), slice(r0, r0 + K_WIN), kpos0)


def _attn_prompt(u16, u32, fvec):
    b, s, _ = u16.shape
    blk = (None, ATT_BLOCK, D_A)
    last = s // ATT_BLOCK - 1
    prev = lambda col: (lambda bi, t: (bi, jnp.maximum(t - 1, 0), col))
    cur = lambda col: (lambda bi, t: (bi, t, col))
    newest = lambda col: (lambda bi, t: (bi, last, col))
    state = pl.BlockSpec((None, ATT_BLOCK, N_HEADS, HEAD_DIM), lambda bi, t: (bi, 0, 0, 0))
    state_shape = jax.ShapeDtypeStruct((b, ATT_BLOCK, N_HEADS, HEAD_DIM), F32)
    return pl.pallas_call(
        _attn_prompt_kernel,
        grid=(b, s // ATT_BLOCK),
        in_specs=[
            pl.BlockSpec(blk, cur(U_Q)),
            pl.BlockSpec(blk, prev(U_K)),
            pl.BlockSpec(blk, cur(U_K)),
            pl.BlockSpec(blk, prev(U_V)),
            pl.BlockSpec(blk, cur(U_V)),
            pl.BlockSpec(blk, newest(U32_K)),
            pl.BlockSpec(blk, newest(U32_V)),
            pl.BlockSpec(fvec.shape, lambda bi, t: (0, 0)),
        ],
        out_specs=[pl.BlockSpec(blk, cur(0)), state, state],
        out_shape=[jax.ShapeDtypeStruct((b, s, D_A), BF16), state_shape, state_shape],
        scratch_shapes=[
            pltpu.VMEM((2 * ATT_BLOCK, D_A), BF16),
            pltpu.VMEM((2 * ATT_BLOCK, D_A), BF16),
            pltpu.VMEM((N_HEADS, Q_TILE, K_WIN), F32),
        ],
        compiler_params=_cparams(("arbitrary", "arbitrary")),
        name="attn_prompt",
    )(u16, u16, u16, u16, u16, u32, u32, fvec)


def _attn_sample_kernel(q_ref, kn_ref, vn_ref, kc_ref, vc_ref, fvec_ref, o_ref, nk_ref, nv_ref,
                        kcat_ref, vcat_ref, bias_ref):
    n_streams, tq, _ = q_ref.shape
    lc = kc_ref.shape[1]

    @pl.when(pl.program_id(0) == 0)
    def _():
        _expand_rel_bias(fvec_ref, bias_ref, lc + tq, False)
        kcat_ref[lc + tq:, :] = jnp.zeros((K_WIN - lc - tq, D_A), BF16)
        vcat_ref[lc + tq:, :] = jnp.zeros((K_WIN - lc - tq, D_A), BF16)

    for s in range(n_streams):
        kcat_ref[lc:lc + tq, :] = kn_ref[s].astype(BF16)
        vcat_ref[lc:lc + tq, :] = vn_ref[s].astype(BF16)
        nk_ref[s] = _heads_to_rows(kn_ref.at[s])
        nv_ref[s] = _heads_to_rows(vn_ref.at[s])
        kch = jnp.swapaxes(kc_ref[s], 0, 1)
        vch = jnp.swapaxes(vc_ref[s], 0, 1)
        for h in range(N_HEADS):
            c0 = h * HEAD_DIM
            kcat_ref[0:lc, c0:c0 + HEAD_DIM] = kch[h].astype(BF16)
            vcat_ref[0:lc, c0:c0 + HEAD_DIM] = vch[h].astype(BF16)
        _attend_heads(q_ref.at[s], kcat_ref, vcat_ref, bias_ref, o_ref.at[s], slice(None), slice(None), None)


def _attn_sample(u16, u32, k_cache, v_cache, fvec):
    b, t, _ = u16.shape
    lc = k_cache.shape[1]
    assert lc + t <= K_WIN and t <= Q_TILE
    ns = SAMPLE_STREAMS_PER_STEP if b % SAMPLE_STREAMS_PER_STEP == 0 else 1
    new = lambda col: pl.BlockSpec((ns, t, D_A), lambda bi: (bi, 0, col))
    cache = pl.BlockSpec((ns, lc, N_HEADS, HEAD_DIM), lambda bi: (bi, 0, 0, 0))
    state = pl.BlockSpec((ns, t, N_HEADS, HEAD_DIM), lambda bi: (bi, 0, 0, 0))
    state_shape = jax.ShapeDtypeStruct((b, t, N_HEADS, HEAD_DIM), F32)
    return pl.pallas_call(
        _attn_sample_kernel,
        grid=(b // ns,),
        in_specs=[new(U_Q), new(U32_K), new(U32_V), cache, cache, pl.BlockSpec(fvec.shape, lambda bi: (0, 0))],
        out_specs=[pl.BlockSpec((ns, t, D_A), lambda bi: (bi, 0, 0)), state, state],
        out_shape=[jax.ShapeDtypeStruct((b, t, D_A), BF16), state_shape, state_shape],
        scratch_shapes=[
            pltpu.VMEM((K_WIN, D_A), BF16),
            pltpu.VMEM((K_WIN, D_A), BF16),
            pltpu.VMEM((N_HEADS, t, K_WIN), F32),
        ],
        compiler_params=_cparams(("arbitrary",)),
        name="attn_sample",
    )(u16, u32, u32, k_cache, v_cache, fvec)


def _rel_bias_vector(rel_table):
    width = Q_TILE + K_WIN
    n_const = BAND_PAST + Q_TILE - REL_CLIP + 1
    n_rev = width - n_const
    assert 0 < n_rev <= 2 * REL_CLIP
    const = jnp.broadcast_to(rel_table[:, 2 * REL_CLIP:], (N_HEADS, n_const))
    rev = lax.rev(rel_table[:, 2 * REL_CLIP - n_rev:2 * REL_CLIP], (1,))
    return jnp.concatenate([const, rev], axis=1).astype(F32)


def _log_sigmoid(x):
    return -(jnp.maximum(-x, 0.0) + jnp.log1p(jnp.exp(-jnp.abs(x))))


def _shift_rows(x, d, fill, block):
    n, c = x.shape
    if d % SUBLANES == 0:
        x3 = x.reshape(n // block, block, c)
        pad = jnp.full((n // block, d, c), fill, x.dtype)
        return jnp.concatenate([pad, x3[:, :block - d]], axis=1).reshape(n, c)
    rolled = pltpu.roll(x, d, 0)
    row = lax.broadcasted_iota(jnp.int32, x.shape, 0)
    return jnp.where(row % block < d, fill, rolled)


def _rglru_kernel(xb_ref, lb_ref, h0_ref, cw_ref, cb_ref, wa_ref, ba_ref, wx_ref, bx_ref, lam_ref,
                  h_ref, hl_ref, xext_ref, hc_ref):
    t = pl.program_id(1)
    tt = xb_ref.shape[0]

    @pl.when(t == 0)
    def _():
        xext_ref[0:LRU_HALO, :] = lb_ref[...]
        hc_ref[...] = h0_ref[...]

    xext_ref[LRU_HALO:, :] = xb_ref[...]
    xc = cb_ref[...] + cw_ref[CONV_B - 1:CONV_B, :] * xb_ref[...]
    win = xext_ref[...]
    for k in range(CONV_B - 1):
        off = LRU_HALO - (CONV_B - 1) + k
        xc = xc + pltpu.roll(cw_ref[k:k + 1, :] * win, tt + LRU_HALO - off, 0)[:tt]
    xext_ref[0:LRU_HALO, :] = xext_ref[tt:tt + LRU_HALO, :]

    xcb = xc.astype(BF16)
    ra, rx = [], []
    for n in range(N_BLOCKS_B):
        blk = xcb[:, n * BLOCK_B:(n + 1) * BLOCK_B]
        ra.append(jnp.dot(blk, wa_ref[n], preferred_element_type=F32))
        rx.append(jnp.dot(blk, wx_ref[n], preferred_element_type=F32))
    r = _sigmoid(jnp.concatenate(ra, axis=-1) + ba_ref[...])
    i = _sigmoid(jnp.concatenate(rx, axis=-1) + bx_ref[...])
    log_a = LRU_C * r * _log_sigmoid(lam_ref[...])
    a = jnp.exp(log_a)
    th = jnp.tanh(log_a)
    bt = jnp.sqrt(-2.0 * th / (1.0 - th)) * (i * xc)

    block = min(tt, LRU_SCAN_ROWS)
    d = 1
    while d < block:
        bt = bt + a * _shift_rows(bt, d, 0.0, block)
        a = a * _shift_rows(a, d, 1.0, block)
        d *= 2
    carry = hc_ref[...]
    for r0 in range(0, tt, block):
        h = a[r0:r0 + block] * carry + bt[r0:r0 + block]
        h_ref[r0:r0 + block, :] = h.astype(h_ref.dtype)
        carry = h[block - 1:block, :]
    hc_ref[...] = carry
    hl_ref[...] = carry


def _rglru(u32, lb0, h0, cw, cb, w_a, b_a, w_x, b_x, lam, *, tt):
    b, t, _ = u32.shape
    row = lambda a: a.reshape(1, D_B)
    full = lambda a: pl.BlockSpec(a.shape, lambda bi, ti: (0,) * a.ndim)
    args = (cw, row(cb), w_a, row(b_a), w_x, row(b_x), row(lam))
    return pl.pallas_call(
        _rglru_kernel,
        grid=(b, t // tt),
        in_specs=[
            pl.BlockSpec((None, tt, D_B), lambda bi, ti: (bi, ti, U32_XB)),
            pl.BlockSpec((None, LRU_HALO, D_B), lambda bi, ti: (bi, 0, 0)),
            pl.BlockSpec((None, 1, D_B), lambda bi, ti: (bi, 0, 0)),
        ] + [full(a) for a in args],
        out_specs=[
            pl.BlockSpec((None, tt, D_B), lambda bi, ti: (bi, ti, 0)),
            pl.BlockSpec((None, 1, D_B), lambda bi, ti: (bi, 0, 0)),
        ],
        out_shape=[jax.ShapeDtypeStruct((b, t, D_B), BF16), jax.ShapeDtypeStruct((b, 1, D_B), F32)],
        scratch_shapes=[pltpu.VMEM((LRU_HALO + tt, D_B), F32), pltpu.VMEM((1, D_B), F32)],
        compiler_params=_cparams(("parallel", "arbitrary")),
        name="rglru",
    )(u32, lb0, h0, *args)


def _gated_out_kernel(oa_ref, ga_ref, h_ref, gb_ref, x_ref, w_ref, o_ref):
    ma = (_silu(ga_ref[...].astype(F32)) * oa_ref[...].astype(F32)).astype(BF16)
    mb = (_silu(gb_ref[...].astype(F32)) * h_ref[...].astype(F32)).astype(BF16)
    acc = jnp.dot(ma, w_ref[0:D_A, :], preferred_element_type=F32)
    acc = acc + jnp.dot(mb, w_ref[D_A:, :], preferred_element_type=F32)
    o_ref[...] = x_ref[...] + acc


def _gated_out(o_a, u16, h, x, w, *, tm):
    m = x.shape[0]
    return pl.pallas_call(
        _gated_out_kernel,
        grid=(m // tm,),
        in_specs=[
            pl.BlockSpec((tm, D_A), lambda i: (i, 0)),
            pl.BlockSpec((tm, D_A), lambda i: (i, U_GA)),
            pl.BlockSpec((tm, D_B), lambda i: (i, 0)),
            pl.BlockSpec((tm, D_B), lambda i: (i, U_GB)),
            pl.BlockSpec((tm, D_MODEL), lambda i: (i, 0)),
            pl.BlockSpec(w.shape, lambda i: (0, 0), pipeline_mode=pl.Buffered(1)),
        ],
        out_specs=pl.BlockSpec((tm, D_MODEL), lambda i: (i, 0)),
        out_shape=jax.ShapeDtypeStruct((m, D_MODEL), F32),
        compiler_params=_cparams(("parallel",)),
        name="gated_out_proj",
    )(o_a, u16, h, u16, x, w)


def _ln_out_norm_kernel(cv_ref, sg_ref, lg_ref, lb_ref, w_ref, x_ref, g_ref, o_ref):
    acc = cv_ref[...]
    mu = jnp.mean(acc, axis=-1, keepdims=True)
    cen = acc - mu
    var = jnp.mean(cen * cen, axis=-1, keepdims=True)
    yn = cen * lax.rsqrt(var + EPS) * lg_ref[...] + lb_ref[...]
    y = (_silu(yn) * sg_ref[...].astype(F32)).astype(BF16)
    x = x_ref[...] + jnp.dot(y, w_ref[...], preferred_element_type=F32)
    ms = jnp.mean(x * x, axis=-1, keepdims=True)
    o_ref[...] = x * lax.rsqrt(ms + EPS) * g_ref[...]


def _ln_out_norm(cv, sg, ln_g, ln_b, w, x, g, *, tm):
    m = x.shape[0]
    row = pl.BlockSpec((1, D_C), lambda i: (0, 0))
    return pl.pallas_call(
        _ln_out_norm_kernel,
        grid=(m // tm,),
        in_specs=[
            pl.BlockSpec((tm, D_C), lambda i: (i, 0)),
            pl.BlockSpec((tm, D_C), lambda i: (i, 0)),
            row,
            row,
            pl.BlockSpec(w.shape, lambda i: (0, 0), pipeline_mode=pl.Buffered(1)),
            pl.BlockSpec((tm, D_MODEL), lambda i: (i, 0)),
            pl.BlockSpec((1, D_MODEL), lambda i: (0, 0)),
        ],
        out_specs=pl.BlockSpec((tm, D_MODEL), lambda i: (i, 0)),
        out_shape=jax.ShapeDtypeStruct((m, D_MODEL), F32),
        compiler_params=_cparams(("parallel",)),
        name="ln_out_proj_final_norm",
    )(cv, sg, ln_g.reshape(1, D_C), ln_b.reshape(1, D_C), w, x, g)


def _front_pad_rows(a, rows):
    return jnp.pad(a, ((0, 0), (rows - a.shape[1], 0), (0, 0)))


def _trunk(x, caches, w, *, tm, tm_out, lru_tt):
    b, t, _ = x.shape
    m = b * t
    x2d = x.reshape(m, D_MODEL)

    u16, u32 = _norm_matmul(x2d, w["norm_ab"], w["w_in_ab"], tm=tm)
    u16_3 = u16.reshape(b, t, IN_AB)
    u32_3 = u32.reshape(b, t, 3 * D_A)
    if caches is None:
        assert t % ATT_BLOCK == 0 and ATT_BLOCK == BAND_PAST
        o_a, new_k, new_v = _attn_prompt(u16_3, u32_3, w["rel_bias_vec"])
        h0 = jnp.zeros((b, 1, D_B), F32)
        lb0 = jnp.zeros((b, LRU_HALO, D_B), F32)
        cb0 = jnp.zeros((b, CONV_HALO, D_C), F32)
    else:
        kc, vc, hc, lbc, cbc = caches
        o_a, new_k, new_v = _attn_sample(u16_3, u32_3, kc, vc, w["rel_bias_vec"])
        h0 = hc.reshape(b, 1, D_B)
        lb0 = _front_pad_rows(lbc, LRU_HALO)
        cb0 = _front_pad_rows(cbc, CONV_HALO)
    h, h_last = _rglru(u32_3, lb0, h0, w["lru_conv_w"], w["lru_conv_b"], w["lru_w_a"], w["lru_b_a"],
                       w["lru_w_x"], w["lru_b_x"], w["lru_lambda"], tt=lru_tt)
    x1 = _gated_out(o_a.reshape(m, D_A), u16, h.reshape(m, D_B), x2d, w["w_out_ab"], tm=tm_out)

    new_h = h_last[:, 0]
    new_lb = u32_3[:, t - (CONV_B - 1):, U32_XB * D_B:(U32_XB + 1) * D_B]

    cv, sg, z_tail = _norm_glu_conv(x1, w["norm_cv"], w["w_in_cv"], cb0, w["dw_w"], w["dw_b"], t=t, tm=tm, tn=512)
    out = _ln_out_norm(cv, sg, w["ln_g"], w["ln_b"], w["w_out_cv"], x1, w["final_norm"], tm=tm_out)
    segs_per_batch = z_tail.shape[0] // b
    new_cb = z_tail[segs_per_batch - 1::segs_per_batch, CONV_HALO - (CONV_C - 1):]

    return (out.reshape(b, t, D_MODEL), new_k[None], new_v[None], new_h[None], new_lb[None], new_cb[None])


def kernel(x_prompt, x_sample, cache_attn_k, cache_attn_v, state_lru_h, state_lru_conv, state_conv, norm_ab, w_in_ab, w_out_ab, rel_bias, lru_conv_w, lru_conv_b, lru_w_a, lru_b_a, lru_w_x, lru_b_x, lru_lambda, norm_cv, w_in_cv, w_out_cv, dw_w, dw_b, ln_g, ln_b, final_norm):
    assert norm_ab.shape[0] == 1 and norm_cv.shape[0] == 1, "one even and one odd layer"
    t_s = x_sample.shape[1]
    w = {
        "norm_ab": norm_ab[0].reshape(1, D_MODEL),
        "w_in_ab": w_in_ab[0].astype(BF16),
        "w_out_ab": w_out_ab[0].astype(BF16),
        "rel_bias_vec": _rel_bias_vector(rel_bias[0]),
        "lru_conv_w": lru_conv_w[0],
        "lru_conv_b": lru_conv_b[0],
        "lru_w_a": lru_w_a[0].astype(BF16),
        "lru_b_a": lru_b_a[0],
        "lru_w_x": lru_w_x[0].astype(BF16),
        "lru_b_x": lru_b_x[0],
        "lru_lambda": lru_lambda[0],
        "norm_cv": norm_cv[0].reshape(1, D_MODEL),
        "w_in_cv": w_in_cv[0].astype(BF16),
        "w_out_cv": w_out_cv[0].astype(BF16),
        "dw_w": dw_w[0],
        "dw_b": dw_b[0],
        "ln_g": ln_g[0],
        "ln_b": ln_b[0],
        "final_norm": final_norm.reshape(1, D_MODEL),
    }
    y_p, k_p, v_p, h_p, lb_p, cb_p = _trunk(x_prompt, None, w, tm=1024, tm_out=512, lru_tt=512)
    caches = (cache_attn_k[0], cache_attn_v[0], state_lru_h[0], state_lru_conv[0], state_conv[0])
    y_s, k_s, v_s, h_s, lb_s, cb_s = _trunk(x_sample, caches, w, tm=512, tm_out=256, lru_tt=t_s)
    return (y_p, y_s, k_p, v_p, h_p, lb_p, cb_p, k_s, v_s, h_s, lb_s, cb_s)
```

```python
import functools

import jax
import jax.numpy as jnp
from jax import lax
from jax.experimental import pallas as pl
from jax.experimental.pallas import tpu as pltpu

D_MODEL = 2048
CHUNK = 64
LEFT_CHUNKS = 8
BAND_PAST = CHUNK * LEFT_CHUNKS
N_HEADS = 8
HEAD_DIM = 128
D_A = N_HEADS * HEAD_DIM
REL_CLIP = 128
D_B = 1024
N_BLOCKS_B = 8
BLOCK_B = D_B // N_BLOCKS_B
CONV_B = 4
LRU_C = 8.0
D_C = D_MODEL
CONV_C = 31
IN_AB = 4 * D_A + 2 * D_B
IN_CV = 3 * D_C
EPS = 1e-6

F32 = jnp.float32
BF16 = jnp.bfloat16

SUBLANES = 8
LANES = 128
Q_TILE = 2 * CHUNK
K_WIN = BAND_PAST + Q_TILE
ATT_BLOCK = 512
W_SLOTS = 3
SAMPLE_STREAMS_PER_STEP = 2
CONV_HALO = 32
CONV_ROWS = 128
LRU_SCAN_ROWS = 32
LRU_HALO = 8
VMEM_LIMIT = 56 * 1024 * 1024


def _cparams(sem):
    return pltpu.CompilerParams(dimension_semantics=sem, vmem_limit_bytes=VMEM_LIMIT)


def _sigmoid(x):
    return 0.5 * jnp.tanh(0.5 * x) + 0.5


def _silu(x):
    half = 0.5 * x
    return half + half * jnp.tanh(half)


def _norm_to_bf16(x, g):
    ms = jnp.mean(x * x, axis=-1, keepdims=True)
    return (x * lax.rsqrt(ms + EPS) * g).astype(BF16)


U_Q, U_K, U_V, U_GA, U_XB, U_GB = range(6)
U32_K, U32_V, U32_XB = U_K // 2, U_V // 2, U_XB // 2


def _norm_matmul_kernel(x_ref, g_ref, w_hbm, o16_ref, o32_ref, xn_ref, wbuf_ref, wsem_ref):
    i = pl.program_id(0)
    j = pl.program_id(1)
    nj = pl.num_programs(1)
    step = i * nj + j
    n_steps = pl.num_programs(0) * nj

    def w_copy(s):
        col = pl.multiple_of(lax.rem(s, nj) * D_A, D_A)
        slot = lax.rem(s, W_SLOTS)
        return pltpu.make_async_copy(w_hbm.at[:, pl.ds(col, D_A)], wbuf_ref.at[slot], wsem_ref.at[slot])

    @pl.when(step == 0)
    def _():
        for s in range(W_SLOTS - 1):
            w_copy(s).start()

    @pl.when(step + W_SLOTS - 1 < n_steps)
    def _():
        w_copy(step + W_SLOTS - 1).start()

    @pl.when(j == 0)
    def _():
        xn_ref[...] = _norm_to_bf16(x_ref[...], g_ref[...])

    w_copy(step).wait()
    acc = jnp.dot(xn_ref[...], wbuf_ref[lax.rem(step, W_SLOTS)], preferred_element_type=F32)
    o16_ref[...] = acc.astype(BF16)

    @pl.when((j == U_K) | (j == U_V) | (j == U_XB))
    def _():
        o32_ref[...] = acc


def _norm_matmul(x, g, w, *, tm):
    m, d = x.shape
    assert D_A == D_B and w.shape[1] == IN_AB and (m // tm) * (IN_AB // D_A) >= W_SLOTS - 1
    return pl.pallas_call(
        _norm_matmul_kernel,
        grid=(m // tm, IN_AB // D_A),
        in_specs=[
            pl.BlockSpec((tm, d), lambda i, j: (i, 0)),
            pl.BlockSpec((1, d), lambda i, j: (0, 0)),
            pl.BlockSpec(memory_space=pl.ANY),
        ],
        out_specs=[pl.BlockSpec((tm, D_A), lambda i, j: (i, j)), pl.BlockSpec((tm, D_A), lambda i, j: (i, j // 2))],
        out_shape=[jax.ShapeDtypeStruct((m, IN_AB), BF16), jax.ShapeDtypeStruct((m, 3 * D_A), F32)],
        scratch_shapes=[
            pltpu.VMEM((tm, d), BF16),
            pltpu.VMEM((W_SLOTS, d, D_A), BF16),
            pltpu.SemaphoreType.DMA((W_SLOTS,)),
        ],
        compiler_params=_cparams(("arbitrary", "arbitrary")),
        name="norm_in_proj",
    )(x, g, w)


def _dwconv_rows(zext_ref, base, rows, lane0, w_ref, b_ref):
    lanes = slice(lane0, lane0 + LANES)
    acc = jnp.broadcast_to(b_ref[...], (rows, LANES))
    for phase in range(SUBLANES):
        win_rows = rows if phase == 0 else rows + SUBLANES
        part = None
        for k in range(CONV_C):
            off = base + CONV_HALO - (CONV_C - 1) + k
            if off % SUBLANES != phase:
                continue
            tap = w_ref[k]
            win = zext_ref[off - phase:off - phase + win_rows, lanes]
            term = win.reshape(win_rows // SUBLANES, SUBLANES, LANES) * tap[None]
            part = term if part is None else part + term
        if part is None:
            continue
        part = part.reshape(win_rows, LANES)
        if phase:
            part = pltpu.roll(part, win_rows - phase, 0)[:rows]
        acc = acc + part
    return acc


def _norm_glu_conv_kernel(x_ref, g_ref, wv_ref, wg_ref, wt_ref, st_ref, cw_ref, cb_ref,
                          cv_ref, sg_ref, tail_ref, xn_ref, zext_ref, carry_ref, *, seg, tiles_per_batch):
    i = pl.program_id(0)
    j = pl.program_id(1)
    n_seg = x_ref.shape[0] // seg
    ext = CONV_HALO + seg

    @pl.when(j == 0)
    def _():
        xn_ref[...] = _norm_to_bf16(x_ref[...], g_ref[...])

    xn = xn_ref[...]
    gate = jnp.dot(xn, wt_ref[...], preferred_element_type=F32)
    sg_ref[...] = _silu(gate).astype(BF16)
    val = jnp.dot(xn, wv_ref[...], preferred_element_type=F32)
    glu = jnp.dot(xn, wg_ref[...], preferred_element_type=F32)
    z = val * _sigmoid(glu)

    carried = tiles_per_batch > 1
    if carried:
        @pl.when(i % tiles_per_batch == 0)
        def _():
            carry_ref[j] = st_ref[0]

    rows = min(seg, CONV_ROWS)
    for s in range(n_seg):
        base = s * ext
        zext_ref[base:base + CONV_HALO, :] = carry_ref[j] if carried else st_ref[s]
        zext_ref[base + CONV_HALO:base + ext, :] = z[s * seg:(s + 1) * seg]
        for c in range(cw_ref.shape[0]):
            for r0 in range(0, seg, rows):
                out_rows = slice(s * seg + r0, s * seg + r0 + rows)
                cv_ref[out_rows, c * LANES:(c + 1) * LANES] = _dwconv_rows(
                    zext_ref, base + r0, rows, c * LANES, cw_ref.at[c], cb_ref.at[c])
        tail = zext_ref[base + seg:base + ext, :]
        tail_ref[s] = tail
        if carried:
            carry_ref[j] = tail


def _norm_glu_conv(x, g, w, state, dw_w, dw_b, *, t, tm, tn):
    m, d = x.shape
    nb = D_C // tn
    seg = min(t, tm)
    assert tm % seg == 0 and t % seg == 0 and seg % min(seg, CONV_ROWS) == 0 and seg % SUBLANES == 0
    tiles_per_batch = t // seg
    dw_w = dw_w.reshape(CONV_C, D_C // LANES, 1, LANES).transpose(1, 0, 2, 3)
    dw_w = jnp.broadcast_to(dw_w, (D_C // LANES, CONV_C, SUBLANES, LANES))
    dw_b = dw_b.reshape(D_C // LANES, 1, LANES)
    bpt = tm // seg if tiles_per_batch == 1 else 1
    batch_blk = (lambda i: i // tiles_per_batch) if tiles_per_batch > 1 else (lambda i: i)
    state_spec = pl.BlockSpec((bpt, CONV_HALO, tn), lambda i, j: (batch_blk(i), 0, j))
    body = functools.partial(_norm_glu_conv_kernel, seg=seg, tiles_per_batch=tiles_per_batch)
    return pl.pallas_call(
        body,
        grid=(m // tm, nb),
        in_specs=[
            pl.BlockSpec((tm, d), lambda i, j: (i, 0)),
            pl.BlockSpec((1, d), lambda i, j: (0, 0)),
            pl.BlockSpec((d, tn), lambda i, j: (0, j)),
            pl.BlockSpec((d, tn), lambda i, j: (0, j + nb)),
            pl.BlockSpec((d, tn), lambda i, j: (0, j + 2 * nb)),
            state_spec,
            pl.BlockSpec((tn // LANES, CONV_C, SUBLANES, LANES), lambda i, j: (j, 0, 0, 0)),
            pl.BlockSpec((tn // LANES, 1, LANES), lambda i, j: (j, 0, 0)),
        ],
        out_specs=[
            pl.BlockSpec((tm, tn), lambda i, j: (i, j)),
            pl.BlockSpec((tm, tn), lambda i, j: (i, j)),
            pl.BlockSpec((tm // seg, CONV_HALO, tn), lambda i, j: (i, 0, j)),
        ],
        out_shape=[
            jax.ShapeDtypeStruct((m, D_C), F32),
            jax.ShapeDtypeStruct((m, D_C), BF16),
            jax.ShapeDtypeStruct((m // seg, CONV_HALO, D_C), F32),
        ],
        scratch_shapes=[
            pltpu.VMEM((tm, d), BF16),
            pltpu.VMEM(((tm // seg) * (CONV_HALO + seg), tn), F32),
            pltpu.VMEM((nb, CONV_HALO, tn), F32),
        ],
        compiler_params=_cparams(("arbitrary", "arbitrary")),
        name="norm_in_proj_glu_conv",
    )(x, g, w, w, w, state, dw_w, dw_b)


def _attend_heads(q_ref, kcat_ref, vcat_ref, bias_ref, o_ref, q_rows, k_rows, kpos0):
    heads = [slice(h * HEAD_DIM, (h + 1) * HEAD_DIM) for h in range(N_HEADS)]
    scores = []
    for cols in heads:
        q = q_ref[q_rows, cols].astype(BF16)
        scores.append(lax.dot_general(q, kcat_ref[k_rows, cols], (((1,), (1,)), ((), ())),
                                      preferred_element_type=F32))
    if kpos0 is not None:
        col = lax.broadcasted_iota(jnp.int32, (1, scores[0].shape[1]), 1)
        before_start = jnp.where(col + kpos0 >= 0, 0.0, -jnp.inf)
    probs = []
    for h, s in enumerate(scores):
        s = s * (HEAD_DIM ** -0.5) + bias_ref[h]
        if kpos0 is not None:
            s = s + before_start
        m = jnp.max(s, axis=-1, keepdims=True)
        p = jnp.exp(s - m)
        probs.append((p.astype(BF16), jnp.sum(p, axis=-1, keepdims=True)))
    for cols, (p, l) in zip(heads, probs):
        o = jnp.dot(p, vcat_ref[k_rows, cols], preferred_element_type=F32) / l
        o_ref[q_rows, cols] = o.astype(o_ref.dtype)


def _heads_to_rows(x_ref):
    heads = [x_ref[:, h * HEAD_DIM:(h + 1) * HEAD_DIM] for h in range(N_HEADS)]
    return jnp.swapaxes(jnp.stack(heads, axis=0), 0, 1)


def _expand_rel_bias(fvec_ref, bias_ref, n_valid, chunk_masks):
    tq = bias_ref.shape[1]
    i = lax.broadcasted_iota(jnp.int32, (tq, K_WIN), 0)
    j = lax.broadcasted_iota(jnp.int32, (tq, K_WIN), 1)
    if chunk_masks:
        lo = jnp.where(i < CHUNK, 0, CHUNK)
        hi = jnp.where(i < CHUNK, min(BAND_PAST + CHUNK, n_valid), n_valid)
        hidden = (j < lo) | (j >= hi)
    else:
        hidden = j >= n_valid
    for h in range(N_HEADS):
        rows = jnp.broadcast_to(fvec_ref[h:h + 1, :], (tq, Q_TILE + K_WIN))
        skew = pltpu.roll(rows, 0, 1, stride=1, stride_axis=0)
        bias_ref[h] = jnp.where(hidden, -jnp.inf, skew[:, Q_TILE:])


def _attn_prompt_kernel(q_ref, kp_ref, kc_ref, vp_ref, vc_ref, kf_ref, vf_ref, fvec_ref, o_ref, nk_ref, nv_ref,
                        kcat_ref, vcat_ref, bias_ref):
    b = pl.program_id(0)
    t = pl.program_id(1)

    @pl.when((b == 0) & (t == 0))
    def _():
        _expand_rel_bias(fvec_ref, bias_ref, K_WIN, True)

    @pl.when(t == pl.num_programs(1) - 1)
    def _():
        nk_ref[...] = _heads_to_rows(kf_ref)
        nv_ref[...] = _heads_to_rows(vf_ref)

    kcat_ref[0:ATT_BLOCK, :] = kp_ref[...]
    kcat_ref[ATT_BLOCK:, :] = kc_ref[...]
    vcat_ref[0:ATT_BLOCK, :] = vp_ref[...]
    vcat_ref[ATT_BLOCK:, :] = vc_ref[...]
    for r0 in range(0, ATT_BLOCK, Q_TILE):
        kpos0 = t * ATT_BLOCK + r0 - BAND_PAST
        _attend_heads(q_ref, kcat_ref, vcat_ref, bias_ref, o_ref,
                      slice(r0, r0 + Q_TILE), slice(r0, r0 + K_WIN), kpos0)


def _attn_prompt(u16, u32, fvec):
    b, s, _ = u16.shape
    blk = (None, ATT_BLOCK, D_A)
    last = s // ATT_BLOCK - 1
    prev = lambda col: (lambda bi, t: (bi, jnp.maximum(t - 1, 0), col))
    cur = lambda col: (lambda bi, t: (bi, t, col))
    newest = lambda col: (lambda bi, t: (bi, last, col))
    state = pl.BlockSpec((None, ATT_BLOCK, N_HEADS, HEAD_DIM), lambda bi, t: (bi, 0, 0, 0))
    state_shape = jax.ShapeDtypeStruct((b, ATT_BLOCK, N_HEADS, HEAD_DIM), F32)
    return pl.pallas_call(
        _attn_prompt_kernel,
        grid=(b, s // ATT_BLOCK),
        in_specs=[
            pl.BlockSpec(blk, cur(U_Q)),
            pl.BlockSpec(blk, prev(U_K)),
            pl.BlockSpec(blk, cur(U_K)),
            pl.BlockSpec(blk, prev(U_V)),
            pl.BlockSpec(blk, cur(U_V)),
            pl.BlockSpec(blk, newest(U32_K)),
            pl.BlockSpec(blk, newest(U32_V)),
            pl.BlockSpec(fvec.shape, lambda bi, t: (0, 0)),
        ],
        out_specs=[pl.BlockSpec(blk, cur(0)), state, state],
        out_shape=[jax.ShapeDtypeStruct((b, s, D_A), BF16), state_shape, state_shape],
        scratch_shapes=[
            pltpu.VMEM((2 * ATT_BLOCK, D_A), BF16),
            pltpu.VMEM((2 * ATT_BLOCK, D_A), BF16),
            pltpu.VMEM((N_HEADS, Q_TILE, K_WIN), F32),
        ],
        compiler_params=_cparams(("arbitrary", "arbitrary")),
        name="attn_prompt",
    )(u16, u16, u16, u16, u16, u32, u32, fvec)


def _attn_sample_kernel(q_ref, kn_ref, vn_ref, kc_ref, vc_ref, fvec_ref, o_ref, nk_ref, nv_ref,
                        kcat_ref, vcat_ref, bias_ref):
    n_streams, tq, _ = q_ref.shape
    lc = kc_ref.shape[1]

    @pl.when(pl.program_id(0) == 0)
    def _():
        _expand_rel_bias(fvec_ref, bias_ref, lc + tq, False)
        kcat_ref[lc + tq:, :] = jnp.zeros((K_WIN - lc - tq, D_A), BF16)
        vcat_ref[lc + tq:, :] = jnp.zeros((K_WIN - lc - tq, D_A), BF16)

    for s in range(n_streams):
        kcat_ref[lc:lc + tq, :] = kn_ref[s].astype(BF16)
        vcat_ref[lc:lc + tq, :] = vn_ref[s].astype(BF16)
        nk_ref[s] = _heads_to_rows(kn_ref.at[s])
        nv_ref[s] = _heads_to_rows(vn_ref.at[s])
        kch = jnp.swapaxes(kc_ref[s], 0, 1)
        vch = jnp.swapaxes(vc_ref[s], 0, 1)
        for h in range(N_HEADS):
            c0 = h * HEAD_DIM
            kcat_ref[0:lc, c0:c0 + HEAD_DIM] = kch[h].astype(BF16)
            vcat_ref[0:lc, c0:c0 + HEAD_DIM] = vch[h].astype(BF16)
        _attend_heads(q_ref.at[s], kcat_ref, vcat_ref, bias_ref, o_ref.at[s], slice(None), slice(None), None)


def _attn_sample(u16, u32, k_cache, v_cache, fvec):
    b, t, _ = u16.shape
    lc = k_cache.shape[1]
    assert lc + t <= K_WIN and t <= Q_TILE
    ns = SAMPLE_STREAMS_PER_STEP if b % SAMPLE_STREAMS_PER_STEP == 0 else 1
    new = lambda col: pl.BlockSpec((ns, t, D_A), lambda bi: (bi, 0, col))
    cache = pl.BlockSpec((ns, lc, N_HEADS, HEAD_DIM), lambda bi: (bi, 0, 0, 0))
    state = pl.BlockSpec((ns, t, N_HEADS, HEAD_DIM), lambda bi: (bi, 0, 0, 0))
    state_shape = jax.ShapeDtypeStruct((b, t, N_HEADS, HEAD_DIM), F32)
    return pl.pallas_call(
        _attn_sample_kernel,
        grid=(b // ns,),
        in_specs=[new(U_Q), new(U32_K), new(U32_V), cache, cache, pl.BlockSpec(fvec.shape, lambda bi: (0, 0))],
        out_specs=[pl.BlockSpec((ns, t, D_A), lambda bi: (bi, 0, 0)), state, state],
        out_shape=[jax.ShapeDtypeStruct((b, t, D_A), BF16), state_shape, state_shape],
        scratch_shapes=[
            pltpu.VMEM((K_WIN, D_A), BF16),
            pltpu.VMEM((K_WIN, D_A), BF16),
            pltpu.VMEM((N_HEADS, t, K_WIN), F32),
        ],
        compiler_params=_cparams(("arbitrary",)),
        name="attn_sample",
    )(u16, u32, u32, k_cache, v_cache, fvec)


def _rel_bias_vector(rel_table):
    width = Q_TILE + K_WIN
    n_const = BAND_PAST + Q_TILE - REL_CLIP + 1
    n_rev = width - n_const
    assert 0 < n_rev <= 2 * REL_CLIP
    const = jnp.broadcast_to(rel_table[:, 2 * REL_CLIP:], (N_HEADS, n_const))
    rev = lax.rev(rel_table[:, 2 * REL_CLIP - n_rev:2 * REL_CLIP], (1,))
    return jnp.concatenate([const, rev], axis=1).astype(F32)


def _log_sigmoid(x):
    return -(jnp.maximum(-x, 0.0) + jnp.log1p(jnp.exp(-jnp.abs(x))))


def _shift_rows(x, d, fill, block):
    n, c = x.shape
    if d % SUBLANES == 0:
        x3 = x.reshape(n // block, block, c)
        pad = jnp.full((n // block, d, c), fill, x.dtype)
        return jnp.concatenate([pad, x3[:, :block - d]], axis=1).reshape(n, c)
    rolled = pltpu.roll(x, d, 0)
    row = lax.broadcasted_iota(jnp.int32, x.shape, 0)
    return jnp.where(row % block < d, fill, rolled)


def _rglru_kernel(xb_ref, lb_ref, h0_ref, cw_ref, cb_ref, wa_ref, ba_ref, wx_ref, bx_ref, lam_ref,
                  h_ref, hl_ref, xext_ref, hc_ref):
    t = pl.program_id(1)
    tt = xb_ref.shape[0]

    @pl.when(t == 0)
    def _():
        xext_ref[0:LRU_HALO, :] = lb_ref[...]
        hc_ref[...] = h0_ref[...]

    xext_ref[LRU_HALO:, :] = xb_ref[...]
    xc = cb_ref[...] + cw_ref[CONV_B - 1:CONV_B, :] * xb_ref[...]
    win = xext_ref[...]
    for k in range(CONV_B - 1):
        off = LRU_HALO - (CONV_B - 1) + k
        xc = xc + pltpu.roll(cw_ref[k:k + 1, :] * win, tt + LRU_HALO - off, 0)[:tt]
    xext_ref[0:LRU_HALO, :] = xext_ref[tt:tt + LRU_HALO, :]

    xcb = xc.astype(BF16)
    ra, rx = [], []
    for n in range(N_BLOCKS_B):
        blk = xcb[:, n * BLOCK_B:(n + 1) * BLOCK_B]
        ra.append(jnp.dot(blk, wa_ref[n], preferred_element_type=F32))
        rx.append(jnp.dot(blk, wx_ref[n], preferred_element_type=F32))
    r = _sigmoid(jnp.concatenate(ra, axis=-1) + ba_ref[...])
    i = _sigmoid(jnp.concatenate(rx, axis=-1) + bx_ref[...])
    log_a = LRU_C * r * _log_sigmoid(lam_ref[...])
    a = jnp.exp(log_a)
    th = jnp.tanh(log_a)
    bt = jnp.sqrt(-2.0 * th / (1.0 - th)) * (i * xc)

    block = min(tt, LRU_SCAN_ROWS)
    d = 1
    while d < block:
        bt = bt + a * _shift_rows(bt, d, 0.0, block)
        a = a * _shift_rows(a, d, 1.0, block)
        d *= 2
    carry = hc_ref[...]
    for r0 in range(0, tt, block):
        h = a[r0:r0 + block] * carry + bt[r0:r0 + block]
        h_ref[r0:r0 + block, :] = h.astype(h_ref.dtype)
        carry = h[block - 1:block, :]
    hc_ref[...] = carry
    hl_ref[...] = carry


def _rglru(u32, lb0, h0, cw, cb, w_a, b_a, w_x, b_x, lam, *, tt):
    b, t, _ = u32.shape
    row = lambda a: a.reshape(1, D_B)
    full = lambda a: pl.BlockSpec(a.shape, lambda bi, ti: (0,) * a.ndim)
    args = (cw, row(cb), w_a, row(b_a), w_x, row(b_x), row(lam))
    return pl.pallas_call(
        _rglru_kernel,
        grid=(b, t // tt),
        in_specs=[
            pl.BlockSpec((None, tt, D_B), lambda bi, ti: (bi, ti, U32_XB)),
            pl.BlockSpec((None, LRU_HALO, D_B), lambda bi, ti: (bi, 0, 0)),
            pl.BlockSpec((None, 1, D_B), lambda bi, ti: (bi, 0, 0)),
        ] + [full(a) for a in args],
        out_specs=[
            pl.BlockSpec((None, tt, D_B), lambda bi, ti: (bi, ti, 0)),
            pl.BlockSpec((None, 1, D_B), lambda bi, ti: (bi, 0, 0)),
        ],
        out_shape=[jax.ShapeDtypeStruct((b, t, D_B), BF16), jax.ShapeDtypeStruct((b, 1, D_B), F32)],
        scratch_shapes=[pltpu.VMEM((LRU_HALO + tt, D_B), F32), pltpu.VMEM((1, D_B), F32)],
        compiler_params=_cparams(("parallel", "arbitrary")),
        name="rglru",
    )(u32, lb0, h0, *args)


def _ring_row_tile(hbm_ref, buf_ref, sem_ref):
    step = pl.program_id(0)
    n_steps = pl.num_programs(0)
    tm = buf_ref.shape[1]

    def copy(s):
        slot = lax.rem(s, W_SLOTS)
        rows = pl.ds(pl.multiple_of(s * tm, tm), tm)
        return pltpu.make_async_copy(hbm_ref.at[rows, :], buf_ref.at[slot], sem_ref.at[slot])

    @pl.when(step == 0)
    def _():
        for s in range(W_SLOTS - 1):
            copy(s).start()

    @pl.when(step + W_SLOTS - 1 < n_steps)
    def _():
        copy(step + W_SLOTS - 1).start()

    copy(step).wait()
    return buf_ref.at[lax.rem(step, W_SLOTS)]


def _gated_out_kernel(oa_ref, ga_ref, h_ref, gb_ref, x_hbm, w_ref, o_ref, xbuf_ref, xsem_ref):
    x_ref = _ring_row_tile(x_hbm, xbuf_ref, xsem_ref)
    ma = (_silu(ga_ref[...].astype(F32)) * oa_ref[...].astype(F32)).astype(BF16)
    mb = (_silu(gb_ref[...].astype(F32)) * h_ref[...].astype(F32)).astype(BF16)
    acc = jnp.dot(ma, w_ref[0:D_A, :], preferred_element_type=F32)
    acc = acc + jnp.dot(mb, w_ref[D_A:, :], preferred_element_type=F32)
    o_ref[...] = x_ref[...] + acc


def _gated_out(o_a, u16, h, x, w, *, tm):
    m = x.shape[0]
    assert m // tm >= W_SLOTS - 1
    return pl.pallas_call(
        _gated_out_kernel,
        grid=(m // tm,),
        in_specs=[
            pl.BlockSpec((tm, D_A), lambda i: (i, 0)),
            pl.BlockSpec((tm, D_A), lambda i: (i, U_GA)),
            pl.BlockSpec((tm, D_B), lambda i: (i, 0)),
            pl.BlockSpec((tm, D_B), lambda i: (i, U_GB)),
            pl.BlockSpec(memory_space=pl.ANY),
            pl.BlockSpec(w.shape, lambda i: (0, 0), pipeline_mode=pl.Buffered(1)),
        ],
        out_specs=pl.BlockSpec((tm, D_MODEL), lambda i: (i, 0)),
        out_shape=jax.ShapeDtypeStruct((m, D_MODEL), F32),
        scratch_shapes=[pltpu.VMEM((W_SLOTS, tm, D_MODEL), F32), pltpu.SemaphoreType.DMA((W_SLOTS,))],
        compiler_params=_cparams(("arbitrary",)),
        name="gated_out_proj",
    )(o_a, u16, h, u16, x, w)


def _ln_out_norm_kernel(cv_hbm, sg_ref, lg_ref, lb_ref, w_ref, x_hbm, g_ref, o_ref,
                        cvbuf_ref, cvsem_ref, xbuf_ref, xsem_ref):
    cv_ref = _ring_row_tile(cv_hbm, cvbuf_ref, cvsem_ref)
    x_ref = _ring_row_tile(x_hbm, xbuf_ref, xsem_ref)
    acc = cv_ref[...]
    mu = jnp.mean(acc, axis=-1, keepdims=True)
    cen = acc - mu
    var = jnp.mean(cen * cen, axis=-1, keepdims=True)
    yn = cen * lax.rsqrt(var + EPS) * lg_ref[...] + lb_ref[...]
    y = (_silu(yn) * sg_ref[...].astype(F32)).astype(BF16)
    x = x_ref[...] + jnp.dot(y, w_ref[...], preferred_element_type=F32)
    ms = jnp.mean(x * x, axis=-1, keepdims=True)
    o_ref[...] = x * lax.rsqrt(ms + EPS) * g_ref[...]


def _ln_out_norm(cv, sg, ln_g, ln_b, w, x, g, *, tm):
    m = x.shape[0]
    assert m // tm >= W_SLOTS - 1
    row = pl.BlockSpec((1, D_C), lambda i: (0, 0))
    return pl.pallas_call(
        _ln_out_norm_kernel,
        grid=(m // tm,),
        in_specs=[
            pl.BlockSpec(memory_space=pl.ANY),
            pl.BlockSpec((tm, D_C), lambda i: (i, 0)),
            row,
            row,
            pl.BlockSpec(w.shape, lambda i: (0, 0), pipeline_mode=pl.Buffered(1)),
            pl.BlockSpec(memory_space=pl.ANY),
            pl.BlockSpec((1, D_MODEL), lambda i: (0, 0)),
        ],
        out_specs=pl.BlockSpec((tm, D_MODEL), lambda i: (i, 0)),
        out_shape=jax.ShapeDtypeStruct((m, D_MODEL), F32),
        scratch_shapes=[
            pltpu.VMEM((W_SLOTS, tm, D_C), F32),
            pltpu.SemaphoreType.DMA((W_SLOTS,)),
            pltpu.VMEM((W_SLOTS, tm, D_MODEL), F32),
            pltpu.SemaphoreType.DMA((W_SLOTS,)),
        ],
        compiler_params=_cparams(("arbitrary",)),
        name="ln_out_proj_final_norm",
    )(cv, sg, ln_g.reshape(1, D_C), ln_b.reshape(1, D_C), w, x, g)


def _front_pad_rows(a, rows):
    return jnp.pad(a, ((0, 0), (rows - a.shape[1], 0), (0, 0)))


def _trunk(x, caches, w, *, tm, tm_out, lru_tt):
    b, t, _ = x.shape
    m = b * t
    x2d = x.reshape(m, D_MODEL)

    u16, u32 = _norm_matmul(x2d, w["norm_ab"], w["w_in_ab"], tm=tm)
    u16_3 = u16.reshape(b, t, IN_AB)
    u32_3 = u32.reshape(b, t, 3 * D_A)
    if caches is None:
        assert t % ATT_BLOCK == 0 and ATT_BLOCK == BAND_PAST
        o_a, new_k, new_v = _attn_prompt(u16_3, u32_3, w["rel_bias_vec"])
        h0 = jnp.zeros((b, 1, D_B), F32)
        lb0 = jnp.zeros((b, LRU_HALO, D_B), F32)
        cb0 = jnp.zeros((b, CONV_HALO, D_C), F32)
    else:
        kc, vc, hc, lbc, cbc = caches
        o_a, new_k, new_v = _attn_sample(u16_3, u32_3, kc, vc, w["rel_bias_vec"])
        h0 = hc.reshape(b, 1, D_B)
        lb0 = _front_pad_rows(lbc, LRU_HALO)
        cb0 = _front_pad_rows(cbc, CONV_HALO)
    h, h_last = _rglru(u32_3, lb0, h0, w["lru_conv_w"], w["lru_conv_b"], w["lru_w_a"], w["lru_b_a"],
                       w["lru_w_x"], w["lru_b_x"], w["lru_lambda"], tt=lru_tt)
    x1 = _gated_out(o_a.reshape(m, D_A), u16, h.reshape(m, D_B), x2d, w["w_out_ab"], tm=tm_out)

    new_h = h_last[:, 0]
    new_lb = u32_3[:, t - (CONV_B - 1):, U32_XB * D_B:(U32_XB + 1) * D_B]

    cv, sg, z_tail = _norm_glu_conv(x1, w["norm_cv"], w["w_in_cv"], cb0, w["dw_w"], w["dw_b"], t=t, tm=tm, tn=512)
    out = _ln_out_norm(cv, sg, w["ln_g"], w["ln_b"], w["w_out_cv"], x1, w["final_norm"], tm=tm_out)
    segs_per_batch = z_tail.shape[0] // b
    new_cb = z_tail[segs_per_batch - 1::segs_per_batch, CONV_HALO - (CONV_C - 1):]

    return (out.reshape(b, t, D_MODEL), new_k[None], new_v[None], new_h[None], new_lb[None], new_cb[None])


def kernel(x_prompt, x_sample, cache_attn_k, cache_attn_v, state_lru_h, state_lru_conv, state_conv, norm_ab, w_in_ab, w_out_ab, rel_bias, lru_conv_w, lru_conv_b, lru_w_a, lru_b_a, lru_w_x, lru_b_x, lru_lambda, norm_cv, w_in_cv, w_out_cv, dw_w, dw_b, ln_g, ln_b, final_norm):
    assert norm_ab.shape[0] == 1 and norm_cv.shape[0] == 1, "one even and one odd layer"
    t_s = x_sample.shape[1]
    w = {
        "norm_ab": norm_ab[0].reshape(1, D_MODEL),
        "w_in_ab": w_in_ab[0].astype(BF16),
        "w_out_ab": w_out_ab[0].astype(BF16),
        "rel_bias_vec": _rel_bias_vector(rel_bias[0]),
        "lru_conv_w": lru_conv_w[0],
        "lru_conv_b": lru_conv_b[0],
        "lru_w_a": lru_w_a[0].astype(BF16),
        "lru_b_a": lru_b_a[0],
        "lru_w_x": lru_w_x[0].astype(BF16),
        "lru_b_x": lru_b_x[0],
        "lru_lambda": lru_lambda[0],
        "norm_cv": norm_cv[0].reshape(1, D_MODEL),
        "w_in_cv": w_in_cv[0].astype(BF16),
        "w_out_cv": w_out_cv[0].astype(BF16),
        "dw_w": dw_w[0],
        "dw_b": dw_b[0],
        "ln_g": ln_g[0],
        "ln_b": ln_b[0],
        "final_norm": final_norm.reshape(1, D_MODEL),
    }
    y_p, k_p, v_p, h_p, lb_p, cb_p = _trunk(x_prompt, None, w, tm=1024, tm_out=512, lru_tt=512)
    caches = (cache_attn_k[0], cache_attn_v[0], state_lru_h[0], state_lru_conv[0], state_conv[0])
    y_s, k_s, v_s, h_s, lb_s, cb_s = _trunk(x_sample, caches, w, tm=512, tm_out=256, lru_tt=t_s)
    return (y_p, y_s, k_p, v_p, h_p, lb_p, cb_p, k_s, v_s, h_s, lb_s, cb_s)
```
